```python
import math
import jax, jax.numpy as jnp
from jax import lax
import numpy as np

D_MODEL = 1024
BATCH = 2
SEQ = 16384
DEPTH = 1
DEC_BATCH = 128
DEC_SEQ = 4
PAST_LEN = 8192
PAGE_SIZE = 128

N_HEADS = 8
HEAD_DIM = 64
ATTN_DIM = N_HEADS * HEAD_DIM
MOBA_BLOCK = 256
MOBA_TOPK = 3
MOBA_Q_CHUNK = 64
REL_BUCKETS = 32
REL_MAX_DIST = 128
D_INNER = D_MODEL
SSM_HEAD_DIM = 64
SSM_HEADS = D_INNER // SSM_HEAD_DIM
N_GROUPS = 4
D_STATE = 128
SSM_CONV = 4
SSM_CHUNK = 256
CONV_DIM = D_INNER + 2 * N_GROUPS * D_STATE
D_FF = ((8 * D_MODEL // 3 + 255) // 256) * 256
FFN_CONV = 3
IN_DIM = 3 * ATTN_DIM + D_INNER + CONV_DIM + SSM_HEADS + 2 * D_MODEL
ALPHA = (2 * DEPTH) ** 0.25
BETA = (8 * DEPTH) ** -0.25
EPS = 1e-5

kernel_name = 'moba_ssd_hybrid_convffn_decode_step'


def layer_norm(x, g, b):
    xf = x.astype(jnp.float32)
    mu = jnp.mean(xf, -1, keepdims=True)
    var = jnp.mean(jnp.square(xf - mu), -1, keepdims=True)
    return ((xf - mu) * lax.rsqrt(var + EPS) * g.astype(jnp.float32) + b.astype(jnp.float32)).astype(x.dtype)


def group_rms_norm(y, w):
    bn, l, _ = y.shape
    yg = y.reshape(bn, l, N_GROUPS, D_INNER // N_GROUPS)
    yg = yg * lax.rsqrt(jnp.mean(jnp.square(yg), -1, keepdims=True) + EPS)
    return yg.reshape(bn, l, D_INNER) * w.astype(jnp.float32)


def causal_dwconv(x_ext, w, b):
    c = x_ext.shape[-1]
    y = lax.conv_general_dilated(x_ext, w[:, None, :].astype(x_ext.dtype), window_strides=(1,), padding='VALID',
                                 dimension_numbers=('NWC', 'WIO', 'NWC'), feature_group_count=c)
    return y + b.astype(x_ext.dtype)


def rel_bucket(dist):
    n = jnp.maximum(dist, 0)
    max_exact = REL_BUCKETS // 2
    nf = jnp.maximum(n, 1).astype(jnp.float32)
    large = max_exact + (jnp.log(nf / max_exact) / math.log(REL_MAX_DIST / max_exact)
                         * (REL_BUCKETS - max_exact)).astype(jnp.int32)
    large = jnp.minimum(large, REL_BUCKETS - 1)
    return jnp.where(n < max_exact, n, large)


def moba_sequence(q, k, v, q_offset, rel_table):
    lq, h, dh = q.shape
    lk = k.shape[0]
    n_full = lk // MOBA_BLOCK
    nb = -(-lk // MOBA_BLOCK)
    pad = nb * MOBA_BLOCK - lk
    kb = jnp.pad(k, ((0, pad), (0, 0), (0, 0))).reshape(nb, MOBA_BLOCK, h, dh).transpose(2, 0, 1, 3)
    vb = jnp.pad(v, ((0, pad), (0, 0), (0, 0))).reshape(nb, MOBA_BLOCK, h, dh).transpose(2, 0, 1, 3)
    kmean = jnp.mean(kb[:, :n_full].astype(jnp.float32), axis=2)
    n_cols = max(n_full, MOBA_TOPK)
    qc = MOBA_Q_CHUNK if lq % MOBA_Q_CHUNK == 0 else lq
    scale = HEAD_DIM ** -0.5
    hi = jnp.arange(h)[None, :, None]

    def chunk(ci):
        qs = lax.dynamic_slice_in_dim(q, ci * qc, qc, 0)
        pos = q_offset + ci * qc + jnp.arange(qc, dtype=jnp.int32)
        cur = pos // MOBA_BLOCK
        s = jnp.einsum('qhd,hnd->qhn', qs.astype(jnp.float32), kmean)
        s = jnp.pad(s, ((0, 0), (0, 0), (0, n_cols - n_full)), constant_values=-jnp.inf)
        blk = jnp.arange(n_cols, dtype=jnp.int32)
        s = jnp.where(blk[None, None, :] < cur[:, None, None], s, -jnp.inf)
        _, top = lax.top_k(s, MOBA_TOPK)
        valid = top < cur[:, None, None]
        cur_b = jnp.broadcast_to(cur[:, None, None], (qc, h, 1))
        sel = jnp.concatenate([jnp.where(valid, top, cur_b), cur_b], -1)
        sel_ok = jnp.concatenate([valid, jnp.ones((qc, h, 1), bool)], -1)
        kg = kb[hi, sel]
        vg = vb[hi, sel]
        kpos = sel[..., None] * MOBA_BLOCK + jnp.arange(MOBA_BLOCK, dtype=jnp.int32)
        dist = pos[:, None, None, None] - kpos
        ok = sel_ok[..., None] & (dist >= 0)
        bias = rel_table[rel_bucket(dist), hi[..., None]].astype(jnp.float32)
        logits = jnp.einsum('qhd,qhsbd->qhsb', qs, kg).astype(jnp.float32) * scale + bias
        logits = jnp.where(ok, logits, -jnp.inf).reshape(qc, h, -1)
        p = jax.nn.softmax(logits, axis=-1).astype(v.dtype).reshape(kg.shape[:4])
        return jnp.einsum('qhsb,qhsbd->qhd', p, vg)

    out = lax.map(chunk, jnp.arange(lq // qc, dtype=jnp.int32))
    return out.reshape(lq, h, dh)


def ssd_chunked(x, dt, a, bm, cm, state0):
    b, L, H, P = x.shape
    rep = H // bm.shape[2]
    l = math.gcd(L, SSM_CHUNK)
    nc = L // l
    f32 = jnp.float32

    def to_chunks(t):
        return jnp.moveaxis(t.astype(f32).reshape(b, nc, l, *t.shape[2:]), 1, 0)

    tril = jnp.tril(jnp.ones((l, l), bool))

    def step(state, inp):
        xc, dtc, bc, cc = inp
        bh = jnp.repeat(bc, rep, axis=2)
        ch = jnp.repeat(cc, rep, axis=2)
        acs = jnp.cumsum(dtc * a, axis=1)
        seg = acs[:, :, None, :] - acs[:, None, :, :]
        lm = jnp.exp(jnp.where(tril[None, :, :, None], seg, -jnp.inf))
        xd = xc * dtc[..., None]
        cb = jnp.einsum('blhn,bshn->blsh', ch, bh)
        y = jnp.einsum('blsh,bshp->blhp', cb * lm, xd) + \
            jnp.einsum('blhn,bhpn->blhp', ch, state) * jnp.exp(acs)[..., None]
        dec = jnp.exp(acs[:, -1:, :] - acs)
        state = state * jnp.exp(acs[:, -1, :])[:, :, None, None] + \
            jnp.einsum('bshn,bshp->bhpn', bh * dec[..., None], xd)
        return state, y

    state, ys = lax.scan(step, state0.astype(f32), (to_chunks(x), to_chunks(dt), to_chunks(bm), to_chunks(cm)))
    return jnp.moveaxis(ys, 0, 1).reshape(b, L, H, P), state


def layer(x, c, conv_ssm_prev, ssm_prev, conv_ffn_prev, attend, p):
    bn, L, _ = x.shape
    mod = (jax.nn.silu(c) @ p['w_ada'] + p['b_ada']).reshape(bn, 6, D_MODEL)[:, :, None, :]
    shift1, scale1, gate1, shift2, scale2, gate2 = (mod[:, i] for i in range(6))
    u = x * (1 + scale1) + shift1
    proj = u @ p['w_in']
    cuts = [int(t) for t in np.cumsum([ATTN_DIM, ATTN_DIM, ATTN_DIM, D_INNER, CONV_DIM, SSM_HEADS, D_MODEL])]
    q, k, v, z, xbc, dt_raw, ga, gb = jnp.split(proj, cuts, axis=-1)
    q = q.reshape(bn, L, N_HEADS, HEAD_DIM)
    k = k.reshape(bn, L, N_HEADS, HEAD_DIM)
    v = v.reshape(bn, L, N_HEADS, HEAD_DIM)
    attn = attend(q, k, v).reshape(bn, L, ATTN_DIM)
    xbc_ext = jnp.concatenate([conv_ssm_prev.astype(xbc.dtype), xbc], axis=1)
    new_conv_ssm = xbc_ext[:, -(SSM_CONV - 1):]
    xbc = jax.nn.silu(causal_dwconv(xbc_ext, p['w_conv_ssm'], p['b_conv_ssm']))
    xs, bm, cm = jnp.split(xbc, [D_INNER, D_INNER + N_GROUPS * D_STATE], axis=-1)
    xs = xs.reshape(bn, L, SSM_HEADS, SSM_HEAD_DIM)
    bm = bm.reshape(bn, L, N_GROUPS, D_STATE)
    cm = cm.reshape(bn, L, N_GROUPS, D_STATE)
    dt = jax.nn.softplus(dt_raw.astype(jnp.float32) + p['dt_bias'].astype(jnp.float32))
    a = -jnp.exp(p['a_log'].astype(jnp.float32))
    y, new_ssm = ssd_chunked(xs, dt, a, bm, cm, ssm_prev)
    y = y + p['d_skip'].astype(jnp.float32)[:, None] * xs.astype(jnp.float32)
    y = y.reshape(bn, L, D_INNER) * jax.nn.silu(z.astype(jnp.float32))
    y = group_rms_norm(y, p['w_norm_ssm']).astype(x.dtype)
    merged = jax.nn.sigmoid(ga) * (attn @ p['w_proj_attn']) + jax.nn.sigmoid(gb) * (y @ p['w_proj_ssm'])
    x = layer_norm(ALPHA * x + gate1 * (merged @ p['w_out']), p['ln1_g'], p['ln1_b'])
    u2 = x * (1 + scale2) + shift2
    hup = u2 @ p['w_up']
    h_ext = jnp.concatenate([conv_ffn_prev.astype(hup.dtype), hup], axis=1)
    new_conv_ffn = h_ext[:, -(FFN_CONV - 1):]
    hc = causal_dwconv(h_ext, p['w_conv_ffn'], p['b_conv_ffn'])
    ha, hb = jnp.split(hc, 2, axis=-1)
    f = (jax.nn.silu(ha) * hb) @ p['w_down']
    x = layer_norm(ALPHA * x + gate2 * f, p['ln2_g'], p['ln2_b'])
    return x, k, v, new_ssm, new_conv_ssm, new_conv_ffn


def make_prompt_attend(rel_table):
    def attend(q, k, v):
        return lax.map(lambda t: moba_sequence(t[0], t[1], t[2], 0, rel_table), (q, k, v))
    return attend


def make_sample_attend(ck, cv, page_table, rel_table):
    past = page_table.shape[1] * ck.shape[1]

    def attend(q, k, v):
        def one(t):
            qb, kn, vn, pt = t
            kp = ck[pt].reshape(past, N_HEADS, HEAD_DIM)
            vp = cv[pt].reshape(past, N_HEADS, HEAD_DIM)
            kf = jnp.concatenate([kp.astype(kn.dtype), kn], axis=0)
            vf = jnp.concatenate([vp.astype(vn.dtype), vn], axis=0)
            return moba_sequence(qb, kf, vf, past, rel_table)
        return lax.map(one, (q, k, v, page_table))
    return attend


def setup_inputs(seed: int = 0) -> dict:
    key = jax.random.key(seed)
    ks = jax.random.split(key, 32)
    f32 = jnp.float32

    def nrm(i, shape, s):
        return jax.random.normal(ks[i], shape, f32) * s

    n_pages = PAST_LEN // PAGE_SIZE
    n_used = DEC_BATCH * n_pages
    n_pool = n_used + max(1, n_used // 4)
    page_table = jax.random.permutation(ks[0], n_pool)[:n_used].reshape(DEC_BATCH, n_pages).astype(jnp.int32)
    dt0 = jnp.exp(jax.random.uniform(ks[1], (DEPTH, SSM_HEADS), f32, math.log(1e-3), math.log(1e-1)))
    dt_bias = dt0 + jnp.log(-jnp.expm1(-dt0))
    a_log = jnp.log(jax.random.uniform(ks[2], (DEPTH, SSM_HEADS), f32, 1.0, 16.0))
    return {
        'x_prompt': nrm(3, (BATCH, SEQ, D_MODEL), 1.0),
        'x_sample': nrm(4, (DEC_BATCH, DEC_SEQ, D_MODEL), 1.0),
        'cache_k': nrm(5, (DEPTH, n_pool, PAGE_SIZE, N_HEADS, HEAD_DIM), 1.0),
        'cache_v': nrm(6, (DEPTH, n_pool, PAGE_SIZE, N_HEADS, HEAD_DIM), 1.0),
        'page_table': page_table,
        'state_ssm': nrm(7, (DEPTH, DEC_BATCH, SSM_HEADS, SSM_HEAD_DIM, D_STATE), 0.1),
        'state_conv_ssm': nrm(8, (DEPTH, DEC_BATCH, SSM_CONV - 1, CONV_DIM), 1.0),
        'state_conv_ffn': nrm(9, (DEPTH, DEC_BATCH, FFN_CONV - 1, 2 * D_FF), 1.0),
        'c_prompt': nrm(10, (BATCH, D_MODEL), 1.0),
        'c_sample': nrm(11, (DEC_BATCH, D_MODEL), 1.0),
        'rel_table': nrm(12, (REL_BUCKETS, N_HEADS), 0.2),
        'w_ada': nrm(13, (DEPTH, D_MODEL, 6 * D_MODEL), D_MODEL ** -0.5),
        'b_ada': nrm(14, (DEPTH, 6 * D_MODEL), 0.01),
        'w_in': nrm(15, (DEPTH, D_MODEL, IN_DIM), D_MODEL ** -0.5),
        'w_conv_ssm': nrm(16, (DEPTH, SSM_CONV, CONV_DIM), SSM_CONV ** -0.5),
        'b_conv_ssm': nrm(17, (DEPTH, CONV_DIM), 0.01),
        'dt_bias': dt_bias,
        'a_log': a_log,
        'd_skip': 1.0 + nrm(18, (DEPTH, SSM_HEADS), 0.01),
        'w_norm_ssm': 1.0 + nrm(19, (DEPTH, D_INNER), 0.01),
        'w_proj_attn': nrm(20, (DEPTH, ATTN_DIM, D_MODEL), ATTN_DIM ** -0.5),
        'w_proj_ssm': nrm(21, (DEPTH, D_INNER, D_MODEL), D_INNER ** -0.5),
        'w_out': nrm(22, (DEPTH, D_MODEL, D_MODEL), BETA * D_MODEL ** -0.5),
        'ln1_g': 1.0 + nrm(23, (DEPTH, D_MODEL), 0.01),
        'ln1_b': nrm(24, (DEPTH, D_MODEL), 0.01),
        'w_up': nrm(25, (DEPTH, D_MODEL, 2 * D_FF), D_MODEL ** -0.5),
        'w_conv_ffn': nrm(26, (DEPTH, FFN_CONV, 2 * D_FF), FFN_CONV ** -0.5),
        'b_conv_ffn': nrm(27, (DEPTH, 2 * D_FF), 0.01),
        'w_down': nrm(28, (DEPTH, D_FF, D_MODEL), BETA * D_FF ** -0.5),
        'ln2_g': 1.0 + nrm(29, (DEPTH, D_MODEL), 0.01),
        'ln2_b': nrm(30, (DEPTH, D_MODEL), 0.01),
    }


def reference(x_prompt, x_sample, cache_k, cache_v, page_table, state_ssm, state_conv_ssm, state_conv_ffn,
              c_prompt, c_sample, rel_table, w_ada, b_ada, w_in, w_conv_ssm, b_conv_ssm, dt_bias, a_log,
              d_skip, w_norm_ssm, w_proj_attn, w_proj_ssm, w_out, ln1_g, ln1_b, w_up, w_conv_ffn, b_conv_ffn,
              w_down, ln2_g, ln2_b):
    bp = x_prompt.shape[0]
    yp, ys = x_prompt, x_sample
    kp_l, vp_l, sp_l, csp_l, cfp_l = [], [], [], [], []
    ks_l, vs_l, ss_l, css_l, cfs_l = [], [], [], [], []
    attend_prompt = make_prompt_attend(rel_table)
    for l in range(DEPTH):
        p = {'w_ada': w_ada[l], 'b_ada': b_ada[l], 'w_in': w_in[l], 'w_conv_ssm': w_conv_ssm[l],
             'b_conv_ssm': b_conv_ssm[l], 'dt_bias': dt_bias[l], 'a_log': a_log[l], 'd_skip': d_skip[l],
             'w_norm_ssm': w_norm_ssm[l], 'w_proj_attn': w_proj_attn[l], 'w_proj_ssm': w_proj_ssm[l],
             'w_out': w_out[l], 'ln1_g': ln1_g[l], 'ln1_b': ln1_b[l], 'w_up': w_up[l],
             'w_conv_ffn': w_conv_ffn[l], 'b_conv_ffn': b_conv_ffn[l], 'w_down': w_down[l],
             'ln2_g': ln2_g[l], 'ln2_b': ln2_b[l]}
        yp, kp, vp, sp, csp, cfp = layer(
            yp, c_prompt,
            jnp.zeros((bp, SSM_CONV - 1, CONV_DIM), yp.dtype),
            jnp.zeros((bp, SSM_HEADS, SSM_HEAD_DIM, D_STATE), jnp.float32),
            jnp.zeros((bp, FFN_CONV - 1, 2 * D_FF), yp.dtype),
            attend_prompt, p)
        ys, kn, vn, sn, csn, cfn = layer(
            ys, c_sample, state_conv_ssm[l], state_ssm[l], state_conv_ffn[l],
            make_sample_attend(cache_k[l], cache_v[l], page_table, rel_table), p)
        kp_l.append(kp); vp_l.append(vp); sp_l.append(sp); csp_l.append(csp); cfp_l.append(cfp)
        ks_l.append(kn); vs_l.append(vn); ss_l.append(sn); css_l.append(csn); cfs_l.append(cfn)
    return (yp, ys,
            jnp.stack(kp_l), jnp.stack(vp_l), jnp.stack(sp_l), jnp.stack(csp_l), jnp.stack(cfp_l),
            jnp.stack(ks_l), jnp.stack(vs_l), jnp.stack(ss_l), jnp.stack(css_l), jnp.stack(cfs_l))
```

```python
import functools
import math

import jax
import jax.numpy as jnp
from jax import lax
from jax.experimental import pallas as pl
from jax.experimental.pallas import tpu as pltpu

N_HEADS = 8
HEAD_DIM = 64
ATTN_DIM = N_HEADS * HEAD_DIM
MOBA_BLOCK = 256
MOBA_TOPK = 3
REL_BUCKETS = 32
REL_MAX_DIST = 128
SSM_HEAD_DIM = 64
SSM_CHUNK = 256
EPS = 1e-5

LANES = 128
SUBLANES = 8
BF16_ROWS = 16
VMEM_LIMIT = 56 * 1024 * 1024

MASKED = -1e30
SAMPLE_ROWS = 16
SAMPLE_BLOCKS_PER_STEP = 4

f32 = jnp.float32
bf16 = jnp.bfloat16

_NT = (((1,), (1,)), ((), ()))
_TN = (((0,), (0,)), ((), ()))


def _dot(a, b):
    return jnp.dot(a, b, preferred_element_type=f32)


def _dot_nt(a, b):
    return lax.dot_general(a, b, _NT, preferred_element_type=f32)


def _dot_tn(a, b):
    return lax.dot_general(a, b, _TN, preferred_element_type=f32)


def _split3(x):
    hi = x.astype(bf16)
    r = x - hi.astype(f32)
    mid = r.astype(bf16)
    lo = (r - mid.astype(f32)).astype(bf16)
    return hi, mid, lo


def _dot_exact_rhs(a, m):
    hi, mid, lo = _split3(a)
    return _dot(hi, m) + _dot(mid, m) + _dot(lo, m)


def _dot_exact_lhs(m, a):
    hi, mid, lo = _split3(a)
    return _dot(m, hi) + _dot(m, mid) + _dot(m, lo)


def _silu(x):
    return x * jax.nn.sigmoid(x)


def _softplus(x):
    return jnp.maximum(x, 0.0) + jnp.log1p(jnp.exp(-jnp.abs(x)))


def _layer_norm(x, g, b):
    mu = jnp.mean(x, axis=-1, keepdims=True)
    xc = x - mu
    var = jnp.mean(xc * xc, axis=-1, keepdims=True)
    return xc * lax.rsqrt(var + EPS) * g + b


def _const_spec(shape):
    nd = len(shape)
    return pl.BlockSpec(shape, lambda *_: (0,) * nd, pipeline_mode=pl.Buffered(1))


def _params(n_grid):
    return pltpu.CompilerParams(
        dimension_semantics=("arbitrary",) * n_grid, vmem_limit_bytes=VMEM_LIMIT)


def _ada_body(c_ref, w_ref, b_ref, o_ref):
    a = _silu(c_ref[...])
    hi, mid, lo = _split3(a)
    whi, wmid, wlo = _split3(w_ref[...])
    acc = _dot(hi, whi) + (_dot(hi, wmid) + _dot(mid, whi))
    acc = acc + (_dot(hi, wlo) + _dot(mid, wmid) + _dot(lo, whi))
    o_ref[...] = acc + b_ref[...]


def _ada(c, w, b):
    n, d = c.shape
    dn = w.shape[1]
    tn = 1024 if dn % 1024 == 0 else dn
    return pl.pallas_call(
        _ada_body,
        grid=(dn // tn,),
        in_specs=[pl.BlockSpec((n, d), lambda j: (0, 0)),
                  pl.BlockSpec((d, tn), lambda j: (0, j)),
                  pl.BlockSpec((1, tn), lambda j: (0, j))],
        out_specs=pl.BlockSpec((n, tn), lambda j: (0, j)),
        out_shape=jax.ShapeDtypeStruct((n, dn), f32),
        compiler_params=_params(1),
        name="ada_mod",
    )(c, w, b.reshape(1, dn))


def _inproj_body(x_ref, sh_ref, sc_ref, wn_ref, wt_ref, *outs, nat, tr, k_col, tiles_per_seq):
    u = (x_ref[...] * (1.0 + sc_ref[...]) + sh_ref[...]).astype(bf16)
    tm = u.shape[0]
    o = 0
    col = 0
    for width, _ in nat:
        res = _dot(u, wn_ref[:, col:col + width])
        outs[o][...] = res.astype(outs[o].dtype)
        if k_col is not None and col == k_col:
            kf = res
        o += 1
        col += width
    if tr:
        t = _dot_nt(wt_ref[...], u)
        row = 0
        for height, _ in tr:
            outs[o][...] = t[row:row + height].astype(outs[o].dtype)
            o += 1
            row += height
    if k_col is not None:
        pos = (pl.program_id(0) % tiles_per_seq) * tm
        blk = pos // MOBA_BLOCK
        onehot = (lax.broadcasted_iota(jnp.int32, (tm, LANES), 1) == blk).astype(bf16)
        k2_ref = outs[o]
        for p in range(ATTN_DIM // LANES):
            k2_ref[p] = jnp.concatenate([kf[:, p * LANES:(p + 1) * LANES].astype(bf16), onehot], axis=1)


def _inproj(x2d, shift, scale, wn, wt, nat, tr, *, per_row_mod, seq_len, attn_layout):
    r, d = x2d.shape
    tm = 256
    assert r % tm == 0
    n_tiles = r // tm
    if per_row_mod:
        mod_spec = pl.BlockSpec((tm, d), lambda i: (i, 0))
        tiles_per_seq = 1
    else:
        assert seq_len % tm == 0 and MOBA_BLOCK % tm == 0
        tiles_per_seq = seq_len // tm
        mod_spec = pl.BlockSpec((None, 1, d), lambda i: (i // tiles_per_seq, 0, 0))
    out_shape, out_specs = [], []
    for width, dt in nat:
        out_shape.append(jax.ShapeDtypeStruct((r, width), dt))
        out_specs.append(pl.BlockSpec((tm, width), lambda i: (i, 0)))
    for height, dt in tr:
        out_shape.append(jax.ShapeDtypeStruct((height, r), dt))
        out_specs.append(pl.BlockSpec((height, tm), lambda i: (0, i)))
    k_col = None
    if attn_layout:
        k_col = 0
        n_pairs = ATTN_DIM // LANES
        out_shape.append(jax.ShapeDtypeStruct((n_pairs, r, 2 * LANES), bf16))
        out_specs.append(pl.BlockSpec((n_pairs, tm, 2 * LANES), lambda i: (0, i, 0)))
    body = functools.partial(_inproj_body, nat=tuple(nat), tr=tuple(tr), k_col=k_col,
                             tiles_per_seq=tiles_per_seq)
    return pl.pallas_call(
        body,
        grid=(n_tiles,),
        in_specs=[pl.BlockSpec((tm, d), lambda i: (i, 0)), mod_spec, mod_spec,
                  _const_spec(wn.shape), _const_spec(wt.shape)],
        out_specs=out_specs,
        out_shape=out_shape,
        compiler_params=_params(1),
        name="in_proj",
    )(x2d, shift, scale, wn, wt)


def _select_topk_rows(s, n_valid_rows):
    nblk = s.shape[0]
    row = lax.broadcasted_iota(jnp.int32, s.shape, 0)
    low = jnp.float32(-3e38)
    s = jnp.where(row < n_valid_rows, s, low)
    sel = jnp.zeros(s.shape, jnp.bool_)
    for _ in range(MOBA_TOPK):
        m = jnp.max(s, axis=0, keepdims=True)
        idx = jnp.min(jnp.where(s == m, row, nblk), axis=0, keepdims=True)
        hit = (row == idx) & (m > low)
        sel = sel | hit
        s = jnp.where(row == idx, low, s)
    return sel


def _attn_body(k2_ref, qT_ref, vT_ref, bd_ref, bp_ref, o_ref, km_ref, q2_ref, *, n_blocks):
    i = pl.program_id(2)
    blk = MOBA_BLOCK
    nsel = q2_ref.shape[0] - 2 * HEAD_DIM - HEAD_DIM

    @pl.when(i == 0)
    def _():
        km_ref[...] = jnp.zeros(km_ref.shape, f32)

        def mean_body(j, c):
            kk = k2_ref[pl.ds(pl.multiple_of(j * blk, blk), blk), :].astype(f32)
            km_ref[pl.ds(j, 1), :] = jnp.sum(kk, axis=0, keepdims=True) * (1.0 / blk)
            return c
        lax.fori_loop(0, n_blocks, mean_body, 0)

    q_pair = qT_ref[:, pl.ds(pl.multiple_of(i * blk, blk), blk)]
    km = km_ref[...]
    km_hi = km.astype(bf16)
    km_lo = (km - km_hi.astype(f32)).astype(bf16)
    row128 = lax.broadcasted_iota(jnp.int32, q_pair.shape, 0)
    rowsel = lax.broadcasted_iota(jnp.int32, (nsel, blk), 0)
    far_pad = jnp.where(i > 0, 0.0, MASKED).astype(f32)
    outs = []
    for hh in range(2):
        in_head = (row128 >= hh * HEAD_DIM) & (row128 < (hh + 1) * HEAD_DIM)
        q2_ref[0:2 * HEAD_DIM, :] = jnp.where(in_head, q_pair, jnp.zeros_like(q_pair))
        q2_ref[2 * HEAD_DIM:, :] = jnp.zeros((q2_ref.shape[0] - 2 * HEAD_DIM, blk), bf16)
        q2 = q2_ref[...]
        s_blk = _dot(km_hi, q2) + _dot(km_lo, q2)
        sel = _select_topk_rows(s_blk, i) | (rowsel == i)
        q2_ref[2 * HEAD_DIM:2 * HEAD_DIM + nsel, :] = jnp.where(sel, 0.0, MASKED).astype(bf16)

        def tile(j):
            kk = k2_ref[pl.ds(pl.multiple_of(j * blk, blk), blk), :]
            return _dot(kk, q2_ref[...])

        def v_tile(j):
            return vT_ref[hh * HEAD_DIM:(hh + 1) * HEAD_DIM, pl.ds(pl.multiple_of(j * blk, blk), blk)]

        def update(s, j, carry):
            m, l, acc = carry
            m2 = jnp.maximum(m, jnp.max(s, axis=0, keepdims=True))
            a = jnp.exp(m - m2)
            p = jnp.exp(s - m2)
            l = a * l + jnp.sum(p, axis=0, keepdims=True)
            acc = a * acc + _dot(v_tile(j), p.astype(bf16))
            return m2, l, acc

        s = tile(i) + bd_ref[hh]
        m = jnp.max(s, axis=0, keepdims=True)
        p = jnp.exp(s - m)
        carry = (m, jnp.sum(p, axis=0, keepdims=True), _dot(v_tile(i), p.astype(bf16)))
        jp = jnp.maximum(i - 1, 0)
        carry = update(tile(jp) + (bp_ref[hh] + far_pad), jp, carry)
        carry = lax.fori_loop(0, i - 1, lambda j, c: update(tile(j), j, c), carry)
        _, l, acc = carry
        outs.append(acc / l)
    o_ref[...] = jnp.concatenate(outs, axis=0).T.astype(o_ref.dtype)


def _attn_prompt(k2, qT, vT, bd, bp, *, bn, seq_len):
    n_pairs, r, _ = k2.shape
    blk = MOBA_BLOCK
    assert seq_len % blk == 0
    nb = seq_len // blk
    nsel = HEAD_DIM
    assert nb <= nsel
    body = functools.partial(_attn_body, n_blocks=nb)
    return pl.pallas_call(
        body,
        grid=(bn, n_pairs, nb),
        in_specs=[pl.BlockSpec((None, seq_len, 2 * LANES), lambda b, p, i: (p, b, 0)),
                  pl.BlockSpec((LANES, seq_len), lambda b, p, i: (p, b)),
                  pl.BlockSpec((LANES, seq_len), lambda b, p, i: (p, b)),
                  pl.BlockSpec((2, blk, blk), lambda b, p, i: (p, 0, 0)),
                  pl.BlockSpec((2, blk, blk), lambda b, p, i: (p, 0, 0))],
        out_specs=pl.BlockSpec((blk, LANES), lambda b, p, i: (b * nb + i, p)),
        out_shape=jax.ShapeDtypeStruct((r, ATTN_DIM), bf16),
        scratch_shapes=[pltpu.VMEM((nsel, 2 * LANES), f32),
                        pltpu.VMEM((2 * LANES, blk), bf16)],
        compiler_params=_params(3),
        name="moba_prompt",
    )(k2, qT, vT, bd, bp)


def _ssd_chunk(conv, dt_raw, dt_raw_t, z, st_ref, dtb, dtb_t, a_row, a_col, expand, dskip, wnorm,
               *, n_valid):
    l = conv.shape[0]
    d_inner = z.shape[1]
    n_state = st_ref.shape[1]
    n_heads = d_inner // SSM_HEAD_DIM
    n_groups = (conv.shape[1] - d_inner) // (2 * n_state)
    hpg = n_heads // n_groups
    gw = hpg * SSM_HEAD_DIM
    last = (l if n_valid is None else n_valid) - 1

    act = _silu(conv)
    xs = act[:, :d_inner]
    bm = act[:, d_inner:d_inner + n_groups * n_state]
    cm = act[:, d_inner + n_groups * n_state:]

    dt = _softplus(dt_raw + dtb)
    dt_t = _softplus(dt_raw_t + dtb_t)
    r_i = lax.broadcasted_iota(jnp.int32, (l, l), 0)
    c_i = lax.broadcasted_iota(jnp.int32, (l, l), 1)
    causal = r_i >= c_i
    tril = causal.astype(bf16)
    triu = (r_i <= c_i).astype(bf16)
    acs = _dot_exact_lhs(tril, dt * a_row)
    acs_t = _dot_exact_rhs(dt_t * a_col, triu)
    eacs = jnp.exp(acs)
    dec = jnp.exp(acs[last:last + 1, :] - acs)
    ea_t = jnp.exp(acs_t)

    dt_full = _dot_exact_rhs(dt, expand)
    eacs_full = _dot_exact_rhs(eacs, expand)
    dec_full = _dot_exact_rhs(dec, expand)
    xd = xs * dt_full
    xdd = xd * dec_full
    if n_valid is not None:
        rows = lax.broadcasted_iota(jnp.int32, xdd.shape, 0)
        xdd = jnp.where(rows < n_valid, xdd, 0.0)

    lane = lax.broadcasted_iota(jnp.int32, (l, 2 * SSM_HEAD_DIM), 1)
    y_parts = []
    for g in range(n_groups):
        bg = bm[:, g * n_state:(g + 1) * n_state].astype(bf16)
        cg = cm[:, g * n_state:(g + 1) * n_state].astype(bf16)
        cb = _dot_nt(cg, bg)
        st_g = st_ref[g * gw:(g + 1) * gw, :]
        y_inter = _dot_nt(cg, st_g.astype(bf16)) * eacs_full[:, g * gw:(g + 1) * gw]
        pair_parts = []
        for q in range(hpg // 2):
            h0 = g * hpg + 2 * q
            xdp = xd[:, h0 * SSM_HEAD_DIM:(h0 + 2) * SSM_HEAD_DIM].astype(bf16)
            res = []
            for h in (h0, h0 + 1):
                seg = acs[:, h:h + 1] - acs_t[h:h + 1, :]
                lm = jnp.exp(jnp.where(causal, seg, MASKED))
                res.append(_dot((cb * lm).astype(bf16), xdp))
            pair_parts.append(jnp.where(lane < SSM_HEAD_DIM, res[0], res[1]))
        y_parts.append(jnp.concatenate(pair_parts, axis=1) + y_inter)
        upd = _dot_tn(xdd[:, g * gw:(g + 1) * gw].astype(bf16), bg)
        for hl in range(hpg):
            h = g * hpg + hl
            rs = slice(g * gw + hl * SSM_HEAD_DIM, g * gw + (hl + 1) * SSM_HEAD_DIM)
            st_ref[rs, :] = (st_g[hl * SSM_HEAD_DIM:(hl + 1) * SSM_HEAD_DIM] * ea_t[h:h + 1, last:last + 1]
                             + upd[hl * SSM_HEAD_DIM:(hl + 1) * SSM_HEAD_DIM])
    y = jnp.concatenate(y_parts, axis=1) + dskip * xs
    y = y * _silu(z.astype(f32))
    normed = []
    for g in range(n_groups):
        yg = y[:, g * gw:(g + 1) * gw]
        normed.append(yg * lax.rsqrt(jnp.mean(yg * yg, axis=-1, keepdims=True) + EPS))
    return jnp.concatenate(normed, axis=1) * wnorm


def _ssd_prompt_body(xbc_ref, dt_ref, dtt_ref, z_ref, wc_ref, bc_ref, dtb_ref, dtbt_ref, a_ref, at_ref,
                     ex_ref, dk_ref, wn_ref, y_ref, st_ref, buf_ref, *, conv_w):
    c = pl.program_id(1)
    l = xbc_ref.shape[0]

    @pl.when(c == 0)
    def _():
        buf_ref[0:SUBLANES, :] = jnp.zeros((SUBLANES, buf_ref.shape[1]), f32)
        st_ref[...] = jnp.zeros(st_ref.shape, f32)

    buf_ref[SUBLANES:SUBLANES + l, :] = xbc_ref[...]
    conv = bc_ref[...]
    for k in range(conv_w):
        off = SUBLANES - (conv_w - 1) + k
        conv = conv + wc_ref[k:k + 1, :] * buf_ref[off:off + l, :]
    buf_ref[0:SUBLANES, :] = buf_ref[l:l + SUBLANES, :]
    y = _ssd_chunk(conv, dt_ref[...], dtt_ref[...], z_ref[...], st_ref, dtb_ref[...], dtbt_ref[...],
                   a_ref[...], at_ref[...], ex_ref[...], dk_ref[...], wn_ref[...], n_valid=None)
    y_ref[...] = y.astype(y_ref.dtype)


def _ssd_prompt(xbc, dt, dtt, z, ssm_w, *, bn, seq_len, n_state):
    r, conv_dim = xbc.shape
    d_inner = z.shape[1]
    l = math.gcd(seq_len, SSM_CHUNK)
    assert l % LANES == 0
    nc = seq_len // l
    wc, bc, dtb, dtbt, a_row, a_col, expand, dskip, wnorm = ssm_w
    conv_w = wc.shape[0]
    assert conv_w - 1 <= SUBLANES
    body = functools.partial(_ssd_prompt_body, conv_w=conv_w)
    consts = [wc, bc, dtb, dtbt, a_row, a_col, expand, dskip, wnorm]
    return pl.pallas_call(
        body,
        grid=(bn, nc),
        in_specs=[pl.BlockSpec((l, conv_dim), lambda b, c: (b * nc + c, 0)),
                  pl.BlockSpec((l, LANES), lambda b, c: (b * nc + c, 0)),
                  pl.BlockSpec((LANES, l), lambda b, c: (0, b * nc + c)),
                  pl.BlockSpec((l, d_inner), lambda b, c: (b * nc + c, 0))]
                 + [_const_spec(w.shape) for w in consts],
        out_specs=[pl.BlockSpec((l, d_inner), lambda b, c: (b * nc + c, 0)),
                   pl.BlockSpec((None, d_inner, n_state), lambda b, c: (b, 0, 0))],
        out_shape=[jax.ShapeDtypeStruct((r, d_inner), bf16),
                   jax.ShapeDtypeStruct((bn, d_inner, n_state), f32)],
        scratch_shapes=[pltpu.VMEM((l + SUBLANES, conv_dim), f32)],
        compiler_params=_params(2),
        name="ssd_prompt",
    )(xbc, dt, dtt, z, *consts)


def _ssd_sample_body(ext_ref, dt_ref, dtt_ref, z_ref, st_in_ref, wc_ref, bc_ref, dtb_ref, dtbt_ref, a_ref,
                     at_ref, ex_ref, dk_ref, wn_ref, y_ref, st_ref, *, conv_w, n_valid):
    lp = dt_ref.shape[0]
    st_ref[...] = st_in_ref[...]
    conv = bc_ref[...]
    for k in range(conv_w):
        conv = conv + wc_ref[k:k + 1, :] * ext_ref[k:k + lp, :]
    y = _ssd_chunk(conv, dt_ref[...], dtt_ref[...], z_ref[...], st_ref, dtb_ref[...], dtbt_ref[...],
                   a_ref[...], at_ref[...], ex_ref[...], dk_ref[...], wn_ref[...], n_valid=n_valid)
    y_ref[...] = y.astype(y_ref.dtype)


def _ssd_sample(ext, dt, dtt, z, state, ssm_w, *, n_valid):
    bs, ext_rows, conv_dim = ext.shape
    lp = dt.shape[1]
    d_inner = z.shape[2]
    n_state = state.shape[2]
    wc, bc, dtb, dtbt, a_row, a_col, expand, dskip, wnorm = ssm_w
    body = functools.partial(_ssd_sample_body, conv_w=wc.shape[0], n_valid=n_valid)
    consts = [wc, bc, dtb, dtbt, a_row, a_col, expand, dskip, wnorm]
    return pl.pallas_call(
        body,
        grid=(bs,),
        in_specs=[pl.BlockSpec((None, ext_rows, conv_dim), lambda s: (s, 0, 0)),
                  pl.BlockSpec((None, lp, LANES), lambda s: (s, 0, 0)),
                  pl.BlockSpec((None, LANES, lp), lambda s: (s, 0, 0)),
                  pl.BlockSpec((None, lp, d_inner), lambda s: (s, 0, 0)),
                  pl.BlockSpec((None, d_inner, n_state), lambda s: (s, 0, 0))]
                 + [_const_spec(w.shape) for w in consts],
        out_specs=[pl.BlockSpec((None, lp, d_inner), lambda s: (s, 0, 0)),
                   pl.BlockSpec((None, d_inner, n_state), lambda s: (s, 0, 0))],
        out_shape=[jax.ShapeDtypeStruct((bs, lp, d_inner), bf16),
                   jax.ShapeDtypeStruct((bs, d_inner, n_state), f32)],
        compiler_params=_params(1),
        name="ssd_sample",
    )(ext, dt, dtt, z, state, *consts)


def _sattn_body(pt_ref, q_ref, kn_ref, vn_ref, bprev_ref, bcur_ref, *rest, nbs, n_full, n_tok, ppb):
    n_pages = nbs * ppb
    kp = rest[:n_pages]
    vp = rest[n_pages:2 * n_pages]
    o_ref = rest[2 * n_pages]
    km_ref, ms_ref, ls_ref, os_ref = rest[2 * n_pages + 1:]
    g = pl.program_id(1)
    n_rows = N_HEADS * SUBLANES
    d = ATTN_DIM

    q4 = q_ref[...]
    q8 = jnp.concatenate([q4, jnp.zeros((SUBLANES - n_tok, d), f32)], axis=0)
    r_i = lax.broadcasted_iota(jnp.int32, (n_rows, d), 0)
    c_i = lax.broadcasted_iota(jnp.int32, (n_rows, d), 1)
    head_mask = (r_i // SUBLANES) == (c_i // HEAD_DIM)
    q_rows = jnp.where(head_mask, jnp.concatenate([q8] * N_HEADS, axis=0), 0.0)
    qb = q_rows.astype(bf16)

    for b in range(nbs):
        kblk = jnp.concatenate([kp[b * ppb + t][...] for t in range(ppb)], axis=0)
        vblk = jnp.concatenate([vp[b * ppb + t][...] for t in range(ppb)], axis=0)
        jj = g * nbs + b
        km_ref[pl.ds(jj, 1), :] = jnp.sum(kblk, axis=0, keepdims=True) * (1.0 / MOBA_BLOCK)
        s = _dot_nt(qb, kblk.astype(bf16))
        s = s + jnp.where(jj == n_full - 1, 1.0, 0.0).astype(f32) * bprev_ref[...]
        m = jnp.max(s, axis=-1, keepdims=True)
        p = jnp.exp(s - m)
        l = jnp.sum(p, axis=-1, keepdims=True)
        ms_ref[jj] = jnp.broadcast_to(m, (n_rows, LANES))
        ls_ref[jj] = jnp.broadcast_to(l, (n_rows, LANES))
        os_ref[jj] = _dot(p.astype(bf16), vblk.astype(bf16))

    @pl.when(g == pl.num_programs(1) - 1)
    def _():
        km = km_ref[...]
        q_hi = qb
        q_lo = (q_rows - q_hi.astype(f32)).astype(bf16)
        km_hi = km.astype(bf16)
        km_lo = (km - km_hi.astype(f32)).astype(bf16)
        sc = _dot_nt(q_hi, km_hi) + (_dot_nt(q_lo, km_hi) + _dot_nt(q_hi, km_lo))
        col = lax.broadcasted_iota(jnp.int32, sc.shape, 1)
        low = jnp.float32(-3e38)
        sel = jnp.zeros(sc.shape, jnp.bool_)
        for _ in range(min(MOBA_TOPK, n_full)):
            mx = jnp.max(sc, axis=-1, keepdims=True)
            idx = jnp.min(jnp.where(sc == mx, col, n_full), axis=-1, keepdims=True)
            sel = sel | (col == idx)
            sc = jnp.where(col == idx, low, sc)
        kn = jnp.concatenate([kn_ref[...], jnp.zeros((SAMPLE_ROWS - n_tok, d), f32)], axis=0).astype(bf16)
        vn = jnp.concatenate([vn_ref[...], jnp.zeros((SAMPLE_ROWS - n_tok, d), f32)], axis=0).astype(bf16)
        s_cur = _dot_nt(qb, kn) + bcur_ref[...]
        m_tot = jnp.max(s_cur, axis=-1, keepdims=True)
        for j in range(n_full):
            m_tot = jnp.maximum(m_tot, jnp.where(sel[:, j:j + 1], ms_ref[j][:, 0:1], low))
        p_cur = jnp.exp(s_cur - m_tot)
        l_tot = jnp.sum(p_cur, axis=-1, keepdims=True)
        acc = _dot(p_cur.astype(bf16), vn)
        for j in range(n_full):
            w = jnp.where(sel[:, j:j + 1], jnp.exp(ms_ref[j][:, 0:1] - m_tot), 0.0)
            l_tot = l_tot + w * ls_ref[j][:, 0:1]
            acc = acc + w * os_ref[j]
        out = jnp.where(head_mask, acc / l_tot, 0.0)
        out8 = out[0:SUBLANES]
        for h in range(1, N_HEADS):
            out8 = out8 + out[h * SUBLANES:(h + 1) * SUBLANES]
        o_ref[...] = out8[0:n_tok]


def _attn_sample(q, kn, vn, ck, cv, page_table, bprev, bcur):
    bs, n_tok, d = q.shape
    n_pool, page, _ = ck.shape
    ppb = MOBA_BLOCK // page
    n_pages_seq = page_table.shape[1]
    past = n_pages_seq * page
    assert MOBA_BLOCK % page == 0 and past % MOBA_BLOCK == 0 and n_tok <= SUBLANES
    n_full = past // MOBA_BLOCK
    nbs = math.gcd(n_full, SAMPLE_BLOCKS_PER_STEP)
    n_steps = n_full // nbs
    n_pages = nbs * ppb
    n_rows = N_HEADS * SUBLANES

    def page_spec(t):
        return pl.BlockSpec((None, page, d), lambda s, g, pt: (pt[s, g * n_pages + t], 0, 0))

    seq_spec = pl.BlockSpec((None, n_tok, d), lambda s, g, pt: (s, 0, 0))
    body = functools.partial(_sattn_body, nbs=nbs, n_full=n_full, n_tok=n_tok, ppb=ppb)
    grid_spec = pltpu.PrefetchScalarGridSpec(
        num_scalar_prefetch=1,
        grid=(bs, n_steps),
        in_specs=[seq_spec, seq_spec, seq_spec,
                  pl.BlockSpec(bprev.shape, lambda s, g, pt: (0, 0)),
                  pl.BlockSpec(bcur.shape, lambda s, g, pt: (0, 0))]
                 + [page_spec(t) for t in range(n_pages)] * 2,
        out_specs=seq_spec,
        scratch_shapes=[pltpu.VMEM((n_full, d), f32),
                        pltpu.VMEM((n_full, n_rows, LANES), f32),
                        pltpu.VMEM((n_full, n_rows, LANES), f32),
                        pltpu.VMEM((n_full, n_rows, d), f32)],
    )
    return pl.pallas_call(
        body,
        grid_spec=grid_spec,
        out_shape=jax.ShapeDtypeStruct((bs, n_tok, d), f32),
        compiler_params=_params(2),
        name="moba_sample",
    )(page_table, q, kn, vn, bprev, bcur, *([ck] * n_pages), *([cv] * n_pages))


def _merge_body(at_ref, yn_ref, ga_ref, gb_ref, x_ref, g1_ref, wpa_ref, wps_ref, wo_ref, lg_ref, lb_ref,
                o_ref, *, alpha):
    pa = _dot(at_ref[...].astype(bf16), wpa_ref[...])
    ps = _dot(yn_ref[...].astype(bf16), wps_ref[...])
    merged = jax.nn.sigmoid(ga_ref[...].astype(f32)) * pa + jax.nn.sigmoid(gb_ref[...].astype(f32)) * ps
    mo = _dot(merged.astype(bf16), wo_ref[...])
    o_ref[...] = _layer_norm(alpha * x_ref[...] + g1_ref[...] * mo, lg_ref[...], lb_ref[...])


def _mod_spec(per_row_mod, tm, d, tiles_per_seq):
    if per_row_mod:
        return pl.BlockSpec((tm, d), lambda i: (i, 0))
    return pl.BlockSpec((None, 1, d), lambda i: (i // tiles_per_seq, 0, 0))


def _merge(attn, yn, ga, gb, x2d, gate1, wpa, wps, wo, lg, lb, *, per_row_mod, seq_len, alpha):
    r, d = x2d.shape
    tm = 256
    assert r % tm == 0
    tiles_per_seq = 1 if per_row_mod else seq_len // tm
    ms = _mod_spec(per_row_mod, tm, d, tiles_per_seq)

    def row_spec(w):
        return pl.BlockSpec((tm, w), lambda i: (i, 0))

    return pl.pallas_call(
        functools.partial(_merge_body, alpha=alpha),
        grid=(r // tm,),
        in_specs=[row_spec(attn.shape[1]), row_spec(yn.shape[1]), row_spec(d), row_spec(d), row_spec(d), ms,
                  _const_spec(wpa.shape), _const_spec(wps.shape), _const_spec(wo.shape),
                  _const_spec(lg.shape), _const_spec(lb.shape)],
        out_specs=row_spec(d),
        out_shape=jax.ShapeDtypeStruct((r, d), f32),
        compiler_params=_params(1),
        name="merge_ln1",
    )(attn, yn, ga, gb, x2d, gate1, wpa, wps, wo, lg, lb)


FFN_COL_CHUNKS = 2


def _ffn_body(x_ref, sh_ref, sc_ref, g2_ref, wu_ref, wc_ref, bc_ref, wd_ref, lg_ref, lb_ref, *rest,
              alpha, conv_w, tiles_per_seq, sample_len):
    if sample_len is None:
        y_ref, tail_ref, buf_ref, carry_ref = rest
    else:
        p_refs = rest[:conv_w - 1]
        y_ref, hup_ref, buf_ref, carry_ref = rest[conv_w - 1:]
    x = x_ref[...]
    tm = x.shape[0]
    ff = wd_ref.shape[0]
    cw = ff // FFN_COL_CHUNKS
    t = pl.program_id(0) % tiles_per_seq

    @pl.when(t == 0)
    def _():
        carry_ref[...] = jnp.zeros(carry_ref.shape, f32)

    u = (x * (1.0 + sc_ref[...]) + sh_ref[...]).astype(bf16)
    if sample_len is not None:
        tmod = lax.broadcasted_iota(jnp.int32, (tm, cw), 0) % sample_len
    f = jnp.zeros((tm, x.shape[1]), f32)
    for c in range(FFN_COL_CHUNKS):
        halves = []
        for half in range(2):
            c0 = half * ff + c * cw
            buf_ref[0:SUBLANES, :] = carry_ref[:, c0:c0 + cw]
            hup = _dot(u, wu_ref[:, c0:c0 + cw])
            buf_ref[SUBLANES:SUBLANES + tm, :] = hup
            if sample_len is not None:
                hup_ref[:, c0:c0 + cw] = hup
            hc = bc_ref[:, c0:c0 + cw] + wc_ref[conv_w - 1:conv_w, c0:c0 + cw] * hup
            for k in range(conv_w - 1):
                back = conv_w - 1 - k
                prev = buf_ref[SUBLANES - back:SUBLANES - back + tm, :]
                if sample_len is not None:
                    prev = jnp.where(tmod >= back, prev, p_refs[back - 1][:, c0:c0 + cw])
                hc = hc + wc_ref[k:k + 1, c0:c0 + cw] * prev
            carry_ref[:, c0:c0 + cw] = buf_ref[tm:tm + SUBLANES, :]
            halves.append(hc)
        gact = (_silu(halves[0]) * halves[1]).astype(bf16)
        f = f + _dot(gact, wd_ref[c * cw:(c + 1) * cw, :])
    y_ref[...] = _layer_norm(alpha * x + g2_ref[...] * f, lg_ref[...], lb_ref[...])
    if sample_len is None:
        @pl.when(t == tiles_per_seq - 1)
        def _():
            tail_ref[...] = carry_ref[...]


def _ffn(x2d, shift, scale, gate, wu, wc, bc, wd, lg, lb, prevs, *, per_row_mod, seq_len, bn, alpha,
         sample_len):
    r, d = x2d.shape
    ff2 = wu.shape[1]
    ff = wd.shape[0]
    conv_w = wc.shape[0]
    tm = 256 if sample_len is None else 128
    assert r % tm == 0 and ff % (FFN_COL_CHUNKS * LANES) == 0 and conv_w - 1 <= SUBLANES
    tiles_per_seq = 1 if per_row_mod else seq_len // tm
    ms = _mod_spec(per_row_mod, tm, d, tiles_per_seq)

    def row_spec(w):
        return pl.BlockSpec((tm, w), lambda i: (i, 0))

    in_specs = [row_spec(d), ms, ms, ms, _const_spec(wu.shape), _const_spec(wc.shape), _const_spec(bc.shape),
                _const_spec(wd.shape), _const_spec(lg.shape), _const_spec(lb.shape)]
    args = [x2d, shift, scale, gate, wu, wc, bc, wd, lg, lb]
    if sample_len is None:
        out_shape = [jax.ShapeDtypeStruct((r, d), f32), jax.ShapeDtypeStruct((bn, SUBLANES, ff2), f32)]
        out_specs = [row_spec(d), pl.BlockSpec((None, SUBLANES, ff2), lambda i: (i // tiles_per_seq, 0, 0))]
    else:
        assert tm % sample_len == 0 and len(prevs) == conv_w - 1
        in_specs += [row_spec(ff2)] * len(prevs)
        args += list(prevs)
        out_shape = [jax.ShapeDtypeStruct((r, d), f32), jax.ShapeDtypeStruct((r, ff2), f32)]
        out_specs = [row_spec(d), row_spec(ff2)]
    body = functools.partial(_ffn_body, alpha=alpha, conv_w=conv_w, tiles_per_seq=tiles_per_seq,
                             sample_len=sample_len)
    return pl.pallas_call(
        body,
        grid=(r // tm,),
        in_specs=in_specs,
        out_specs=out_specs,
        out_shape=out_shape,
        scratch_shapes=[pltpu.VMEM((tm + SUBLANES, ff // FFN_COL_CHUNKS), f32),
                        pltpu.VMEM((SUBLANES, ff2), f32)],
        compiler_params=_params(1),
        name="conv_ffn",
    )(*args)


def _rel_bucket(dist):
    n = jnp.maximum(dist, 0)
    max_exact = REL_BUCKETS // 2
    nf = jnp.maximum(n, 1).astype(f32)
    large = max_exact + (jnp.log(nf / max_exact) / math.log(REL_MAX_DIST / max_exact)
                         * (REL_BUCKETS - max_exact)).astype(jnp.int32)
    large = jnp.minimum(large, REL_BUCKETS - 1)
    return jnp.where(n < max_exact, n, large)


def _rel_bias(rel_table, dist):
    far = rel_table[REL_BUCKETS - 1]
    b = rel_table[_rel_bucket(dist)]
    return jnp.moveaxis(b - far, -1, 0)


def _prompt_bias_tables(rel_table):
    a = jnp.arange(MOBA_BLOCK, dtype=jnp.int32)
    dist_d = a[None, :] - a[:, None]
    bd = jnp.where(dist_d >= 0, _rel_bias(rel_table, dist_d), MASKED)
    bp = _rel_bias(rel_table, dist_d + MOBA_BLOCK)
    return bd.astype(f32), bp.astype(f32)


def _sample_bias_tables(rel_table, n_tok):
    n_rows = N_HEADS * SUBLANES
    t = jnp.arange(SUBLANES, dtype=jnp.int32)
    a = jnp.arange(MOBA_BLOCK, dtype=jnp.int32)
    bprev = _rel_bias(rel_table, MOBA_BLOCK + t[:, None] - a[None, :])
    bprev = jnp.where((t < n_tok)[None, :, None], bprev, 0.0).reshape(n_rows, MOBA_BLOCK)
    tk = jnp.arange(SAMPLE_ROWS, dtype=jnp.int32)
    dist = t[:, None] - tk[None, :]
    ok = (dist >= 0) & (t[:, None] < n_tok) & (tk[None, :] < n_tok)
    bcur = jnp.where(ok[None], _rel_bias(rel_table, dist), MASKED).reshape(n_rows, SAMPLE_ROWS)
    return bprev.astype(f32), bcur.astype(f32)


def _pad_cols(w, n):
    return jnp.pad(w, ((0, 0), (0, n - w.shape[1])))


def kernel(x_prompt, x_sample, cache_k, cache_v, page_table, state_ssm, state_conv_ssm, state_conv_ffn,
           c_prompt, c_sample, rel_table, w_ada, b_ada, w_in, w_conv_ssm, b_conv_ssm, dt_bias, a_log,
           d_skip, w_norm_ssm, w_proj_attn, w_proj_ssm, w_out, ln1_g, ln1_b, w_up, w_conv_ffn, b_conv_ffn,
           w_down, ln2_g, ln2_b):
    depth = w_ada.shape[0]
    alpha = (2 * depth) ** 0.25
    bp_, seq, d = x_prompt.shape
    bs, n_tok, _ = x_sample.shape
    ssm_heads = dt_bias.shape[1]
    d_inner = ssm_heads * SSM_HEAD_DIM
    n_state = state_ssm.shape[-1]
    conv_dim = w_conv_ssm.shape[-1]
    ff2 = w_up.shape[-1]
    ssm_conv = w_conv_ssm.shape[1]
    ffn_conv = w_conv_ffn.shape[1]
    assert ssm_heads <= LANES and n_tok <= SUBLANES
    page = cache_k.shape[2]
    scale = HEAD_DIM ** -0.5

    bd, bpv = _prompt_bias_tables(rel_table)
    bprev_s, bcur_s = _sample_bias_tables(rel_table, n_tok)
    expand = (jnp.arange(LANES)[:, None] == (jnp.arange(d_inner)[None, :] // SSM_HEAD_DIM)).astype(bf16)

    yp = x_prompt.reshape(bp_ * seq, d)
    ys = x_sample.reshape(bs * n_tok, d)
    outs_p = [[] for _ in range(5)]
    outs_s = [[] for _ in range(5)]
    for l in range(depth):
        cuts = [ATTN_DIM, 2 * ATTN_DIM, 3 * ATTN_DIM, 3 * ATTN_DIM + d_inner,
                3 * ATTN_DIM + d_inner + conv_dim, 3 * ATTN_DIM + d_inner + conv_dim + ssm_heads,
                3 * ATTN_DIM + d_inner + conv_dim + ssm_heads + d]
        wq, wk, wv, wz, wxbc, wdt, wga, wgb = jnp.split(w_in[l], cuts, axis=1)
        wdt = _pad_cols(wdt, LANES)
        wq = wq * scale
        nat_p = [(ATTN_DIM, f32), (ATTN_DIM, f32), (d_inner, bf16), (conv_dim, f32), (LANES, f32),
                 (d, bf16), (d, bf16)]
        wn_p = jnp.concatenate([wk, wv, wz, wxbc, wdt, wga, wgb], axis=1).astype(bf16)
        tr_p = [(ATTN_DIM, bf16), (ATTN_DIM, bf16), (LANES, f32)]
        wt_p = jnp.concatenate([wq, wv, wdt], axis=1).T.astype(bf16)
        nat_s = [(ATTN_DIM, f32)] + nat_p
        wn_s = jnp.concatenate([wq.astype(bf16), wn_p], axis=1)
        tr_s = [(LANES, f32)]
        wt_s = wdt.T.astype(bf16)
        ssm_w = (w_conv_ssm[l], b_conv_ssm[l].reshape(1, conv_dim),
                 _pad_cols(dt_bias[l].reshape(1, -1), LANES), _pad_cols(dt_bias[l].reshape(1, -1), LANES).T,
                 -jnp.exp(_pad_cols(a_log[l].reshape(1, -1), LANES)),
                 -jnp.exp(_pad_cols(a_log[l].reshape(1, -1), LANES)).T,
                 expand, jnp.repeat(d_skip[l], SSM_HEAD_DIM).reshape(1, d_inner),
                 w_norm_ssm[l].reshape(1, d_inner))
        wpa, wps, wo = (w_proj_attn[l].astype(bf16), w_proj_ssm[l].astype(bf16), w_out[l].astype(bf16))
        lg1, lb1 = ln1_g[l].reshape(1, d), ln1_b[l].reshape(1, d)
        lg2, lb2 = ln2_g[l].reshape(1, d), ln2_b[l].reshape(1, d)
        wu, wd = w_up[l].astype(bf16), w_down[l].astype(bf16)
        wcf, bcf = w_conv_ffn[l], b_conv_ffn[l].reshape(1, ff2)

        mod = _ada(jnp.concatenate([c_prompt, c_sample], axis=0), w_ada[l], b_ada[l])
        mod_p = mod[:bp_].reshape(bp_, 6, 1, d)
        mod_s = jnp.repeat(mod[bp_:].reshape(bs, 6, 1, d), n_tok, axis=2).reshape(bs, 6, n_tok, d)
        mod_s = jnp.moveaxis(mod_s, 1, 0).reshape(6, bs * n_tok, d)
        sh1p, sc1p, g1p, sh2p, sc2p, g2p = (mod_p[:, i] for i in range(6))
        sh1s, sc1s, g1s, sh2s, sc2s, g2s = (mod_s[i] for i in range(6))

        k_p, v_p, z_p, xbc_p, dt_p, ga_p, gb_p, qT_p, vT_p, dtT_p, k2_p = _inproj(
            yp, sh1p, sc1p, wn_p, wt_p, nat_p, tr_p, per_row_mod=False, seq_len=seq, attn_layout=True)
        attn_p = _attn_prompt(k2_p, qT_p, vT_p, bd, bpv, bn=bp_, seq_len=seq)
        yn_p, st_p = _ssd_prompt(xbc_p, dt_p, dtT_p, z_p, ssm_w, bn=bp_, seq_len=seq, n_state=n_state)
        x1_p = _merge(attn_p, yn_p, ga_p, gb_p, yp, g1p, wpa, wps, wo, lg1, lb1,
                      per_row_mod=False, seq_len=seq, alpha=alpha)
        yp, tail_p = _ffn(x1_p, sh2p, sc2p, g2p, wu, wcf, bcf, wd, lg2, lb2, (),
                          per_row_mod=False, seq_len=seq, bn=bp_, alpha=alpha, sample_len=None)
        outs_p[0].append(k_p.reshape(bp_, seq, N_HEADS, HEAD_DIM))
        outs_p[1].append(v_p.reshape(bp_, seq, N_HEADS, HEAD_DIM))
        outs_p[2].append(st_p.reshape(bp_, ssm_heads, SSM_HEAD_DIM, n_state))
        outs_p[3].append(xbc_p.reshape(bp_, seq, conv_dim)[:, seq - (ssm_conv - 1):])
        outs_p[4].append(tail_p[:, SUBLANES - (ffn_conv - 1):])

        r_s = bs * n_tok
        r_pad = -(-r_s // 256) * 256

        def pad_rows(a):
            return jnp.pad(a, ((0, r_pad - r_s), (0, 0)))

        q_s, k_s, v_s, z_s, xbc_s, dt_s, ga_s, gb_s, dtT_s = _inproj(
            pad_rows(ys), pad_rows(sh1s), pad_rows(sc1s), wn_s, wt_s, nat_s, tr_s,
            per_row_mod=True, seq_len=None, attn_layout=False)
        q_s, k_s, v_s = (a[:r_s].reshape(bs, n_tok, ATTN_DIM) for a in (q_s, k_s, v_s))
        attn_s = _attn_sample(q_s, k_s, v_s, cache_k[l].reshape(-1, page, ATTN_DIM),
                              cache_v[l].reshape(-1, page, ATTN_DIM), page_table, bprev_s, bcur_s)
        row_pad = SAMPLE_ROWS - n_tok
        xbc_s3 = xbc_s[:r_s].reshape(bs, n_tok, conv_dim)
        ext = jnp.concatenate([state_conv_ssm[l], xbc_s3,
                               jnp.zeros((bs, row_pad + SUBLANES - (ssm_conv - 1), conv_dim), f32)], axis=1)
        dt_s3 = jnp.pad(dt_s[:r_s].reshape(bs, n_tok, LANES), ((0, 0), (0, row_pad), (0, 0)))
        dtT_s3 = jnp.pad(jnp.moveaxis(dtT_s[:, :r_s].reshape(LANES, bs, n_tok), 0, 1),
                         ((0, 0), (0, 0), (0, row_pad)))
        z_s3 = jnp.pad(z_s[:r_s].reshape(bs, n_tok, d_inner), ((0, 0), (0, row_pad), (0, 0)))
        yn_s3, st_s = _ssd_sample(ext, dt_s3, dtT_s3, z_s3, state_ssm[l].reshape(bs, d_inner, n_state),
                                  ssm_w, n_valid=n_tok)
        yn_s = yn_s3[:, :n_tok].reshape(r_s, d_inner)
        x1_s = _merge(pad_rows(attn_s.reshape(r_s, ATTN_DIM)), pad_rows(yn_s), ga_s, gb_s, pad_rows(ys),
                      pad_rows(g1s), wpa, wps, wo, lg1, lb1, per_row_mod=True, seq_len=None, alpha=alpha)
        cf = state_conv_ffn[l]
        prevs = []
        for back in range(1, ffn_conv):
            rows = [cf[:, ffn_conv - 1 - back + t] if t < back else jnp.zeros((bs, ff2), f32)
                    for t in range(n_tok)]
            prevs.append(pad_rows(jnp.stack(rows, axis=1).reshape(r_s, ff2)))
        y_s, hup_s = _ffn(x1_s, pad_rows(sh2s), pad_rows(sc2s), pad_rows(g2s), wu, wcf, bcf, wd, lg2, lb2,
                          prevs, per_row_mod=True, seq_len=None, bn=bs, alpha=alpha, sample_len=n_tok)
        ys = y_s[:r_s]
        outs_s[0].append(k_s.reshape(bs, n_tok, N_HEADS, HEAD_DIM))
        outs_s[1].append(v_s.reshape(bs, n_tok, N_HEADS, HEAD_DIM))
        outs_s[2].append(st_s.reshape(bs, ssm_heads, SSM_HEAD_DIM, n_state))
        cs_ext = jnp.concatenate([state_conv_ssm[l], xbc_s3], axis=1)
        outs_s[3].append(cs_ext[:, -(ssm_conv - 1):])
        cf_ext = jnp.concatenate([cf, hup_s[:r_s].reshape(bs, n_tok, ff2)], axis=1)
        outs_s[4].append(cf_ext[:, -(ffn_conv - 1):])

    return (yp.reshape(bp_, seq, d), ys.reshape(bs, n_tok, d),
            *(jnp.stack(o) for o in outs_p), *(jnp.stack(o) for o in outs_s))
```

```python
import functools
import math

import jax
import jax.numpy as jnp
from jax import lax
from jax.experimental import pallas as pl
from jax.experimental.pallas import tpu as pltpu

N_HEADS = 8
HEAD_DIM = 64
ATTN_DIM = N_HEADS * HEAD_DIM
MOBA_BLOCK = 256
MOBA_TOPK = 3
REL_BUCKETS = 32
REL_MAX_DIST = 128
SSM_HEAD_DIM = 64
SSM_CHUNK = 256
EPS = 1e-5

LANES = 128
SUBLANES = 8
BF16_ROWS = 16
VMEM_LIMIT = 56 * 1024 * 1024

MASKED = -1e30
SAMPLE_ROWS = 16
SAMPLE_BLOCKS_PER_STEP = 4
FAR_GROUP = 4
LOG2E = math.log2(math.e)

f32 = jnp.float32
bf16 = jnp.bfloat16

_NT = (((1,), (1,)), ((), ()))
_TN = (((0,), (0,)), ((), ()))


def _dot(a, b):
    return jnp.dot(a, b, preferred_element_type=f32)


def _dot_nt(a, b):
    return lax.dot_general(a, b, _NT, preferred_element_type=f32)


def _dot_tn(a, b):
    return lax.dot_general(a, b, _TN, preferred_element_type=f32)


def _split3(x):
    hi = x.astype(bf16)
    r = x - hi.astype(f32)
    mid = r.astype(bf16)
    lo = (r - mid.astype(f32)).astype(bf16)
    return hi, mid, lo


def _dot_exact_rhs(a, m):
    hi, mid, lo = _split3(a)
    return _dot(hi, m) + _dot(mid, m) + _dot(lo, m)


def _dot_exact_lhs(m, a):
    hi, mid, lo = _split3(a)
    return _dot(m, hi) + _dot(m, mid) + _dot(m, lo)


def _silu(x):
    return x * jax.nn.sigmoid(x)


def _softplus(x):
    return jnp.maximum(x, 0.0) + jnp.log1p(jnp.exp(-jnp.abs(x)))


def _layer_norm(x, g, b):
    mu = jnp.mean(x, axis=-1, keepdims=True)
    xc = x - mu
    var = jnp.mean(xc * xc, axis=-1, keepdims=True)
    return xc * lax.rsqrt(var + EPS) * g + b


def _const_spec(shape):
    nd = len(shape)
    return pl.BlockSpec(shape, lambda *_: (0,) * nd, pipeline_mode=pl.Buffered(1))


def _params(n_grid):
    return pltpu.CompilerParams(
        dimension_semantics=("arbitrary",) * n_grid, vmem_limit_bytes=VMEM_LIMIT)


def _ada_body(c_ref, w_ref, b_ref, o_ref):
    a = _silu(c_ref[...])
    hi, mid, lo = _split3(a)
    whi, wmid, wlo = _split3(w_ref[...])
    acc = _dot(hi, whi) + (_dot(hi, wmid) + _dot(mid, whi))
    acc = acc + (_dot(hi, wlo) + _dot(mid, wmid) + _dot(lo, whi))
    o_ref[...] = acc + b_ref[...]


def _ada(c, w, b):
    n, d = c.shape
    dn = w.shape[1]
    tn = 1024 if dn % 1024 == 0 else dn
    return pl.pallas_call(
        _ada_body,
        grid=(dn // tn,),
        in_specs=[pl.BlockSpec((n, d), lambda j: (0, 0)),
                  pl.BlockSpec((d, tn), lambda j: (0, j)),
                  pl.BlockSpec((1, tn), lambda j: (0, j))],
        out_specs=pl.BlockSpec((n, tn), lambda j: (0, j)),
        out_shape=jax.ShapeDtypeStruct((n, dn), f32),
        compiler_params=_params(1),
        name="ada_mod",
    )(c, w, b.reshape(1, dn))


def _inproj_body(x_ref, sh_ref, sc_ref, wn_ref, wt_ref, *outs, nat, tr, k_col, tiles_per_seq):
    u = (x_ref[...] * (1.0 + sc_ref[...]) + sh_ref[...]).astype(bf16)
    tm = u.shape[0]
    o = 0
    col = 0
    for width, _ in nat:
        res = _dot(u, wn_ref[:, col:col + width])
        outs[o][...] = res.astype(outs[o].dtype)
        if k_col is not None and col == k_col:
            kf = res
        o += 1
        col += width
    if tr:
        t = _dot_nt(wt_ref[...], u)
        row = 0
        for height, _ in tr:
            outs[o][...] = t[row:row + height].astype(outs[o].dtype)
            o += 1
            row += height
    if k_col is not None:
        pos = (pl.program_id(0) % tiles_per_seq) * tm
        blk = pos // MOBA_BLOCK
        onehot = (lax.broadcasted_iota(jnp.int32, (tm, LANES), 1) == blk).astype(bf16)
        k2_ref = outs[o]
        for p in range(ATTN_DIM // LANES):
            k2_ref[p] = jnp.concatenate([kf[:, p * LANES:(p + 1) * LANES].astype(bf16), onehot], axis=1)


def _inproj(x2d, shift, scale, wn, wt, nat, tr, *, per_row_mod, seq_len, attn_layout):
    r, d = x2d.shape
    tm = 256
    assert r % tm == 0
    n_tiles = r // tm
    if per_row_mod:
        mod_spec = pl.BlockSpec((tm, d), lambda i: (i, 0))
        tiles_per_seq = 1
    else:
        assert seq_len % tm == 0 and MOBA_BLOCK % tm == 0
        tiles_per_seq = seq_len // tm
        mod_spec = pl.BlockSpec((None, 1, d), lambda i: (i // tiles_per_seq, 0, 0))
    out_shape, out_specs = [], []
    for width, dt in nat:
        out_shape.append(jax.ShapeDtypeStruct((r, width), dt))
        out_specs.append(pl.BlockSpec((tm, width), lambda i: (i, 0)))
    for height, dt in tr:
        out_shape.append(jax.ShapeDtypeStruct((height, r), dt))
        out_specs.append(pl.BlockSpec((height, tm), lambda i: (0, i)))
    k_col = None
    if attn_layout:
        k_col = 0
        n_pairs = ATTN_DIM // LANES
        out_shape.append(jax.ShapeDtypeStruct((n_pairs, r, 2 * LANES), bf16))
        out_specs.append(pl.BlockSpec((n_pairs, tm, 2 * LANES), lambda i: (0, i, 0)))
    body = functools.partial(_inproj_body, nat=tuple(nat), tr=tuple(tr), k_col=k_col,
                             tiles_per_seq=tiles_per_seq)
    return pl.pallas_call(
        body,
        grid=(n_tiles,),
        in_specs=[pl.BlockSpec((tm, d), lambda i: (i, 0)), mod_spec, mod_spec,
                  _const_spec(wn.shape), _const_spec(wt.shape)],
        out_specs=out_specs,
        out_shape=out_shape,
        compiler_params=_params(1),
        name="in_proj",
    )(x2d, shift, scale, wn, wt)


def _select_topk_rows(s, n_valid_rows):
    nblk = s.shape[0]
    row = lax.broadcasted_iota(jnp.int32, s.shape, 0)
    low = jnp.float32(-3e38)
    s = jnp.where(row < n_valid_rows, s, low)
    sel = jnp.zeros(s.shape, jnp.bool_)
    for _ in range(MOBA_TOPK):
        m = jnp.max(s, axis=0, keepdims=True)
        idx = jnp.min(jnp.where(s == m, row, nblk), axis=0, keepdims=True)
        hit = (row == idx) & (m > low)
        sel = sel | hit
        s = jnp.where(row == idx, low, s)
    return sel


def _attn_body(k2_ref, qT_ref, vT_ref, bd_ref, bp_ref, o_ref, km_ref, q2n_ref, q2f_ref, sa_ref, sb_ref, *,
               n_blocks, group):
    i = pl.program_id(2)
    blk = MOBA_BLOCK
    hd = HEAD_DIM
    nsel = km_ref.shape[0]

    @pl.when(i == 0)
    def _():
        km_ref[...] = jnp.zeros(km_ref.shape, f32)

        def mean_body(j, c):
            kk = k2_ref[pl.ds(pl.multiple_of(j * blk, blk), blk), :].astype(f32)
            km_ref[pl.ds(j, 1), :] = jnp.sum(kk, axis=0, keepdims=True) * (1.0 / blk)
            return c
        lax.fori_loop(0, n_blocks, mean_body, 0)

    q_pair = qT_ref[:, pl.ds(pl.multiple_of(i * blk, blk), blk)]
    zq = jnp.zeros((hd, blk), bf16)
    q_rows = jnp.concatenate([jnp.concatenate([q_pair[0:hd], zq], axis=1),
                              jnp.concatenate([zq, q_pair[hd:2 * hd]], axis=1)], axis=0)
    zrest = jnp.zeros((q2f_ref.shape[0] - 2 * hd, 2 * blk), bf16)
    q2f_ref[0:2 * hd, :] = q_rows
    q2f_ref[2 * hd:, :] = zrest
    q2n_ref[0:2 * hd, :] = q_rows
    q2n_ref[2 * hd:, :] = zrest
    km = km_ref[...]
    km_hi = km.astype(bf16)
    km_lo = (km - km_hi.astype(f32)).astype(bf16)
    q2 = q2f_ref[...]
    s_blk = _dot(km_hi, q2) + _dot(km_lo, q2)
    sel = _select_topk_rows(s_blk, i)
    rowsel = lax.broadcasted_iota(jnp.int32, (nsel, 2 * blk), 0)
    far = sel & (rowsel < i - 1)
    near = (sel & (rowsel == i - 1)) | (rowsel == i)
    q2f_ref[2 * hd:2 * hd + nsel, :] = jnp.where(far, 0.0, MASKED).astype(bf16)
    q2n_ref[2 * hd:2 * hd + nsel, :] = jnp.where(near, 0.0, MASKED).astype(bf16)

    def k_rows(j, n):
        return k2_ref[pl.ds(pl.multiple_of(j * blk, blk), n * blk), :]

    def v_cols(h, j, n):
        return vT_ref[h * hd:(h + 1) * hd, pl.ds(pl.multiple_of(j * blk, blk), n * blk)]

    jp = jnp.maximum(i - 1, 0)
    first_pad = jnp.where(i > 0, 0.0, MASKED).astype(f32)
    s_cur = _dot(k_rows(i, 1), q2n_ref[...])
    s_prev = _dot(k_rows(jp, 1), q2n_ref[...])
    carry = []
    for h in range(2):
        s = jnp.concatenate([s_cur[:, h * blk:(h + 1) * blk] + bd_ref[h],
                             s_prev[:, h * blk:(h + 1) * blk] + (bp_ref[h] + first_pad)], axis=0)
        m = jnp.max(s, axis=0, keepdims=True)
        p = jnp.exp2(s - m)
        v = jnp.concatenate([v_cols(h, i, 1), v_cols(h, jp, 1)], axis=1)
        carry += [m, jnp.sum(p, axis=0, keepdims=True), _dot(v, p.astype(bf16))]

    last_group = n_blocks // group - 1

    def logits_into(s_ref, g):
        s_ref[...] = _dot(k_rows(jnp.minimum(g, last_group) * group, group), q2f_ref[...])

    def consume(s_ref, g, carry):
        j0 = jnp.minimum(g, last_group) * group
        pad = jnp.where(g < n_groups, 0.0, MASKED).astype(f32)
        out = []
        for h in range(2):
            m, l, acc = carry[3 * h:3 * h + 3]
            m2 = jnp.maximum(m, jnp.max(s_ref[:, h * blk:(h + 1) * blk], axis=0, keepdims=True) + pad)
            a = jnp.exp2(m - m2)
            p = jnp.exp2(s_ref[:, h * blk:(h + 1) * blk] - (m2 - pad))
            l = a * l + jnp.sum(p, axis=0, keepdims=True)
            acc = a * acc + _dot(v_cols(h, j0, group), p.astype(bf16))
            out += [m2, l, acc]
        return out

    def far_body(t, carry):
        logits_into(sb_ref, 2 * t + 1)
        carry = consume(sa_ref, 2 * t, list(carry))
        logits_into(sa_ref, 2 * t + 2)
        return tuple(consume(sb_ref, 2 * t + 1, carry))

    n_groups = (jnp.maximum(i - 1, 0) + group - 1) // group
    logits_into(sa_ref, 0)
    carry = lax.fori_loop(0, (n_groups + 1) // 2, far_body, tuple(carry))
    outs = [carry[3 * h + 2] / carry[3 * h + 1] for h in range(2)]
    o_ref[...] = jnp.concatenate(outs, axis=0).T.astype(o_ref.dtype)


def _attn_prompt(k2, qT, vT, bd, bp, *, bn, seq_len):
    n_pairs, r, _ = k2.shape
    blk = MOBA_BLOCK
    assert seq_len % blk == 0
    nb = seq_len // blk
    nsel = HEAD_DIM
    assert nb <= nsel
    group = math.gcd(nb, FAR_GROUP)
    body = functools.partial(_attn_body, n_blocks=nb, group=group)
    return pl.pallas_call(
        body,
        grid=(bn, n_pairs, nb),
        in_specs=[pl.BlockSpec((None, seq_len, 2 * LANES), lambda b, p, i: (p, b, 0)),
                  pl.BlockSpec((LANES, seq_len), lambda b, p, i: (p, b)),
                  pl.BlockSpec((LANES, seq_len), lambda b, p, i: (p, b)),
                  pl.BlockSpec((2, blk, blk), lambda b, p, i: (p, 0, 0)),
                  pl.BlockSpec((2, blk, blk), lambda b, p, i: (p, 0, 0))],
        out_specs=pl.BlockSpec((blk, LANES), lambda b, p, i: (b * nb + i, p)),
        out_shape=jax.ShapeDtypeStruct((r, ATTN_DIM), bf16),
        scratch_shapes=[pltpu.VMEM((nsel, 2 * LANES), f32),
                        pltpu.VMEM((2 * LANES, 2 * blk), bf16),
                        pltpu.VMEM((2 * LANES, 2 * blk), bf16),
                        pltpu.VMEM((group * blk, 2 * blk), f32),
                        pltpu.VMEM((group * blk, 2 * blk), f32)],
        compiler_params=_params(3),
        name="moba_prompt",
    )(k2, qT, vT, bd, bp)


def _ssd_chunk(conv, dt_raw, dt_raw_t, z, st_ref, dtb, dtb_t, a_row, a_col, expand, dskip, wnorm,
               *, n_valid):
    l = conv.shape[0]
    d_inner = z.shape[1]
    n_state = st_ref.shape[1]
    n_heads = d_inner // SSM_HEAD_DIM
    n_groups = (conv.shape[1] - d_inner) // (2 * n_state)
    hpg = n_heads // n_groups
    gw = hpg * SSM_HEAD_DIM
    last = (l if n_valid is None else n_valid) - 1

    act = _silu(conv)
    xs = act[:, :d_inner]
    bm = act[:, d_inner:d_inner + n_groups * n_state]
    cm = act[:, d_inner + n_groups * n_state:]

    dt = _softplus(dt_raw + dtb)
    dt_t = _softplus(dt_raw_t + dtb_t)
    r_i = lax.broadcasted_iota(jnp.int32, (l, l), 0)
    c_i = lax.broadcasted_iota(jnp.int32, (l, l), 1)
    causal = r_i >= c_i
    tril = causal.astype(bf16)
    triu = (r_i <= c_i).astype(bf16)
    acs = _dot_exact_lhs(tril, dt * a_row)
    acs_t = _dot_exact_rhs(dt_t * a_col, triu)
    eacs = jnp.exp(acs)
    dec = jnp.exp(acs[last:last + 1, :] - acs)
    ea_t = jnp.exp(acs_t)

    dt_full = _dot_exact_rhs(dt, expand)
    eacs_full = _dot_exact_rhs(eacs, expand)
    dec_full = _dot_exact_rhs(dec, expand)
    xd = xs * dt_full
    xdd = xd * dec_full
    if n_valid is not None:
        rows = lax.broadcasted_iota(jnp.int32, xdd.shape, 0)
        xdd = jnp.where(rows < n_valid, xdd, 0.0)

    lane = lax.broadcasted_iota(jnp.int32, (l, 2 * SSM_HEAD_DIM), 1)
    y_parts = []
    for g in range(n_groups):
        bg = bm[:, g * n_state:(g + 1) * n_state].astype(bf16)
        cg = cm[:, g * n_state:(g + 1) * n_state].astype(bf16)
        cb = _dot_nt(cg, bg)
        st_g = st_ref[g * gw:(g + 1) * gw, :]
        y_inter = _dot_nt(cg, st_g.astype(bf16)) * eacs_full[:, g * gw:(g + 1) * gw]
        pair_parts = []
        for q in range(hpg // 2):
            h0 = g * hpg + 2 * q
            xdp = xd[:, h0 * SSM_HEAD_DIM:(h0 + 2) * SSM_HEAD_DIM].astype(bf16)
            res = []
            for h in (h0, h0 + 1):
                seg = acs[:, h:h + 1] - acs_t[h:h + 1, :]
                lm = jnp.exp(jnp.where(causal, seg, MASKED))
                res.append(_dot((cb * lm).astype(bf16), xdp))
            pair_parts.append(jnp.where(lane < SSM_HEAD_DIM, res[0], res[1]))
        y_parts.append(jnp.concatenate(pair_parts, axis=1) + y_inter)
        upd = _dot_tn(xdd[:, g * gw:(g + 1) * gw].astype(bf16), bg)
        for hl in range(hpg):
            h = g * hpg + hl
            rs = slice(g * gw + hl * SSM_HEAD_DIM, g * gw + (hl + 1) * SSM_HEAD_DIM)
            st_ref[rs, :] = (st_g[hl * SSM_HEAD_DIM:(hl + 1) * SSM_HEAD_DIM] * ea_t[h:h + 1, last:last + 1]
                             + upd[hl * SSM_HEAD_DIM:(hl + 1) * SSM_HEAD_DIM])
    y = jnp.concatenate(y_parts, axis=1) + dskip * xs
    y = y * _silu(z.astype(f32))
    normed = []
    for g in range(n_groups):
        yg = y[:, g * gw:(g + 1) * gw]
        normed.append(yg * lax.rsqrt(jnp.mean(yg * yg, axis=-1, keepdims=True) + EPS))
    return jnp.concatenate(normed, axis=1) * wnorm


def _ssd_prompt_body(xbc_ref, dt_ref, dtt_ref, z_ref, wc_ref, bc_ref, dtb_ref, dtbt_ref, a_ref, at_ref,
                     ex_ref, dk_ref, wn_ref, y_ref, st_ref, buf_ref, *, conv_w):
    c = pl.program_id(1)
    l = xbc_ref.shape[0]

    @pl.when(c == 0)
    def _():
        buf_ref[0:SUBLANES, :] = jnp.zeros((SUBLANES, buf_ref.shape[1]), f32)
        st_ref[...] = jnp.zeros(st_ref.shape, f32)

    buf_ref[SUBLANES:SUBLANES + l, :] = xbc_ref[...]
    conv = bc_ref[...]
    for k in range(conv_w):
        off = SUBLANES - (conv_w - 1) + k
        conv = conv + wc_ref[k:k + 1, :] * buf_ref[off:off + l, :]
    buf_ref[0:SUBLANES, :] = buf_ref[l:l + SUBLANES, :]
    y = _ssd_chunk(conv, dt_ref[...], dtt_ref[...], z_ref[...], st_ref, dtb_ref[...], dtbt_ref[...],
                   a_ref[...], at_ref[...], ex_ref[...], dk_ref[...], wn_ref[...], n_valid=None)
    y_ref[...] = y.astype(y_ref.dtype)


def _ssd_prompt(xbc, dt, dtt, z, ssm_w, *, bn, seq_len, n_state):
    r, conv_dim = xbc.shape
    d_inner = z.shape[1]
    l = math.gcd(seq_len, SSM_CHUNK)
    assert l % LANES == 0
    nc = seq_len // l
    wc, bc, dtb, dtbt, a_row, a_col, expand, dskip, wnorm = ssm_w
    conv_w = wc.shape[0]
    assert conv_w - 1 <= SUBLANES
    body = functools.partial(_ssd_prompt_body, conv_w=conv_w)
    consts = [wc, bc, dtb, dtbt, a_row, a_col, expand, dskip, wnorm]
    return pl.pallas_call(
        body,
        grid=(bn, nc),
        in_specs=[pl.BlockSpec((l, conv_dim), lambda b, c: (b * nc + c, 0)),
                  pl.BlockSpec((l, LANES), lambda b, c: (b * nc + c, 0)),
                  pl.BlockSpec((LANES, l), lambda b, c: (0, b * nc + c)),
                  pl.BlockSpec((l, d_inner), lambda b, c: (b * nc + c, 0))]
                 + [_const_spec(w.shape) for w in consts],
        out_specs=[pl.BlockSpec((l, d_inner), lambda b, c: (b * nc + c, 0)),
                   pl.BlockSpec((None, d_inner, n_state), lambda b, c: (b, 0, 0))],
        out_shape=[jax.ShapeDtypeStruct((r, d_inner), bf16),
                   jax.ShapeDtypeStruct((bn, d_inner, n_state), f32)],
        scratch_shapes=[pltpu.VMEM((l + SUBLANES, conv_dim), f32)],
        compiler_params=_params(2),
        name="ssd_prompt",
    )(xbc, dt, dtt, z, *consts)


def _ssd_sample_body(ext_ref, dt_ref, dtt_ref, z_ref, st_in_ref, wc_ref, bc_ref, dtb_ref, dtbt_ref, a_ref,
                     at_ref, ex_ref, dk_ref, wn_ref, y_ref, st_ref, *, conv_w, n_valid):
    lp = dt_ref.shape[0]
    st_ref[...] = st_in_ref[...]
    conv = bc_ref[...]
    for k in range(conv_w):
        conv = conv + wc_ref[k:k + 1, :] * ext_ref[k:k + lp, :]
    y = _ssd_chunk(conv, dt_ref[...], dtt_ref[...], z_ref[...], st_ref, dtb_ref[...], dtbt_ref[...],
                   a_ref[...], at_ref[...], ex_ref[...], dk_ref[...], wn_ref[...], n_valid=n_valid)
    y_ref[...] = y.astype(y_ref.dtype)


def _ssd_sample(ext, dt, dtt, z, state, ssm_w, *, n_valid):
    bs, ext_rows, conv_dim = ext.shape
    lp = dt.shape[1]
    d_inner = z.shape[2]
    n_state = state.shape[2]
    wc, bc, dtb, dtbt, a_row, a_col, expand, dskip, wnorm = ssm_w
    body = functools.partial(_ssd_sample_body, conv_w=wc.shape[0], n_valid=n_valid)
    consts = [wc, bc, dtb, dtbt, a_row, a_col, expand, dskip, wnorm]
    return pl.pallas_call(
        body,
        grid=(bs,),
        in_specs=[pl.BlockSpec((None, ext_rows, conv_dim), lambda s: (s, 0, 0)),
                  pl.BlockSpec((None, lp, LANES), lambda s: (s, 0, 0)),
                  pl.BlockSpec((None, LANES, lp), lambda s: (s, 0, 0)),
                  pl.BlockSpec((None, lp, d_inner), lambda s: (s, 0, 0)),
                  pl.BlockSpec((None, d_inner, n_state), lambda s: (s, 0, 0))]
                 + [_const_spec(w.shape) for w in consts],
        out_specs=[pl.BlockSpec((None, lp, d_inner), lambda s: (s, 0, 0)),
                   pl.BlockSpec((None, d_inner, n_state), lambda s: (s, 0, 0))],
        out_shape=[jax.ShapeDtypeStruct((bs, lp, d_inner), bf16),
                   jax.ShapeDtypeStruct((bs, d_inner, n_state), f32)],
        compiler_params=_params(1),
        name="ssd_sample",
    )(ext, dt, dtt, z, state, *consts)


def _sattn_body(pt_ref, q_ref, kn_ref, vn_ref, bprev_ref, bcur_ref, *rest, nbs, n_full, n_tok, ppb):
    n_pages = nbs * ppb
    kp = rest[:n_pages]
    vp = rest[n_pages:2 * n_pages]
    o_ref = rest[2 * n_pages]
    km_ref, ms_ref, ls_ref, os_ref = rest[2 * n_pages + 1:]
    g = pl.program_id(1)
    n_rows = N_HEADS * SUBLANES
    d = ATTN_DIM

    q4 = q_ref[...]
    q8 = jnp.concatenate([q4, jnp.zeros((SUBLANES - n_tok, d), f32)], axis=0)
    r_i = lax.broadcasted_iota(jnp.int32, (n_rows, d), 0)
    c_i = lax.broadcasted_iota(jnp.int32, (n_rows, d), 1)
    head_mask = (r_i // SUBLANES) == (c_i // HEAD_DIM)
    q_rows = jnp.where(head_mask, jnp.concatenate([q8] * N_HEADS, axis=0), 0.0)
    qb = q_rows.astype(bf16)

    @pl.when(g == 0)
    def _():
        km_ref[...] = jnp.zeros(km_ref.shape, f32)
        ms_ref[...] = jnp.zeros(ms_ref.shape, f32)
        ls_ref[...] = jnp.zeros(ls_ref.shape, f32)

    lane_blk = lax.broadcasted_iota(jnp.int32, (1, LANES), 1)
    key_lane = lax.broadcasted_iota(jnp.int32, (MOBA_BLOCK, LANES), 1)
    for b in range(nbs):
        kblk_t = jnp.concatenate([kp[b * ppb + t][...].reshape(d, -1) for t in range(ppb)], axis=1)
        vblk_t = jnp.concatenate([vp[b * ppb + t][...].reshape(d, -1) for t in range(ppb)], axis=1)
        jj = g * nbs + b
        k_hi = kblk_t.astype(bf16)
        k_lo = (kblk_t - k_hi.astype(f32)).astype(bf16)
        to_col = (key_lane == jj).astype(bf16)
        km_ref[...] += (_dot(k_hi, to_col) + _dot(k_lo, to_col)) * (1.0 / MOBA_BLOCK)
        s = _dot(qb, k_hi)
        s = s + jnp.where(jj == n_full - 1, 1.0, 0.0).astype(f32) * bprev_ref[...]
        m = jnp.max(s, axis=-1, keepdims=True)
        p = jnp.exp(s - m)
        l = jnp.sum(p, axis=-1, keepdims=True)
        here = (lane_blk == jj).astype(f32)
        ms_ref[...] += m * here
        ls_ref[...] += l * here
        os_ref[jj] = _dot_nt(p.astype(bf16), vblk_t.astype(bf16))

    @pl.when(g == pl.num_programs(1) - 1)
    def _():
        km = km_ref[...]
        q_hi = qb
        q_lo = (q_rows - q_hi.astype(f32)).astype(bf16)
        km_hi = km.astype(bf16)
        km_lo = (km - km_hi.astype(f32)).astype(bf16)
        sc = _dot(q_hi, km_hi) + (_dot(q_lo, km_hi) + _dot(q_hi, km_lo))
        col = lax.broadcasted_iota(jnp.int32, sc.shape, 1)
        low = jnp.float32(-3e38)
        sc = jnp.where(col < n_full, sc, low)
        self_ = jnp.zeros(sc.shape, f32)
        for _ in range(min(MOBA_TOPK, n_full)):
            mx = jnp.max(sc, axis=-1, keepdims=True)
            idx = jnp.min(jnp.where(sc == mx, col, LANES), axis=-1, keepdims=True)
            self_ = jnp.where(col == idx, 1.0, self_)
            sc = jnp.where(col == idx, low, sc)
        sel = self_ > 0.5
        kn = jnp.concatenate([kn_ref[...], jnp.zeros((SAMPLE_ROWS - n_tok, d), f32)], axis=0).astype(bf16)
        vn = jnp.concatenate([vn_ref[...], jnp.zeros((SAMPLE_ROWS - n_tok, d), f32)], axis=0).astype(bf16)
        s_cur = _dot_nt(qb, kn) + bcur_ref[...]
        ms = ms_ref[...]
        m_tot = jnp.maximum(jnp.max(s_cur, axis=-1, keepdims=True),
                            jnp.max(jnp.where(sel, ms, low), axis=-1, keepdims=True))
        p_cur = jnp.exp(s_cur - m_tot)
        w = jnp.where(sel, jnp.exp(ms - m_tot), 0.0)
        l_tot = jnp.sum(p_cur, axis=-1, keepdims=True) + jnp.sum(w * ls_ref[...], axis=-1, keepdims=True)
        acc = _dot(p_cur.astype(bf16), vn)
        for j in range(n_full):
            acc = acc + w[:, j:j + 1] * os_ref[j]
        out = jnp.where(head_mask, acc / l_tot, 0.0)
        out8 = out[0:SUBLANES]
        for h in range(1, N_HEADS):
            out8 = out8 + out[h * SUBLANES:(h + 1) * SUBLANES]
        o_ref[...] = out8[0:n_tok]


def _attn_sample(q, kn, vn, ck, cv, page_table, bprev, bcur):
    bs, n_tok, d = q.shape
    n_pool, n_heads, hd, page = ck.shape
    assert n_heads * hd == d
    ppb = MOBA_BLOCK // page
    n_pages_seq = page_table.shape[1]
    past = n_pages_seq * page
    assert MOBA_BLOCK % page == 0 and past % MOBA_BLOCK == 0 and n_tok <= SUBLANES
    n_full = past // MOBA_BLOCK
    assert n_full <= LANES
    nbs = math.gcd(n_full, SAMPLE_BLOCKS_PER_STEP)
    n_steps = n_full // nbs
    n_pages = nbs * ppb
    n_rows = N_HEADS * SUBLANES

    def page_spec(t):
        return pl.BlockSpec((None, n_heads, hd, page), lambda s, g, pt: (pt[s, g * n_pages + t], 0, 0, 0))

    seq_spec = pl.BlockSpec((None, n_tok, d), lambda s, g, pt: (s, 0, 0))
    body = functools.partial(_sattn_body, nbs=nbs, n_full=n_full, n_tok=n_tok, ppb=ppb)
    grid_spec = pltpu.PrefetchScalarGridSpec(
        num_scalar_prefetch=1,
        grid=(bs, n_steps),
        in_specs=[seq_spec, seq_spec, seq_spec,
                  pl.BlockSpec(bprev.shape, lambda s, g, pt: (0, 0)),
                  pl.BlockSpec(bcur.shape, lambda s, g, pt: (0, 0))]
                 + [page_spec(t) for t in range(n_pages)] * 2,
        out_specs=seq_spec,
        scratch_shapes=[pltpu.VMEM((d, LANES), f32),
                        pltpu.VMEM((n_rows, LANES), f32),
                        pltpu.VMEM((n_rows, LANES), f32),
                        pltpu.VMEM((n_full, n_rows, d), f32)],
    )
    return pl.pallas_call(
        body,
        grid_spec=grid_spec,
        out_shape=jax.ShapeDtypeStruct((bs, n_tok, d), f32),
        compiler_params=_params(2),
        name="moba_sample",
    )(page_table, q, kn, vn, bprev, bcur, *([ck] * n_pages), *([cv] * n_pages))


def _merge_body(at_ref, yn_ref, ga_ref, gb_ref, x_ref, g1_ref, wpa_ref, wps_ref, wo_ref, lg_ref, lb_ref,
                o_ref, *, alpha):
    pa = _dot(at_ref[...].astype(bf16), wpa_ref[...])
    ps = _dot(yn_ref[...].astype(bf16), wps_ref[...])
    merged = jax.nn.sigmoid(ga_ref[...].astype(f32)) * pa + jax.nn.sigmoid(gb_ref[...].astype(f32)) * ps
    mo = _dot(merged.astype(bf16), wo_ref[...])
    o_ref[...] = _layer_norm(alpha * x_ref[...] + g1_ref[...] * mo, lg_ref[...], lb_ref[...])


def _mod_spec(per_row_mod, tm, d, tiles_per_seq):
    if per_row_mod:
        return pl.BlockSpec((tm, d), lambda i: (i, 0))
    return pl.BlockSpec((None, 1, d), lambda i: (i // tiles_per_seq, 0, 0))


def _merge(attn, yn, ga, gb, x2d, gate1, wpa, wps, wo, lg, lb, *, per_row_mod, seq_len, alpha):
    r, d = x2d.shape
    tm = 256
    assert r % tm == 0
    tiles_per_seq = 1 if per_row_mod else seq_len // tm
    ms = _mod_spec(per_row_mod, tm, d, tiles_per_seq)

    def row_spec(w):
        return pl.BlockSpec((tm, w), lambda i: (i, 0))

    return pl.pallas_call(
        functools.partial(_merge_body, alpha=alpha),
        grid=(r // tm,),
        in_specs=[row_spec(attn.shape[1]), row_spec(yn.shape[1]), row_spec(d), row_spec(d), row_spec(d), ms,
                  _const_spec(wpa.shape), _const_spec(wps.shape), _const_spec(wo.shape),
                  _const_spec(lg.shape), _const_spec(lb.shape)],
        out_specs=row_spec(d),
        out_shape=jax.ShapeDtypeStruct((r, d), f32),
        compiler_params=_params(1),
        name="merge_ln1",
    )(attn, yn, ga, gb, x2d, gate1, wpa, wps, wo, lg, lb)


FFN_COL_CHUNKS = 2


def _ffn_body(x_ref, sh_ref, sc_ref, g2_ref, wu_ref, wc_ref, bc_ref, wd_ref, lg_ref, lb_ref, *rest,
              alpha, conv_w, tiles_per_seq, sample_len):
    if sample_len is None:
        y_ref, tail_ref, buf_ref, carry_ref = rest
    else:
        p_refs = rest[:conv_w - 1]
        y_ref, hup_ref, buf_ref, carry_ref = rest[conv_w - 1:]
    x = x_ref[...]
    tm = x.shape[0]
    ff = wd_ref.shape[0]
    cw = ff // FFN_COL_CHUNKS
    t = pl.program_id(0) % tiles_per_seq

    @pl.when(t == 0)
    def _():
        carry_ref[...] = jnp.zeros(carry_ref.shape, f32)

    u = (x * (1.0 + sc_ref[...]) + sh_ref[...]).astype(bf16)
    if sample_len is not None:
        tmod = lax.broadcasted_iota(jnp.int32, (tm, cw), 0) % sample_len
    f = jnp.zeros((tm, x.shape[1]), f32)
    for c in range(FFN_COL_CHUNKS):
        halves = []
        for half in range(2):
            c0 = half * ff + c * cw
            buf_ref[0:SUBLANES, :] = carry_ref[:, c0:c0 + cw]
            hup = _dot(u, wu_ref[:, c0:c0 + cw])
            buf_ref[SUBLANES:SUBLANES + tm, :] = hup
            if sample_len is not None:
                hup_ref[:, c0:c0 + cw] = hup
            hc = bc_ref[:, c0:c0 + cw] + wc_ref[conv_w - 1:conv_w, c0:c0 + cw] * hup
            for k in range(conv_w - 1):
                back = conv_w - 1 - k
                prev = buf_ref[SUBLANES - back:SUBLANES - back + tm, :]
                if sample_len is not None:
                    prev = jnp.where(tmod >= back, prev, p_refs[back - 1][:, c0:c0 + cw])
                hc = hc + wc_ref[k:k + 1, c0:c0 + cw] * prev
            carry_ref[:, c0:c0 + cw] = buf_ref[tm:tm + SUBLANES, :]
            halves.append(hc)
        gact = (_silu(halves[0]) * halves[1]).astype(bf16)
        f = f + _dot(gact, wd_ref[c * cw:(c + 1) * cw, :])
    y_ref[...] = _layer_norm(alpha * x + g2_ref[...] * f, lg_ref[...], lb_ref[...])
    if sample_len is None:
        @pl.when(t == tiles_per_seq - 1)
        def _():
            tail_ref[...] = carry_ref[...]


def _ffn(x2d, shift, scale, gate, wu, wc, bc, wd, lg, lb, prevs, *, per_row_mod, seq_len, bn, alpha,
         sample_len):
    r, d = x2d.shape
    ff2 = wu.shape[1]
    ff = wd.shape[0]
    conv_w = wc.shape[0]
    tm = 256 if sample_len is None else 128
    assert r % tm == 0 and ff % (FFN_COL_CHUNKS * LANES) == 0 and conv_w - 1 <= SUBLANES
    tiles_per_seq = 1 if per_row_mod else seq_len // tm
    ms = _mod_spec(per_row_mod, tm, d, tiles_per_seq)

    def row_spec(w):
        return pl.BlockSpec((tm, w), lambda i: (i, 0))

    in_specs = [row_spec(d), ms, ms, ms, _const_spec(wu.shape), _const_spec(wc.shape), _const_spec(bc.shape),
                _const_spec(wd.shape), _const_spec(lg.shape), _const_spec(lb.shape)]
    args = [x2d, shift, scale, gate, wu, wc, bc, wd, lg, lb]
    if sample_len is None:
        out_shape = [jax.ShapeDtypeStruct((r, d), f32), jax.ShapeDtypeStruct((bn, SUBLANES, ff2), f32)]
        out_specs = [row_spec(d), pl.BlockSpec((None, SUBLANES, ff2), lambda i: (i // tiles_per_seq, 0, 0))]
    else:
        assert tm % sample_len == 0 and len(prevs) == conv_w - 1
        in_specs += [row_spec(ff2)] * len(prevs)
        args += list(prevs)
        out_shape = [jax.ShapeDtypeStruct((r, d), f32), jax.ShapeDtypeStruct((r, ff2), f32)]
        out_specs = [row_spec(d), row_spec(ff2)]
    body = functools.partial(_ffn_body, alpha=alpha, conv_w=conv_w, tiles_per_seq=tiles_per_seq,
                             sample_len=sample_len)
    return pl.pallas_call(
        body,
        grid=(r // tm,),
        in_specs=in_specs,
        out_specs=out_specs,
        out_shape=out_shape,
        scratch_shapes=[pltpu.VMEM((tm + SUBLANES, ff // FFN_COL_CHUNKS), f32),
                        pltpu.VMEM((SUBLANES, ff2), f32)],
        compiler_params=_params(1),
        name="conv_ffn",
    )(*args)


def _rel_bucket(dist):
    n = jnp.maximum(dist, 0)
    max_exact = REL_BUCKETS // 2
    nf = jnp.maximum(n, 1).astype(f32)
    large = max_exact + (jnp.log(nf / max_exact) / math.log(REL_MAX_DIST / max_exact)
                         * (REL_BUCKETS - max_exact)).astype(jnp.int32)
    large = jnp.minimum(large, REL_BUCKETS - 1)
    return jnp.where(n < max_exact, n, large)


def _rel_bias(rel_table, dist):
    rel = rel_table - rel_table[REL_BUCKETS - 1]
    onehot = (_rel_bucket(dist)[..., None] == jnp.arange(REL_BUCKETS)).astype(f32)
    b = jnp.dot(onehot, rel, precision=lax.Precision.HIGHEST)
    return jnp.moveaxis(b, -1, 0)


def _prompt_bias_tables(rel_table):
    a = jnp.arange(MOBA_BLOCK, dtype=jnp.int32)
    dist_d = a[None, :] - a[:, None]
    bd = jnp.where(dist_d >= 0, _rel_bias(rel_table, dist_d) * LOG2E, MASKED)
    bp = _rel_bias(rel_table, dist_d + MOBA_BLOCK) * LOG2E
    return bd.astype(f32), bp.astype(f32)


def _sample_bias_tables(rel_table, n_tok):
    n_rows = N_HEADS * SUBLANES
    t = jnp.arange(SUBLANES, dtype=jnp.int32)
    a = jnp.arange(MOBA_BLOCK, dtype=jnp.int32)
    bprev = _rel_bias(rel_table, MOBA_BLOCK + t[:, None] - a[None, :])
    bprev = jnp.where((t < n_tok)[None, :, None], bprev, 0.0).reshape(n_rows, MOBA_BLOCK)
    tk = jnp.arange(SAMPLE_ROWS, dtype=jnp.int32)
    dist = t[:, None] - tk[None, :]
    ok = (dist >= 0) & (t[:, None] < n_tok) & (tk[None, :] < n_tok)
    bcur = jnp.where(ok[None], _rel_bias(rel_table, dist), MASKED).reshape(n_rows, SAMPLE_ROWS)
    return bprev.astype(f32), bcur.astype(f32)


def _pad_cols(w, n):
    return jnp.pad(w, ((0, 0), (0, n - w.shape[1])))


def kernel(x_prompt, x_sample, cache_k, cache_v, page_table, state_ssm, state_conv_ssm, state_conv_ffn,
           c_prompt, c_sample, rel_table, w_ada, b_ada, w_in, w_conv_ssm, b_conv_ssm, dt_bias, a_log,
           d_skip, w_norm_ssm, w_proj_attn, w_proj_ssm, w_out, ln1_g, ln1_b, w_up, w_conv_ffn, b_conv_ffn,
           w_down, ln2_g, ln2_b):
    depth = w_ada.shape[0]
    alpha = (2 * depth) ** 0.25
    bp_, seq, d = x_prompt.shape
    bs, n_tok, _ = x_sample.shape
    ssm_heads = dt_bias.shape[1]
    d_inner = ssm_heads * SSM_HEAD_DIM
    n_state = state_ssm.shape[-1]
    conv_dim = w_conv_ssm.shape[-1]
    ff2 = w_up.shape[-1]
    ssm_conv = w_conv_ssm.shape[1]
    ffn_conv = w_conv_ffn.shape[1]
    assert ssm_heads <= LANES and n_tok <= SUBLANES
    page = cache_k.shape[2]
    scale = HEAD_DIM ** -0.5

    bd, bpv = _prompt_bias_tables(rel_table)
    bprev_s, bcur_s = _sample_bias_tables(rel_table, n_tok)
    expand = (jnp.arange(LANES)[:, None] == (jnp.arange(d_inner)[None, :] // SSM_HEAD_DIM)).astype(bf16)

    yp = x_prompt.reshape(bp_ * seq, d)
    ys = x_sample.reshape(bs * n_tok, d)
    outs_p = [[] for _ in range(5)]
    outs_s = [[] for _ in range(5)]
    for l in range(depth):
        cuts = [ATTN_DIM, 2 * ATTN_DIM, 3 * ATTN_DIM, 3 * ATTN_DIM + d_inner,
                3 * ATTN_DIM + d_inner + conv_dim, 3 * ATTN_DIM + d_inner + conv_dim + ssm_heads,
                3 * ATTN_DIM + d_inner + conv_dim + ssm_heads + d]
        wq, wk, wv, wz, wxbc, wdt, wga, wgb = jnp.split(w_in[l], cuts, axis=1)
        wdt = _pad_cols(wdt, LANES)
        wq = wq * scale
        nat_p = [(ATTN_DIM, f32), (ATTN_DIM, f32), (d_inner, bf16), (conv_dim, f32), (LANES, f32),
                 (d, bf16), (d, bf16)]
        wn_p = jnp.concatenate([wk, wv, wz, wxbc, wdt, wga, wgb], axis=1).astype(bf16)
        tr_p = [(ATTN_DIM, bf16), (ATTN_DIM, bf16), (LANES, f32)]
        wt_p = jnp.concatenate([wq * LOG2E, wv, wdt], axis=1).T.astype(bf16)
        nat_s = [(ATTN_DIM, f32)] + nat_p
        wn_s = jnp.concatenate([wq.astype(bf16), wn_p], axis=1)
        tr_s = [(LANES, f32)]
        wt_s = wdt.T.astype(bf16)
        ssm_w = (w_conv_ssm[l], b_conv_ssm[l].reshape(1, conv_dim),
                 _pad_cols(dt_bias[l].reshape(1, -1), LANES), _pad_cols(dt_bias[l].reshape(1, -1), LANES).T,
                 -jnp.exp(_pad_cols(a_log[l].reshape(1, -1), LANES)),
                 -jnp.exp(_pad_cols(a_log[l].reshape(1, -1), LANES)).T,
                 expand, jnp.repeat(d_skip[l], SSM_HEAD_DIM).reshape(1, d_inner),
                 w_norm_ssm[l].reshape(1, d_inner))
        wpa, wps, wo = (w_proj_attn[l].astype(bf16), w_proj_ssm[l].astype(bf16), w_out[l].astype(bf16))
        lg1, lb1 = ln1_g[l].reshape(1, d), ln1_b[l].reshape(1, d)
        lg2, lb2 = ln2_g[l].reshape(1, d), ln2_b[l].reshape(1, d)
        wu, wd = w_up[l].astype(bf16), w_down[l].astype(bf16)
        wcf, bcf = w_conv_ffn[l], b_conv_ffn[l].reshape(1, ff2)

        mod = _ada(jnp.concatenate([c_prompt, c_sample], axis=0), w_ada[l], b_ada[l])
        mod_p = mod[:bp_].reshape(bp_, 6, 1, d)
        mod_s = jnp.repeat(mod[bp_:].reshape(bs, 6, 1, d), n_tok, axis=2).reshape(bs, 6, n_tok, d)
        mod_s = jnp.moveaxis(mod_s, 1, 0).reshape(6, bs * n_tok, d)
        sh1p, sc1p, g1p, sh2p, sc2p, g2p = (mod_p[:, i] for i in range(6))
        sh1s, sc1s, g1s, sh2s, sc2s, g2s = (mod_s[i] for i in range(6))

        k_p, v_p, z_p, xbc_p, dt_p, ga_p, gb_p, qT_p, vT_p, dtT_p, k2_p = _inproj(
            yp, sh1p, sc1p, wn_p, wt_p, nat_p, tr_p, per_row_mod=False, seq_len=seq, attn_layout=True)
        attn_p = _attn_prompt(k2_p, qT_p, vT_p, bd, bpv, bn=bp_, seq_len=seq)
        yn_p, st_p = _ssd_prompt(xbc_p, dt_p, dtT_p, z_p, ssm_w, bn=bp_, seq_len=seq, n_state=n_state)
        x1_p = _merge(attn_p, yn_p, ga_p, gb_p, yp, g1p, wpa, wps, wo, lg1, lb1,
                      per_row_mod=False, seq_len=seq, alpha=alpha)
        yp, tail_p = _ffn(x1_p, sh2p, sc2p, g2p, wu, wcf, bcf, wd, lg2, lb2, (),
                          per_row_mod=False, seq_len=seq, bn=bp_, alpha=alpha, sample_len=None)
        outs_p[0].append(k_p.reshape(bp_, seq, N_HEADS, HEAD_DIM))
        outs_p[1].append(v_p.reshape(bp_, seq, N_HEADS, HEAD_DIM))
        outs_p[2].append(st_p.reshape(bp_, ssm_heads, SSM_HEAD_DIM, n_state))
        outs_p[3].append(xbc_p.reshape(bp_, seq, conv_dim)[:, seq - (ssm_conv - 1):])
        outs_p[4].append(tail_p[:, SUBLANES - (ffn_conv - 1):])

        r_s = bs * n_tok
        r_pad = -(-r_s // 256) * 256

        def pad_rows(a):
            return jnp.pad(a, ((0, r_pad - r_s), (0, 0)))

        q_s, k_s, v_s, z_s, xbc_s, dt_s, ga_s, gb_s, dtT_s = _inproj(
            pad_rows(ys), pad_rows(sh1s), pad_rows(sc1s), wn_s, wt_s, nat_s, tr_s,
            per_row_mod=True, seq_len=None, attn_layout=False)
        q_s, k_s, v_s = (a[:r_s].reshape(bs, n_tok, ATTN_DIM) for a in (q_s, k_s, v_s))
        attn_s = _attn_sample(q_s, k_s, v_s, jnp.transpose(cache_k[l], (0, 2, 3, 1)),
                              jnp.transpose(cache_v[l], (0, 2, 3, 1)), page_table, bprev_s, bcur_s)
        row_pad = SAMPLE_ROWS - n_tok
        xbc_s3 = xbc_s[:r_s].reshape(bs, n_tok, conv_dim)
        ext = jnp.concatenate([state_conv_ssm[l], xbc_s3,
                               jnp.zeros((bs, row_pad + SUBLANES - (ssm_conv - 1), conv_dim), f32)], axis=1)
        dt_s3 = jnp.pad(dt_s[:r_s].reshape(bs, n_tok, LANES), ((0, 0), (0, row_pad), (0, 0)))
        dtT_s3 = jnp.pad(jnp.moveaxis(dtT_s[:, :r_s].reshape(LANES, bs, n_tok), 0, 1),
                         ((0, 0), (0, 0), (0, row_pad)))
        z_s3 = jnp.pad(z_s[:r_s].reshape(bs, n_tok, d_inner), ((0, 0), (0, row_pad), (0, 0)))
        yn_s3, st_s = _ssd_sample(ext, dt_s3, dtT_s3, z_s3, state_ssm[l].reshape(bs, d_inner, n_state),
                                  ssm_w, n_valid=n_tok)
        yn_s = yn_s3[:, :n_tok].reshape(r_s, d_inner)
        x1_s = _merge(pad_rows(attn_s.reshape(r_s, ATTN_DIM)), pad_rows(yn_s), ga_s, gb_s, pad_rows(ys),
                      pad_rows(g1s), wpa, wps, wo, lg1, lb1, per_row_mod=True, seq_len=None, alpha=alpha)
        cf = state_conv_ffn[l]
        prevs = []
        for back in range(1, ffn_conv):
            rows = [cf[:, ffn_conv - 1 - back + t] if t < back else jnp.zeros((bs, ff2), f32)
                    for t in range(n_tok)]
            prevs.append(pad_rows(jnp.stack(rows, axis=1).reshape(r_s, ff2)))
        y_s, hup_s = _ffn(x1_s, pad_rows(sh2s), pad_rows(sc2s), pad_rows(g2s), wu, wcf, bcf, wd, lg2, lb2,
                          prevs, per_row_mod=True, seq_len=None, bn=bs, alpha=alpha, sample_len=n_tok)
        ys = y_s[:r_s]
        outs_s[0].append(k_s.reshape(bs, n_tok, N_HEADS, HEAD_DIM))
        outs_s[1].append(v_s.reshape(bs, n_tok, N_HEADS, HEAD_DIM))
        outs_s[2].append(st_s.reshape(bs, ssm_heads, SSM_HEAD_DIM, n_state))
        cs_ext = jnp.concatenate([state_conv_ssm[l], xbc_s3], axis=1)
        outs_s[3].append(cs_ext[:, -(ssm_conv - 1):])
        cf_ext = jnp.concatenate([cf, hup_s[:r_s].reshape(bs, n_tok, ff2)], axis=1)
        outs_s[4].append(cf_ext[:, -(ffn_conv - 1):])

    return (yp.reshape(bp_, seq, d), ys.reshape(bs, n_tok, d),
            *(jnp.stack(o) for o in outs_p), *(jnp.stack(o) for o in outs_s))
```

```python
import functools
import math

import jax
import jax.numpy as jnp
from jax import lax
from jax.experimental import pallas as pl
from jax.experimental.pallas import tpu as pltpu

N_HEADS = 8
HEAD_DIM = 64
ATTN_DIM = N_HEADS * HEAD_DIM
MOBA_BLOCK = 256
MOBA_TOPK = 3
REL_BUCKETS = 32
REL_MAX_DIST = 128
SSM_HEAD_DIM = 64
SSM_CHUNK = 256
EPS = 1e-5

LANES = 128
SUBLANES = 8
BF16_ROWS = 16
VMEM_LIMIT = 56 * 1024 * 1024

MASKED = -1e30
SAMPLE_ROWS = 16
SAMPLE_BLOCKS_PER_STEP = 4
FAR_GROUP = 4
LOG2E = math.log2(math.e)

f32 = jnp.float32
bf16 = jnp.bfloat16

_NT = (((1,), (1,)), ((), ()))
_TN = (((0,), (0,)), ((), ()))


def _dot(a, b):
    return jnp.dot(a, b, preferred_element_type=f32)


def _dot_nt(a, b):
    return lax.dot_general(a, b, _NT, preferred_element_type=f32)


def _dot_tn(a, b):
    return lax.dot_general(a, b, _TN, preferred_element_type=f32)


def _split3(x):
    hi = x.astype(bf16)
    r = x - hi.astype(f32)
    mid = r.astype(bf16)
    lo = (r - mid.astype(f32)).astype(bf16)
    return hi, mid, lo


def _dot_exact_rhs(a, m):
    hi, mid, lo = _split3(a)
    return _dot(hi, m) + _dot(mid, m) + _dot(lo, m)


def _dot_exact_lhs(m, a):
    hi, mid, lo = _split3(a)
    return _dot(m, hi) + _dot(m, mid) + _dot(m, lo)


def _silu(x):
    return x * jax.nn.sigmoid(x)


def _softplus(x):
    return jnp.maximum(x, 0.0) + jnp.log1p(jnp.exp(-jnp.abs(x)))


def _layer_norm(x, g, b):
    mu = jnp.mean(x, axis=-1, keepdims=True)
    xc = x - mu
    var = jnp.mean(xc * xc, axis=-1, keepdims=True)
    return xc * lax.rsqrt(var + EPS) * g + b


def _const_spec(shape):
    nd = len(shape)
    return pl.BlockSpec(shape, lambda *_: (0,) * nd, pipeline_mode=pl.Buffered(1))


def _params(n_grid):
    return pltpu.CompilerParams(
        dimension_semantics=("arbitrary",) * n_grid, vmem_limit_bytes=VMEM_LIMIT)


def _ada_body(c_ref, w_ref, b_ref, o_ref):
    a = _silu(c_ref[...])
    hi, mid, lo = _split3(a)
    whi, wmid, wlo = _split3(w_ref[...])
    acc = _dot(hi, whi) + (_dot(hi, wmid) + _dot(mid, whi))
    acc = acc + (_dot(hi, wlo) + _dot(mid, wmid) + _dot(lo, whi))
    o_ref[...] = acc + b_ref[...]


def _ada(c, w, b):
    n, d = c.shape
    dn = w.shape[1]
    tn = 1024 if dn % 1024 == 0 else dn
    return pl.pallas_call(
        _ada_body,
        grid=(dn // tn,),
        in_specs=[pl.BlockSpec((n, d), lambda j: (0, 0)),
                  pl.BlockSpec((d, tn), lambda j: (0, j)),
                  pl.BlockSpec((1, tn), lambda j: (0, j))],
        out_specs=pl.BlockSpec((n, tn), lambda j: (0, j)),
        out_shape=jax.ShapeDtypeStruct((n, dn), f32),
        compiler_params=_params(1),
        name="ada_mod",
    )(c, w, b.reshape(1, dn))


def _inproj_body(x_ref, sh_ref, sc_ref, wn_ref, wt_ref, *outs, nat, tr, k_col, tiles_per_seq):
    u = (x_ref[...] * (1.0 + sc_ref[...]) + sh_ref[...]).astype(bf16)
    tm = u.shape[0]
    o = 0
    col = 0
    for width, _ in nat:
        res = _dot(u, wn_ref[:, col:col + width])
        outs[o][...] = res.astype(outs[o].dtype)
        if k_col is not None and col == k_col:
            kf = res
        o += 1
        col += width
    if tr:
        t = _dot_nt(wt_ref[...], u)
        row = 0
        for height, _, ones_rows in tr:
            if ones_rows:
                step = HEAD_DIM + ones_rows
                for h in range(height // HEAD_DIM):
                    outs[o][h * step:h * step + HEAD_DIM, :] = (
                        t[row + h * HEAD_DIM:row + (h + 1) * HEAD_DIM].astype(outs[o].dtype))
                    outs[o][h * step + HEAD_DIM:(h + 1) * step, :] = jnp.ones((ones_rows, tm), outs[o].dtype)
            else:
                outs[o][...] = t[row:row + height].astype(outs[o].dtype)
            o += 1
            row += height
    if k_col is not None:
        pos = (pl.program_id(0) % tiles_per_seq) * tm
        blk = pos // MOBA_BLOCK
        onehot = (lax.broadcasted_iota(jnp.int32, (tm, LANES), 1) == blk).astype(bf16)
        k2_ref = outs[o]
        for p in range(ATTN_DIM // LANES):
            k2_ref[p] = jnp.concatenate([kf[:, p * LANES:(p + 1) * LANES].astype(bf16), onehot], axis=1)


def _inproj(x2d, shift, scale, wn, wt, nat, tr, *, per_row_mod, seq_len, attn_layout):
    r, d = x2d.shape
    tm = 256
    assert r % tm == 0
    n_tiles = r // tm
    if per_row_mod:
        mod_spec = pl.BlockSpec((tm, d), lambda i: (i, 0))
        tiles_per_seq = 1
    else:
        assert seq_len % tm == 0 and MOBA_BLOCK % tm == 0
        tiles_per_seq = seq_len // tm
        mod_spec = pl.BlockSpec((None, 1, d), lambda i: (i // tiles_per_seq, 0, 0))
    out_shape, out_specs = [], []
    for width, dt in nat:
        out_shape.append(jax.ShapeDtypeStruct((r, width), dt))
        out_specs.append(pl.BlockSpec((tm, width), lambda i: (i, 0)))
    for height, dt, ones_rows in tr:
        rows = height + (height // HEAD_DIM) * ones_rows
        out_shape.append(jax.ShapeDtypeStruct((rows, r), dt))
        out_specs.append(pl.BlockSpec((rows, tm), lambda i: (0, i)))
    k_col = None
    if attn_layout:
        k_col = 0
        n_pairs = ATTN_DIM // LANES
        out_shape.append(jax.ShapeDtypeStruct((n_pairs, r, 2 * LANES), bf16))
        out_specs.append(pl.BlockSpec((n_pairs, tm, 2 * LANES), lambda i: (0, i, 0)))
    body = functools.partial(_inproj_body, nat=tuple(nat), tr=tuple(tr), k_col=k_col,
                             tiles_per_seq=tiles_per_seq)
    return pl.pallas_call(
        body,
        grid=(n_tiles,),
        in_specs=[pl.BlockSpec((tm, d), lambda i: (i, 0)), mod_spec, mod_spec,
                  _const_spec(wn.shape), _const_spec(wt.shape)],
        out_specs=out_specs,
        out_shape=out_shape,
        compiler_params=_params(1),
        name="in_proj",
    )(x2d, shift, scale, wn, wt)


def _select_topk_rows(s, n_valid_rows):
    nblk = s.shape[0]
    row = lax.broadcasted_iota(jnp.int32, s.shape, 0)
    low = jnp.float32(-3e38)
    s = jnp.where(row < n_valid_rows, s, low)
    sel = jnp.zeros(s.shape, jnp.bool_)
    for _ in range(MOBA_TOPK):
        m = jnp.max(s, axis=0, keepdims=True)
        idx = jnp.min(jnp.where(s == m, row, nblk), axis=0, keepdims=True)
        hit = (row == idx) & (m > low)
        sel = sel | hit
        s = jnp.where(row == idx, low, s)
    return sel


def _attn_body(k2_ref, qT_ref, vT_ref, bd_ref, bp_ref, o_ref, km_ref, q2n_ref, q2f_ref, sa_ref, sb_ref, *,
               n_blocks, group):
    i = pl.program_id(2)
    blk = MOBA_BLOCK
    hd = HEAD_DIM
    nsel = km_ref.shape[0]

    @pl.when(i == 0)
    def _():
        km_ref[...] = jnp.zeros(km_ref.shape, f32)

        def mean_body(j, c):
            kk = k2_ref[pl.ds(pl.multiple_of(j * blk, blk), blk), :].astype(f32)
            km_ref[pl.ds(j, 1), :] = jnp.sum(kk, axis=0, keepdims=True) * (1.0 / blk)
            return c
        lax.fori_loop(0, n_blocks, mean_body, 0)

    q_pair = qT_ref[:, pl.ds(pl.multiple_of(i * blk, blk), blk)]
    zq = jnp.zeros((hd, blk), bf16)
    q_rows = jnp.concatenate([jnp.concatenate([q_pair[0:hd], zq], axis=1),
                              jnp.concatenate([zq, q_pair[hd:2 * hd]], axis=1)], axis=0)
    zrest = jnp.zeros((q2f_ref.shape[0] - 2 * hd, 2 * blk), bf16)
    q2f_ref[0:2 * hd, :] = q_rows
    q2f_ref[2 * hd:, :] = zrest
    q2n_ref[0:2 * hd, :] = q_rows
    q2n_ref[2 * hd:, :] = zrest
    km = km_ref[...]
    km_hi = km.astype(bf16)
    km_lo = (km - km_hi.astype(f32)).astype(bf16)
    q2 = q2f_ref[...]
    s_blk = _dot(km_hi, q2) + _dot(km_lo, q2)
    sel = _select_topk_rows(s_blk, i)
    rowsel = lax.broadcasted_iota(jnp.int32, (nsel, 2 * blk), 0)
    far = sel & (rowsel < i - 1)
    near = (sel & (rowsel == i - 1)) | (rowsel == i)
    q2f_ref[2 * hd:2 * hd + nsel, :] = jnp.where(far, 0.0, MASKED).astype(bf16)
    q2n_ref[2 * hd:2 * hd + nsel, :] = jnp.where(near, 0.0, MASKED).astype(bf16)

    def k_rows(j, n):
        return k2_ref[pl.ds(pl.multiple_of(j * blk, blk), n * blk), :]

    vrows = vT_ref.shape[0] // 2

    def v_cols(h, j, n):
        return vT_ref[h * vrows:(h + 1) * vrows, pl.ds(pl.multiple_of(j * blk, blk), n * blk)]

    jp = jnp.maximum(i - 1, 0)
    first_pad = jnp.where(i > 0, 0.0, MASKED).astype(f32)
    s_cur = _dot(k_rows(i, 1), q2n_ref[...])
    s_prev = _dot(k_rows(jp, 1), q2n_ref[...])
    carry = []
    for h in range(2):
        s = jnp.concatenate([s_cur[:, h * blk:(h + 1) * blk] + bd_ref[h],
                             s_prev[:, h * blk:(h + 1) * blk] + (bp_ref[h] + first_pad)], axis=0)
        m = jnp.max(s, axis=0, keepdims=True)
        p = jnp.exp2((s - m).astype(bf16))
        v = jnp.concatenate([v_cols(h, i, 1), v_cols(h, jp, 1)], axis=1)
        carry += [m, _dot(v, p)]

    last_group = n_blocks // group - 1

    def logits_into(s_ref, g):
        s_ref[...] = _dot(k_rows(jnp.minimum(g, last_group) * group, group), q2f_ref[...])

    def consume(s_ref, g, carry):
        j0 = jnp.minimum(g, last_group) * group
        pad = jnp.where(g < n_groups, 0.0, MASKED).astype(f32)
        out = []
        for h in range(2):
            m, acc = carry[2 * h:2 * h + 2]
            m2 = jnp.maximum(m, jnp.max(s_ref[:, h * blk:(h + 1) * blk], axis=0, keepdims=True) + pad)
            a = jnp.exp2(m - m2)
            p = jnp.exp2((s_ref[:, h * blk:(h + 1) * blk] - (m2 - pad)).astype(bf16))
            out += [m2, a * acc + _dot(v_cols(h, j0, group), p)]
        return out

    def far_body(t, carry):
        logits_into(sb_ref, 2 * t + 1)
        carry = consume(sa_ref, 2 * t, list(carry))
        logits_into(sa_ref, 2 * t + 2)
        return tuple(consume(sb_ref, 2 * t + 1, carry))

    n_groups = (jnp.maximum(i - 1, 0) + group - 1) // group
    logits_into(sa_ref, 0)
    carry = lax.fori_loop(0, (n_groups + 1) // 2, far_body, tuple(carry))
    outs = [carry[2 * h + 1][0:hd] / carry[2 * h + 1][hd:hd + 1] for h in range(2)]
    o_ref[...] = jnp.concatenate(outs, axis=0).T.astype(o_ref.dtype)


def _attn_prompt(k2, qT, vT, bd, bp, *, bn, seq_len):
    n_pairs, r, _ = k2.shape
    blk = MOBA_BLOCK
    assert seq_len % blk == 0
    nb = seq_len // blk
    nsel = HEAD_DIM
    assert nb <= nsel
    group = math.gcd(nb, FAR_GROUP)
    body = functools.partial(_attn_body, n_blocks=nb, group=group)
    return pl.pallas_call(
        body,
        grid=(bn, n_pairs, nb),
        in_specs=[pl.BlockSpec((None, seq_len, 2 * LANES), lambda b, p, i: (p, b, 0)),
                  pl.BlockSpec((LANES, seq_len), lambda b, p, i: (p, b)),
                  pl.BlockSpec((vT.shape[0] // n_pairs, seq_len), lambda b, p, i: (p, b)),
                  pl.BlockSpec((2, blk, blk), lambda b, p, i: (p, 0, 0)),
                  pl.BlockSpec((2, blk, blk), lambda b, p, i: (p, 0, 0))],
        out_specs=pl.BlockSpec((blk, LANES), lambda b, p, i: (b * nb + i, p)),
        out_shape=jax.ShapeDtypeStruct((r, ATTN_DIM), bf16),
        scratch_shapes=[pltpu.VMEM((nsel, 2 * LANES), f32),
                        pltpu.VMEM((2 * LANES, 2 * blk), bf16),
                        pltpu.VMEM((2 * LANES, 2 * blk), bf16),
                        pltpu.VMEM((group * blk, 2 * blk), f32),
                        pltpu.VMEM((group * blk, 2 * blk), f32)],
        compiler_params=_params(3),
        name="moba_prompt",
    )(k2, qT, vT, bd, bp)


def _ssd_chunk(conv, dt_raw, dt_raw_t, z, st_ref, dtb, dtb_t, a_row, a_col, expand, dskip, wnorm,
               *, n_valid):
    l = conv.shape[0]
    d_inner = z.shape[1]
    n_state = st_ref.shape[1]
    n_heads = d_inner // SSM_HEAD_DIM
    n_groups = (conv.shape[1] - d_inner) // (2 * n_state)
    hpg = n_heads // n_groups
    gw = hpg * SSM_HEAD_DIM
    last = (l if n_valid is None else n_valid) - 1

    act = _silu(conv)
    xs = act[:, :d_inner]
    bm = act[:, d_inner:d_inner + n_groups * n_state]
    cm = act[:, d_inner + n_groups * n_state:]

    dt = _softplus(dt_raw + dtb)
    dt_t = _softplus(dt_raw_t + dtb_t)
    r_i = lax.broadcasted_iota(jnp.int32, (l, l), 0)
    c_i = lax.broadcasted_iota(jnp.int32, (l, l), 1)
    causal = r_i >= c_i
    tril = causal.astype(bf16)
    triu = (r_i <= c_i).astype(bf16)
    acs = _dot_exact_lhs(tril, dt * a_row)
    acs_t = _dot_exact_rhs(dt_t * a_col, triu)
    eacs = jnp.exp(acs)
    dec = jnp.exp(acs[last:last + 1, :] - acs)
    ea_t = jnp.exp(acs_t)

    dt_full = _dot_exact_rhs(dt, expand)
    eacs_full = _dot_exact_rhs(eacs, expand)
    dec_full = _dot_exact_rhs(dec, expand)
    xd = xs * dt_full
    xdd = xd * dec_full
    if n_valid is not None:
        rows = lax.broadcasted_iota(jnp.int32, xdd.shape, 0)
        xdd = jnp.where(rows < n_valid, xdd, 0.0)

    lane = lax.broadcasted_iota(jnp.int32, (l, 2 * SSM_HEAD_DIM), 1)
    y_parts = []
    for g in range(n_groups):
        bg = bm[:, g * n_state:(g + 1) * n_state].astype(bf16)
        cg = cm[:, g * n_state:(g + 1) * n_state].astype(bf16)
        cb = _dot_nt(cg, bg)
        st_g = st_ref[g * gw:(g + 1) * gw, :]
        y_inter = _dot_nt(cg, st_g.astype(bf16)) * eacs_full[:, g * gw:(g + 1) * gw]
        pair_parts = []
        for q in range(hpg // 2):
            h0 = g * hpg + 2 * q
            xdp = xd[:, h0 * SSM_HEAD_DIM:(h0 + 2) * SSM_HEAD_DIM].astype(bf16)
            res = []
            for h in (h0, h0 + 1):
                seg = acs[:, h:h + 1] - acs_t[h:h + 1, :]
                lm = jnp.exp(jnp.where(causal, seg, MASKED))
                res.append(_dot((cb * lm).astype(bf16), xdp))
            pair_parts.append(jnp.where(lane < SSM_HEAD_DIM, res[0], res[1]))
        y_parts.append(jnp.concatenate(pair_parts, axis=1) + y_inter)
        upd = _dot_tn(xdd[:, g * gw:(g + 1) * gw].astype(bf16), bg)
        for hl in range(hpg):
            h = g * hpg + hl
            rs = slice(g * gw + hl * SSM_HEAD_DIM, g * gw + (hl + 1) * SSM_HEAD_DIM)
            st_ref[rs, :] = (st_g[hl * SSM_HEAD_DIM:(hl + 1) * SSM_HEAD_DIM] * ea_t[h:h + 1, last:last + 1]
                             + upd[hl * SSM_HEAD_DIM:(hl + 1) * SSM_HEAD_DIM])
    y = jnp.concatenate(y_parts, axis=1) + dskip * xs
    y = y * _silu(z.astype(f32))
    normed = []
    for g in range(n_groups):
        yg = y[:, g * gw:(g + 1) * gw]
        normed.append(yg * lax.rsqrt(jnp.mean(yg * yg, axis=-1, keepdims=True) + EPS))
    return jnp.concatenate(normed, axis=1) * wnorm


def _ssd_prompt_body(xbc_ref, dt_ref, dtt_ref, z_ref, wc_ref, bc_ref, dtb_ref, dtbt_ref, a_ref, at_ref,
                     ex_ref, dk_ref, wn_ref, y_ref, st_ref, buf_ref, *, conv_w):
    c = pl.program_id(1)
    l = xbc_ref.shape[0]

    @pl.when(c == 0)
    def _():
        buf_ref[0:SUBLANES, :] = jnp.zeros((SUBLANES, buf_ref.shape[1]), f32)
        st_ref[...] = jnp.zeros(st_ref.shape, f32)

    buf_ref[SUBLANES:SUBLANES + l, :] = xbc_ref[...]
    conv = bc_ref[...]
    for k in range(conv_w):
        off = SUBLANES - (conv_w - 1) + k
        conv = conv + wc_ref[k:k + 1, :] * buf_ref[off:off + l, :]
    buf_ref[0:SUBLANES, :] = buf_ref[l:l + SUBLANES, :]
    y = _ssd_chunk(conv, dt_ref[...], dtt_ref[...], z_ref[...], st_ref, dtb_ref[...], dtbt_ref[...],
                   a_ref[...], at_ref[...], ex_ref[...], dk_ref[...], wn_ref[...], n_valid=None)
    y_ref[...] = y.astype(y_ref.dtype)


def _ssd_prompt(xbc, dt, dtt, z, ssm_w, *, bn, seq_len, n_state):
    r, conv_dim = xbc.shape
    d_inner = z.shape[1]
    l = math.gcd(seq_len, SSM_CHUNK)
    assert l % LANES == 0
    nc = seq_len // l
    wc, bc, dtb, dtbt, a_row, a_col, expand, dskip, wnorm = ssm_w
    conv_w = wc.shape[0]
    assert conv_w - 1 <= SUBLANES
    body = functools.partial(_ssd_prompt_body, conv_w=conv_w)
    consts = [wc, bc, dtb, dtbt, a_row, a_col, expand, dskip, wnorm]
    return pl.pallas_call(
        body,
        grid=(bn, nc),
        in_specs=[pl.BlockSpec((l, conv_dim), lambda b, c: (b * nc + c, 0)),
                  pl.BlockSpec((l, LANES), lambda b, c: (b * nc + c, 0)),
                  pl.BlockSpec((LANES, l), lambda b, c: (0, b * nc + c)),
                  pl.BlockSpec((l, d_inner), lambda b, c: (b * nc + c, 0))]
                 + [_const_spec(w.shape) for w in consts],
        out_specs=[pl.BlockSpec((l, d_inner), lambda b, c: (b * nc + c, 0)),
                   pl.BlockSpec((None, d_inner, n_state), lambda b, c: (b, 0, 0))],
        out_shape=[jax.ShapeDtypeStruct((r, d_inner), bf16),
                   jax.ShapeDtypeStruct((bn, d_inner, n_state), f32)],
        scratch_shapes=[pltpu.VMEM((l + SUBLANES, conv_dim), f32)],
        compiler_params=_params(2),
        name="ssd_prompt",
    )(xbc, dt, dtt, z, *consts)


def _ssd_sample_body(ext_ref, dt_ref, dtt_ref, z_ref, st_in_ref, wc_ref, bc_ref, dtb_ref, dtbt_ref, a_ref,
                     at_ref, ex_ref, dk_ref, wn_ref, y_ref, st_ref, *, conv_w, n_valid):
    lp = dt_ref.shape[0]
    st_ref[...] = st_in_ref[...]
    conv = bc_ref[...]
    for k in range(conv_w):
        conv = conv + wc_ref[k:k + 1, :] * ext_ref[k:k + lp, :]
    y = _ssd_chunk(conv, dt_ref[...], dtt_ref[...], z_ref[...], st_ref, dtb_ref[...], dtbt_ref[...],
                   a_ref[...], at_ref[...], ex_ref[...], dk_ref[...], wn_ref[...], n_valid=n_valid)
    y_ref[...] = y.astype(y_ref.dtype)


def _ssd_sample(ext, dt, dtt, z, state, ssm_w, *, n_valid):
    bs, ext_rows, conv_dim = ext.shape
    lp = dt.shape[1]
    d_inner = z.shape[2]
    n_state = state.shape[2]
    wc, bc, dtb, dtbt, a_row, a_col, expand, dskip, wnorm = ssm_w
    body = functools.partial(_ssd_sample_body, conv_w=wc.shape[0], n_valid=n_valid)
    consts = [wc, bc, dtb, dtbt, a_row, a_col, expand, dskip, wnorm]
    return pl.pallas_call(
        body,
        grid=(bs,),
        in_specs=[pl.BlockSpec((None, ext_rows, conv_dim), lambda s: (s, 0, 0)),
                  pl.BlockSpec((None, lp, LANES), lambda s: (s, 0, 0)),
                  pl.BlockSpec((None, LANES, lp), lambda s: (s, 0, 0)),
                  pl.BlockSpec((None, lp, d_inner), lambda s: (s, 0, 0)),
                  pl.BlockSpec((None, d_inner, n_state), lambda s: (s, 0, 0))]
                 + [_const_spec(w.shape) for w in consts],
        out_specs=[pl.BlockSpec((None, lp, d_inner), lambda s: (s, 0, 0)),
                   pl.BlockSpec((None, d_inner, n_state), lambda s: (s, 0, 0))],
        out_shape=[jax.ShapeDtypeStruct((bs, lp, d_inner), bf16),
                   jax.ShapeDtypeStruct((bs, d_inner, n_state), f32)],
        compiler_params=_params(1),
        name="ssd_sample",
    )(ext, dt, dtt, z, state, *consts)


def _sattn_body(pt_ref, q_ref, kn_ref, vn_ref, bprev_ref, bcur_ref, *rest, nbs, n_full, n_tok, ppb):
    n_pages = nbs * ppb
    kp = rest[:n_pages]
    vp = rest[n_pages:2 * n_pages]
    o_ref = rest[2 * n_pages]
    km_ref, ms_ref, ls_ref, os_ref = rest[2 * n_pages + 1:]
    g = pl.program_id(1)
    n_rows = N_HEADS * SUBLANES
    d = ATTN_DIM

    q4 = q_ref[...]
    q8 = jnp.concatenate([q4, jnp.zeros((SUBLANES - n_tok, d), f32)], axis=0)
    r_i = lax.broadcasted_iota(jnp.int32, (n_rows, d), 0)
    c_i = lax.broadcasted_iota(jnp.int32, (n_rows, d), 1)
    head_mask = (r_i // SUBLANES) == (c_i // HEAD_DIM)
    q_rows = jnp.where(head_mask, jnp.concatenate([q8] * N_HEADS, axis=0), 0.0)
    qb = q_rows.astype(bf16)

    @pl.when(g == 0)
    def _():
        km_ref[...] = jnp.zeros(km_ref.shape, f32)
        ms_ref[...] = jnp.zeros(ms_ref.shape, f32)
        ls_ref[...] = jnp.zeros(ls_ref.shape, f32)

    lane_blk = lax.broadcasted_iota(jnp.int32, (1, LANES), 1)
    sc_new = jnp.zeros((n_rows, LANES), f32)
    ms_new = jnp.zeros((n_rows, LANES), f32)
    ls_new = jnp.zeros((n_rows, LANES), f32)
    o_new = []
    k_t = jnp.concatenate([kp[t][...].reshape(d, -1).astype(bf16) for t in range(n_pages)], axis=1)
    s_all = _dot(qb, k_t)
    for b in range(nbs):
        vblk_t = jnp.concatenate([vp[b * ppb + t][...].reshape(d, -1) for t in range(ppb)], axis=1)
        jj = g * nbs + b
        s = s_all[:, b * MOBA_BLOCK:(b + 1) * MOBA_BLOCK]
        here = (lane_blk == jj).astype(f32)
        sc_new = sc_new + (jnp.sum(s, axis=-1, keepdims=True) * (1.0 / MOBA_BLOCK)) * here
        s = s + jnp.where(jj == n_full - 1, 1.0, 0.0).astype(f32) * bprev_ref[...]
        m = jnp.max(s, axis=-1, keepdims=True)
        p = jnp.exp(s - m)
        l = jnp.sum(p, axis=-1, keepdims=True)
        ms_new = ms_new + m * here
        ls_new = ls_new + l * here
        o_new.append(_dot_nt(p.astype(bf16), vblk_t.astype(bf16)))
    km_ref[...] += sc_new
    ms_ref[...] += ms_new
    ls_ref[...] += ls_new
    os_ref[pl.ds(pl.multiple_of(g * nbs, nbs), nbs)] = jnp.stack(o_new, axis=0)

    @pl.when(g == pl.num_programs(1) - 1)
    def _():
        sc = km_ref[...]
        col = lax.broadcasted_iota(jnp.int32, sc.shape, 1)
        low = jnp.float32(-3e38)
        sc = jnp.where(col < n_full, sc, low)
        self_ = jnp.zeros(sc.shape, f32)
        for _ in range(min(MOBA_TOPK, n_full)):
            mx = jnp.max(sc, axis=-1, keepdims=True)
            idx = jnp.min(jnp.where(sc == mx, col, LANES), axis=-1, keepdims=True)
            self_ = jnp.where(col == idx, 1.0, self_)
            sc = jnp.where(col == idx, low, sc)
        sel = self_ > 0.5
        kn = jnp.concatenate([kn_ref[...], jnp.zeros((SAMPLE_ROWS - n_tok, d), f32)], axis=0).astype(bf16)
        vn = jnp.concatenate([vn_ref[...], jnp.zeros((SAMPLE_ROWS - n_tok, d), f32)], axis=0).astype(bf16)
        s_cur = _dot_nt(qb, kn) + bcur_ref[...]
        ms = ms_ref[...]
        m_tot = jnp.maximum(jnp.max(s_cur, axis=-1, keepdims=True),
                            jnp.max(jnp.where(sel, ms, low), axis=-1, keepdims=True))
        p_cur = jnp.exp(s_cur - m_tot)
        w = jnp.where(sel, jnp.exp(ms - m_tot), 0.0)
        l_tot = jnp.sum(p_cur, axis=-1, keepdims=True) + jnp.sum(w * ls_ref[...], axis=-1, keepdims=True)
        acc = _dot(p_cur.astype(bf16), vn)
        for j in range(n_full):
            acc = acc + w[:, j:j + 1] * os_ref[j]
        out = jnp.where(head_mask, acc / l_tot, 0.0)
        out8 = out[0:SUBLANES]
        for h in range(1, N_HEADS):
            out8 = out8 + out[h * SUBLANES:(h + 1) * SUBLANES]
        o_ref[...] = out8[0:n_tok]


def _attn_sample(q, kn, vn, ck, cv, page_table, bprev, bcur):
    bs, n_tok, d = q.shape
    n_pool, n_heads, hd, page = ck.shape
    assert n_heads * hd == d
    ppb = MOBA_BLOCK // page
    n_pages_seq = page_table.shape[1]
    past = n_pages_seq * page
    assert MOBA_BLOCK % page == 0 and past % MOBA_BLOCK == 0 and n_tok <= SUBLANES
    n_full = past // MOBA_BLOCK
    assert n_full <= LANES
    nbs = math.gcd(n_full, SAMPLE_BLOCKS_PER_STEP)
    n_steps = n_full // nbs
    n_pages = nbs * ppb
    n_rows = N_HEADS * SUBLANES

    def page_spec(t):
        return pl.BlockSpec((None, n_heads, hd, page), lambda s, g, pt: (pt[s, g * n_pages + t], 0, 0, 0))

    seq_spec = pl.BlockSpec((None, n_tok, d), lambda s, g, pt: (s, 0, 0))
    body = functools.partial(_sattn_body, nbs=nbs, n_full=n_full, n_tok=n_tok, ppb=ppb)
    grid_spec = pltpu.PrefetchScalarGridSpec(
        num_scalar_prefetch=1,
        grid=(bs, n_steps),
        in_specs=[seq_spec, seq_spec, seq_spec,
                  pl.BlockSpec(bprev.shape, lambda s, g, pt: (0, 0)),
                  pl.BlockSpec(bcur.shape, lambda s, g, pt: (0, 0))]
                 + [page_spec(t) for t in range(n_pages)] * 2,
        out_specs=seq_spec,
        scratch_shapes=[pltpu.VMEM((n_rows, LANES), f32),
                        pltpu.VMEM((n_rows, LANES), f32),
                        pltpu.VMEM((n_rows, LANES), f32),
                        pltpu.VMEM((n_full, n_rows, d), f32)],
    )
    return pl.pallas_call(
        body,
        grid_spec=grid_spec,
        out_shape=jax.ShapeDtypeStruct((bs, n_tok, d), f32),
        compiler_params=_params(2),
        name="moba_sample",
    )(page_table, q, kn, vn, bprev, bcur, *([ck] * n_pages), *([cv] * n_pages))


def _merge_body(at_ref, yn_ref, ga_ref, gb_ref, x_ref, g1_ref, wpa_ref, wps_ref, wo_ref, lg_ref, lb_ref,
                o_ref, *, alpha):
    pa = _dot(at_ref[...].astype(bf16), wpa_ref[...])
    ps = _dot(yn_ref[...].astype(bf16), wps_ref[...])
    merged = jax.nn.sigmoid(ga_ref[...].astype(f32)) * pa + jax.nn.sigmoid(gb_ref[...].astype(f32)) * ps
    mo = _dot(merged.astype(bf16), wo_ref[...])
    o_ref[...] = _layer_norm(alpha * x_ref[...] + g1_ref[...] * mo, lg_ref[...], lb_ref[...])


def _mod_spec(per_row_mod, tm, d, tiles_per_seq):
    if per_row_mod:
        return pl.BlockSpec((tm, d), lambda i: (i, 0))
    return pl.BlockSpec((None, 1, d), lambda i: (i // tiles_per_seq, 0, 0))


def _merge(attn, yn, ga, gb, x2d, gate1, wpa, wps, wo, lg, lb, *, per_row_mod, seq_len, alpha):
    r, d = x2d.shape
    tm = 256
    assert r % tm == 0
    tiles_per_seq = 1 if per_row_mod else seq_len // tm
    ms = _mod_spec(per_row_mod, tm, d, tiles_per_seq)

    def row_spec(w):
        return pl.BlockSpec((tm, w), lambda i: (i, 0))

    return pl.pallas_call(
        functools.partial(_merge_body, alpha=alpha),
        grid=(r // tm,),
        in_specs=[row_spec(attn.shape[1]), row_spec(yn.shape[1]), row_spec(d), row_spec(d), row_spec(d), ms,
                  _const_spec(wpa.shape), _const_spec(wps.shape), _const_spec(wo.shape),
                  _const_spec(lg.shape), _const_spec(lb.shape)],
        out_specs=row_spec(d),
        out_shape=jax.ShapeDtypeStruct((r, d), f32),
        compiler_params=_params(1),
        name="merge_ln1",
    )(attn, yn, ga, gb, x2d, gate1, wpa, wps, wo, lg, lb)


FFN_COL_CHUNKS = 2


def _ffn_body(x_ref, sh_ref, sc_ref, g2_ref, wu_ref, wc_ref, bc_ref, wd_ref, lg_ref, lb_ref, *rest,
              alpha, conv_w, tiles_per_seq, sample_len):
    if sample_len is None:
        y_ref, tail_ref, buf_ref, carry_ref = rest
    else:
        p_refs = rest[:conv_w - 1]
        y_ref, hup_ref, buf_ref, carry_ref = rest[conv_w - 1:]
    x = x_ref[...]
    tm = x.shape[0]
    ff = wd_ref.shape[0]
    cw = ff // FFN_COL_CHUNKS
    t = pl.program_id(0) % tiles_per_seq

    @pl.when(t == 0)
    def _():
        carry_ref[...] = jnp.zeros(carry_ref.shape, f32)

    u = (x * (1.0 + sc_ref[...]) + sh_ref[...]).astype(bf16)
    if sample_len is not None:
        tmod = lax.broadcasted_iota(jnp.int32, (tm, cw), 0) % sample_len
    f = jnp.zeros((tm, x.shape[1]), f32)
    for c in range(FFN_COL_CHUNKS):
        halves = []
        for half in range(2):
            c0 = half * ff + c * cw
            buf_ref[0:SUBLANES, :] = carry_ref[:, c0:c0 + cw]
            hup = _dot(u, wu_ref[:, c0:c0 + cw])
            buf_ref[SUBLANES:SUBLANES + tm, :] = hup
            if sample_len is not None:
                hup_ref[:, c0:c0 + cw] = hup
            hc = bc_ref[:, c0:c0 + cw] + wc_ref[conv_w - 1:conv_w, c0:c0 + cw] * hup
            for k in range(conv_w - 1):
                back = conv_w - 1 - k
                prev = buf_ref[SUBLANES - back:SUBLANES - back + tm, :]
                if sample_len is not None:
                    prev = jnp.where(tmod >= back, prev, p_refs[back - 1][:, c0:c0 + cw])
                hc = hc + wc_ref[k:k + 1, c0:c0 + cw] * prev
            carry_ref[:, c0:c0 + cw] = buf_ref[tm:tm + SUBLANES, :]
            halves.append(hc)
        gact = (_silu(halves[0]) * halves[1]).astype(bf16)
        f = f + _dot(gact, wd_ref[c * cw:(c + 1) * cw, :])
    y_ref[...] = _layer_norm(alpha * x + g2_ref[...] * f, lg_ref[...], lb_ref[...])
    if sample_len is None:
        @pl.when(t == tiles_per_seq - 1)
        def _():
            tail_ref[...] = carry_ref[...]


def _ffn(x2d, shift, scale, gate, wu, wc, bc, wd, lg, lb, prevs, *, per_row_mod, seq_len, bn, alpha,
         sample_len):
    r, d = x2d.shape
    ff2 = wu.shape[1]
    ff = wd.shape[0]
    conv_w = wc.shape[0]
    tm = 256 if sample_len is None else 128
    assert r % tm == 0 and ff % (FFN_COL_CHUNKS * LANES) == 0 and conv_w - 1 <= SUBLANES
    tiles_per_seq = 1 if per_row_mod else seq_len // tm
    ms = _mod_spec(per_row_mod, tm, d, tiles_per_seq)

    def row_spec(w):
        return pl.BlockSpec((tm, w), lambda i: (i, 0))

    in_specs = [row_spec(d), ms, ms, ms, _const_spec(wu.shape), _const_spec(wc.shape), _const_spec(bc.shape),
                _const_spec(wd.shape), _const_spec(lg.shape), _const_spec(lb.shape)]
    args = [x2d, shift, scale, gate, wu, wc, bc, wd, lg, lb]
    if sample_len is None:
        out_shape = [jax.ShapeDtypeStruct((r, d), f32), jax.ShapeDtypeStruct((bn, SUBLANES, ff2), f32)]
        out_specs = [row_spec(d), pl.BlockSpec((None, SUBLANES, ff2), lambda i: (i // tiles_per_seq, 0, 0))]
    else:
        assert tm % sample_len == 0 and len(prevs) == conv_w - 1
        in_specs += [row_spec(ff2)] * len(prevs)
        args += list(prevs)
        out_shape = [jax.ShapeDtypeStruct((r, d), f32), jax.ShapeDtypeStruct((r, ff2), f32)]
        out_specs = [row_spec(d), row_spec(ff2)]
    body = functools.partial(_ffn_body, alpha=alpha, conv_w=conv_w, tiles_per_seq=tiles_per_seq,
                             sample_len=sample_len)
    return pl.pallas_call(
        body,
        grid=(r // tm,),
        in_specs=in_specs,
        out_specs=out_specs,
        out_shape=out_shape,
        scratch_shapes=[pltpu.VMEM((tm + SUBLANES, ff // FFN_COL_CHUNKS), f32),
                        pltpu.VMEM((SUBLANES, ff2), f32)],
        compiler_params=_params(1),
        name="conv_ffn",
    )(*args)


def _rel_bucket(dist):
    n = jnp.maximum(dist, 0)
    max_exact = REL_BUCKETS // 2
    nf = jnp.maximum(n, 1).astype(f32)
    large = max_exact + (jnp.log(nf / max_exact) / math.log(REL_MAX_DIST / max_exact)
                         * (REL_BUCKETS - max_exact)).astype(jnp.int32)
    large = jnp.minimum(large, REL_BUCKETS - 1)
    return jnp.where(n < max_exact, n, large)


def _rel_bias(rel_table, dist):
    rel = rel_table - rel_table[REL_BUCKETS - 1]
    onehot = (_rel_bucket(dist)[..., None] == jnp.arange(REL_BUCKETS)).astype(f32)
    b = jnp.dot(onehot, rel, precision=lax.Precision.HIGHEST)
    return jnp.moveaxis(b, -1, 0)


def _prompt_bias_tables(rel_table):
    a = jnp.arange(MOBA_BLOCK, dtype=jnp.int32)
    dist_d = a[None, :] - a[:, None]
    bd = jnp.where(dist_d >= 0, _rel_bias(rel_table, dist_d) * LOG2E, MASKED)
    bp = _rel_bias(rel_table, dist_d + MOBA_BLOCK) * LOG2E
    return bd.astype(f32), bp.astype(f32)


def _sample_bias_tables(rel_table, n_tok):
    n_rows = N_HEADS * SUBLANES
    t = jnp.arange(SUBLANES, dtype=jnp.int32)
    a = jnp.arange(MOBA_BLOCK, dtype=jnp.int32)
    bprev = _rel_bias(rel_table, MOBA_BLOCK + t[:, None] - a[None, :])
    bprev = jnp.where((t < n_tok)[None, :, None], bprev, 0.0).reshape(n_rows, MOBA_BLOCK)
    tk = jnp.arange(SAMPLE_ROWS, dtype=jnp.int32)
    dist = t[:, None] - tk[None, :]
    ok = (dist >= 0) & (t[:, None] < n_tok) & (tk[None, :] < n_tok)
    bcur = jnp.where(ok[None], _rel_bias(rel_table, dist), MASKED).reshape(n_rows, SAMPLE_ROWS)
    return bprev.astype(f32), bcur.astype(f32)


def _pad_cols(w, n):
    return jnp.pad(w, ((0, 0), (0, n - w.shape[1])))


def kernel(x_prompt, x_sample, cache_k, cache_v, page_table, state_ssm, state_conv_ssm, state_conv_ffn,
           c_prompt, c_sample, rel_table, w_ada, b_ada, w_in, w_conv_ssm, b_conv_ssm, dt_bias, a_log,
           d_skip, w_norm_ssm, w_proj_attn, w_proj_ssm, w_out, ln1_g, ln1_b, w_up, w_conv_ffn, b_conv_ffn,
           w_down, ln2_g, ln2_b):
    depth = w_ada.shape[0]
    alpha = (2 * depth) ** 0.25
    bp_, seq, d = x_prompt.shape
    bs, n_tok, _ = x_sample.shape
    ssm_heads = dt_bias.shape[1]
    d_inner = ssm_heads * SSM_HEAD_DIM
    n_state = state_ssm.shape[-1]
    conv_dim = w_conv_ssm.shape[-1]
    ff2 = w_up.shape[-1]
    ssm_conv = w_conv_ssm.shape[1]
    ffn_conv = w_conv_ffn.shape[1]
    assert ssm_heads <= LANES and n_tok <= SUBLANES
    page = cache_k.shape[2]
    scale = HEAD_DIM ** -0.5

    bd, bpv = _prompt_bias_tables(rel_table)
    bprev_s, bcur_s = _sample_bias_tables(rel_table, n_tok)
    expand = (jnp.arange(LANES)[:, None] == (jnp.arange(d_inner)[None, :] // SSM_HEAD_DIM)).astype(bf16)

    yp = x_prompt.reshape(bp_ * seq, d)
    ys = x_sample.reshape(bs * n_tok, d)
    outs_p = [[] for _ in range(5)]
    outs_s = [[] for _ in range(5)]
    for l in range(depth):
        cuts = [ATTN_DIM, 2 * ATTN_DIM, 3 * ATTN_DIM, 3 * ATTN_DIM + d_inner,
                3 * ATTN_DIM + d_inner + conv_dim, 3 * ATTN_DIM + d_inner + conv_dim + ssm_heads,
                3 * ATTN_DIM + d_inner + conv_dim + ssm_heads + d]
        wq, wk, wv, wz, wxbc, wdt, wga, wgb = jnp.split(w_in[l], cuts, axis=1)
        wdt = _pad_cols(wdt, LANES)
        wq = wq * scale
        nat_p = [(ATTN_DIM, f32), (ATTN_DIM, f32), (d_inner, bf16), (conv_dim, f32), (LANES, f32),
                 (d, bf16), (d, bf16)]
        wn_p = jnp.concatenate([wk, wv, wz, wxbc, wdt, wga, wgb], axis=1).astype(bf16)
        tr_p = [(ATTN_DIM, bf16, 0), (ATTN_DIM, bf16, BF16_ROWS), (LANES, f32, 0)]
        wt_p = jnp.concatenate([wq * LOG2E, wv, wdt], axis=1).T.astype(bf16)
        nat_s = [(ATTN_DIM, f32)] + nat_p
        wn_s = jnp.concatenate([wq.astype(bf16), wn_p], axis=1)
        tr_s = [(LANES, f32, 0)]
        wt_s = wdt.T.astype(bf16)
        ssm_w = (w_conv_ssm[l], b_conv_ssm[l].reshape(1, conv_dim),
                 _pad_cols(dt_bias[l].reshape(1, -1), LANES), _pad_cols(dt_bias[l].reshape(1, -1), LANES).T,
                 -jnp.exp(_pad_cols(a_log[l].reshape(1, -1), LANES)),
                 -jnp.exp(_pad_cols(a_log[l].reshape(1, -1), LANES)).T,
                 expand, jnp.repeat(d_skip[l], SSM_HEAD_DIM).reshape(1, d_inner),
                 w_norm_ssm[l].reshape(1, d_inner))
        wpa, wps, wo = (w_proj_attn[l].astype(bf16), w_proj_ssm[l].astype(bf16), w_out[l].astype(bf16))
        lg1, lb1 = ln1_g[l].reshape(1, d), ln1_b[l].reshape(1, d)
        lg2, lb2 = ln2_g[l].reshape(1, d), ln2_b[l].reshape(1, d)
        wu, wd = w_up[l].astype(bf16), w_down[l].astype(bf16)
        wcf, bcf = w_conv_ffn[l], b_conv_ffn[l].reshape(1, ff2)

        mod = _ada(jnp.concatenate([c_prompt, c_sample], axis=0), w_ada[l], b_ada[l])
        mod_p = mod[:bp_].reshape(bp_, 6, 1, d)
        mod_s = jnp.repeat(mod[bp_:].reshape(bs, 6, 1, d), n_tok, axis=2).reshape(bs, 6, n_tok, d)
        mod_s = jnp.moveaxis(mod_s, 1, 0).reshape(6, bs * n_tok, d)
        sh1p, sc1p, g1p, sh2p, sc2p, g2p = (mod_p[:, i] for i in range(6))
        sh1s, sc1s, g1s, sh2s, sc2s, g2s = (mod_s[i] for i in range(6))

        k_p, v_p, z_p, xbc_p, dt_p, ga_p, gb_p, qT_p, vT_p, dtT_p, k2_p = _inproj(
            yp, sh1p, sc1p, wn_p, wt_p, nat_p, tr_p, per_row_mod=False, seq_len=seq, attn_layout=True)
        attn_p = _attn_prompt(k2_p, qT_p, vT_p, bd, bpv, bn=bp_, seq_len=seq)
        yn_p, st_p = _ssd_prompt(xbc_p, dt_p, dtT_p, z_p, ssm_w, bn=bp_, seq_len=seq, n_state=n_state)
        x1_p = _merge(attn_p, yn_p, ga_p, gb_p, yp, g1p, wpa, wps, wo, lg1, lb1,
                      per_row_mod=False, seq_len=seq, alpha=alpha)
        yp, tail_p = _ffn(x1_p, sh2p, sc2p, g2p, wu, wcf, bcf, wd, lg2, lb2, (),
                          per_row_mod=False, seq_len=seq, bn=bp_, alpha=alpha, sample_len=None)
        outs_p[0].append(k_p.reshape(bp_, seq, N_HEADS, HEAD_DIM))
        outs_p[1].append(v_p.reshape(bp_, seq, N_HEADS, HEAD_DIM))
        outs_p[2].append(st_p.reshape(bp_, ssm_heads, SSM_HEAD_DIM, n_state))
        outs_p[3].append(xbc_p.reshape(bp_, seq, conv_dim)[:, seq - (ssm_conv - 1):])
        outs_p[4].append(tail_p[:, SUBLANES - (ffn_conv - 1):])

        r_s = bs * n_tok
        r_pad = -(-r_s // 256) * 256

        def pad_rows(a):
            return jnp.pad(a, ((0, r_pad - r_s), (0, 0)))

        q_s, k_s, v_s, z_s, xbc_s, dt_s, ga_s, gb_s, dtT_s = _inproj(
            pad_rows(ys), pad_rows(sh1s), pad_rows(sc1s), wn_s, wt_s, nat_s, tr_s,
            per_row_mod=True, seq_len=None, attn_layout=False)
        q_s, k_s, v_s = (a[:r_s].reshape(bs, n_tok, ATTN_DIM) for a in (q_s, k_s, v_s))
        attn_s = _attn_sample(q_s, k_s, v_s, jnp.transpose(cache_k[l], (0, 2, 3, 1)),
                              jnp.transpose(cache_v[l], (0, 2, 3, 1)), page_table, bprev_s, bcur_s)
        row_pad = SAMPLE_ROWS - n_tok
        xbc_s3 = xbc_s[:r_s].reshape(bs, n_tok, conv_dim)
        ext = jnp.concatenate([state_conv_ssm[l], xbc_s3,
                               jnp.zeros((bs, row_pad + SUBLANES - (ssm_conv - 1), conv_dim), f32)], axis=1)
        dt_s3 = jnp.pad(dt_s[:r_s].reshape(bs, n_tok, LANES), ((0, 0), (0, row_pad), (0, 0)))
        dtT_s3 = jnp.pad(jnp.moveaxis(dtT_s[:, :r_s].reshape(LANES, bs, n_tok), 0, 1),
                         ((0, 0), (0, 0), (0, row_pad)))
        z_s3 = jnp.pad(z_s[:r_s].reshape(bs, n_tok, d_inner), ((0, 0), (0, row_pad), (0, 0)))
        yn_s3, st_s = _ssd_sample(ext, dt_s3, dtT_s3, z_s3, state_ssm[l].reshape(bs, d_inner, n_state),
                                  ssm_w, n_valid=n_tok)
        yn_s = yn_s3[:, :n_tok].reshape(r_s, d_inner)
        x1_s = _merge(pad_rows(attn_s.reshape(r_s, ATTN_DIM)), pad_rows(yn_s), ga_s, gb_s, pad_rows(ys),
                      pad_rows(g1s), wpa, wps, wo, lg1, lb1, per_row_mod=True, seq_len=None, alpha=alpha)
        cf = state_conv_ffn[l]
        prevs = []
        for back in range(1, ffn_conv):
            rows = [cf[:, ffn_conv - 1 - back + t] if t < back else jnp.zeros((bs, ff2), f32)
                    for t in range(n_tok)]
            prevs.append(pad_rows(jnp.stack(rows, axis=1).reshape(r_s, ff2)))
        y_s, hup_s = _ffn(x1_s, pad_rows(sh2s), pad_rows(sc2s), pad_rows(g2s), wu, wcf, bcf, wd, lg2, lb2,
                          prevs, per_row_mod=True, seq_len=None, bn=bs, alpha=alpha, sample_len=n_tok)
        ys = y_s[:r_s]
        outs_s[0].append(k_s.reshape(bs, n_tok, N_HEADS, HEAD_DIM))
        outs_s[1].append(v_s.reshape(bs, n_tok, N_HEADS, HEAD_DIM))
        outs_s[2].append(st_s.reshape(bs, ssm_heads, SSM_HEAD_DIM, n_state))
        cs_ext = jnp.concatenate([state_conv_ssm[l], xbc_s3], axis=1)
        outs_s[3].append(cs_ext[:, -(ssm_conv - 1):])
        cf_ext = jnp.concatenate([cf, hup_s[:r_s].reshape(bs, n_tok, ff2)], axis=1)
        outs_s[4].append(cf_ext[:, -(ffn_conv - 1):])

    return (yp.reshape(bp_, seq, d), ys.reshape(bs, n_tok, d),
            *(jnp.stack(o) for o in outs_p), *(jnp.stack(o) for o in outs_s))
```

```python
import functools
import math

import jax
import jax.numpy as jnp
from jax import lax
from jax.experimental import pallas as pl
from jax.experimental.pallas import tpu as pltpu

N_HEADS = 8
HEAD_DIM = 64
ATTN_DIM = N_HEADS * HEAD_DIM
MOBA_BLOCK = 256
MOBA_TOPK = 3
REL_BUCKETS = 32
REL_MAX_DIST = 128
SSM_HEAD_DIM = 64
SSM_CHUNK = 256
EPS = 1e-5

LANES = 128
SUBLANES = 8
BF16_ROWS = 16
VMEM_LIMIT = 56 * 1024 * 1024

MASKED = -1e30
SAMPLE_ROWS = 16
SAMPLE_BLOCKS_PER_STEP = 4
PAGE_SLOTS = 3
FAR_GROUP = 4
LOG2E = math.log2(math.e)

f32 = jnp.float32
bf16 = jnp.bfloat16

_NT = (((1,), (1,)), ((), ()))
_TN = (((0,), (0,)), ((), ()))


def _dot(a, b):
    return jnp.dot(a, b, preferred_element_type=f32)


def _dot_nt(a, b):
    return lax.dot_general(a, b, _NT, preferred_element_type=f32)


def _dot_tn(a, b):
    return lax.dot_general(a, b, _TN, preferred_element_type=f32)


def _split3(x):
    hi = x.astype(bf16)
    r = x - hi.astype(f32)
    mid = r.astype(bf16)
    lo = (r - mid.astype(f32)).astype(bf16)
    return hi, mid, lo


def _dot_exact_rhs(a, m):
    hi, mid, lo = _split3(a)
    return _dot(hi, m) + _dot(mid, m) + _dot(lo, m)


def _dot_exact_lhs(m, a):
    hi, mid, lo = _split3(a)
    return _dot(m, hi) + _dot(m, mid) + _dot(m, lo)


def _silu(x):
    return x * jax.nn.sigmoid(x)


def _softplus(x):
    return jnp.maximum(x, 0.0) + jnp.log1p(jnp.exp(-jnp.abs(x)))


def _layer_norm(x, g, b):
    mu = jnp.mean(x, axis=-1, keepdims=True)
    xc = x - mu
    var = jnp.mean(xc * xc, axis=-1, keepdims=True)
    return xc * lax.rsqrt(var + EPS) * g + b


def _const_spec(shape):
    nd = len(shape)
    return pl.BlockSpec(shape, lambda *_: (0,) * nd, pipeline_mode=pl.Buffered(1))


def _params(n_grid, flags=None):
    return pltpu.CompilerParams(
        dimension_semantics=("arbitrary",) * n_grid, vmem_limit_bytes=VMEM_LIMIT, flags=flags)


def _ada_body(c_ref, w_ref, b_ref, o_ref):
    a = _silu(c_ref[...])
    hi, mid, lo = _split3(a)
    whi, wmid, wlo = _split3(w_ref[...])
    acc = _dot(hi, whi) + (_dot(hi, wmid) + _dot(mid, whi))
    acc = acc + (_dot(hi, wlo) + _dot(mid, wmid) + _dot(lo, whi))
    o_ref[...] = acc + b_ref[...]


def _ada(c, w, b):
    n, d = c.shape
    dn = w.shape[1]
    tn = 1024 if dn % 1024 == 0 else dn
    return pl.pallas_call(
        _ada_body,
        grid=(dn // tn,),
        in_specs=[pl.BlockSpec((n, d), lambda j: (0, 0)),
                  pl.BlockSpec((d, tn), lambda j: (0, j)),
                  pl.BlockSpec((1, tn), lambda j: (0, j))],
        out_specs=pl.BlockSpec((n, tn), lambda j: (0, j)),
        out_shape=jax.ShapeDtypeStruct((n, dn), f32),
        compiler_params=_params(1),
        name="ada_mod",
    )(c, w, b.reshape(1, dn))


def _inproj_body(x_ref, sh_ref, sc_ref, wn_ref, wt_ref, *outs, nat, tr, k_col, tiles_per_seq):
    u = (x_ref[...] * (1.0 + sc_ref[...]) + sh_ref[...]).astype(bf16)
    tm = u.shape[0]
    o = 0
    col = 0
    for width, _ in nat:
        res = _dot(u, wn_ref[:, col:col + width])
        outs[o][...] = res.astype(outs[o].dtype)
        if k_col is not None and col == k_col:
            kf = res
        o += 1
        col += width
    if tr:
        t = _dot_nt(wt_ref[...], u)
        row = 0
        for height, _, ones_rows in tr:
            if ones_rows:
                step = HEAD_DIM + ones_rows
                for h in range(height // HEAD_DIM):
                    outs[o][h * step:h * step + HEAD_DIM, :] = (
                        t[row + h * HEAD_DIM:row + (h + 1) * HEAD_DIM].astype(outs[o].dtype))
                    outs[o][h * step + HEAD_DIM:(h + 1) * step, :] = jnp.ones((ones_rows, tm), outs[o].dtype)
            else:
                outs[o][...] = t[row:row + height].astype(outs[o].dtype)
            o += 1
            row += height
    if k_col is not None:
        pos = (pl.program_id(0) % tiles_per_seq) * tm
        blk = pos // MOBA_BLOCK
        onehot = (lax.broadcasted_iota(jnp.int32, (tm, LANES), 1) == blk).astype(bf16)
        k2_ref = outs[o]
        for p in range(ATTN_DIM // LANES):
            k2_ref[p] = jnp.concatenate([kf[:, p * LANES:(p + 1) * LANES].astype(bf16), onehot], axis=1)


def _inproj(x2d, shift, scale, wn, wt, nat, tr, *, per_row_mod, seq_len, attn_layout):
    r, d = x2d.shape
    tm = 256
    assert r % tm == 0
    n_tiles = r // tm
    if per_row_mod:
        mod_spec = pl.BlockSpec((tm, d), lambda i: (i, 0))
        tiles_per_seq = 1
    else:
        assert seq_len % tm == 0 and MOBA_BLOCK % tm == 0
        tiles_per_seq = seq_len // tm
        mod_spec = pl.BlockSpec((None, 1, d), lambda i: (i // tiles_per_seq, 0, 0))
    out_shape, out_specs = [], []
    for width, dt in nat:
        out_shape.append(jax.ShapeDtypeStruct((r, width), dt))
        out_specs.append(pl.BlockSpec((tm, width), lambda i: (i, 0)))
    for height, dt, ones_rows in tr:
        rows = height + (height // HEAD_DIM) * ones_rows
        out_shape.append(jax.ShapeDtypeStruct((rows, r), dt))
        out_specs.append(pl.BlockSpec((rows, tm), lambda i: (0, i)))
    k_col = None
    if attn_layout:
        k_col = 0
        n_pairs = ATTN_DIM // LANES
        out_shape.append(jax.ShapeDtypeStruct((n_pairs, r, 2 * LANES), bf16))
        out_specs.append(pl.BlockSpec((n_pairs, tm, 2 * LANES), lambda i: (0, i, 0)))
    body = functools.partial(_inproj_body, nat=tuple(nat), tr=tuple(tr), k_col=k_col,
                             tiles_per_seq=tiles_per_seq)
    return pl.pallas_call(
        body,
        grid=(n_tiles,),
        in_specs=[pl.BlockSpec((tm, d), lambda i: (i, 0)), mod_spec, mod_spec,
                  _const_spec(wn.shape), _const_spec(wt.shape)],
        out_specs=out_specs,
        out_shape=out_shape,
        compiler_params=_params(1),
        name="in_proj",
    )(x2d, shift, scale, wn, wt)


def _select_topk_rows(s, n_valid_rows):
    nblk = s.shape[0]
    row = lax.broadcasted_iota(jnp.int32, s.shape, 0)
    low = jnp.float32(-3e38)
    s = jnp.where(row < n_valid_rows, s, low)
    sel = jnp.zeros(s.shape, jnp.bool_)
    for _ in range(MOBA_TOPK):
        m = jnp.max(s, axis=0, keepdims=True)
        idx = jnp.min(jnp.where(s == m, row, nblk), axis=0, keepdims=True)
        hit = (row == idx) & (m > low)
        sel = sel | hit
        s = jnp.where(row == idx, low, s)
    return sel


def _attn_body(k2_ref, qT_ref, vT_ref, bd_ref, bp_ref, o_ref, km_ref, q2n_ref, q2f_ref, sa_ref, sb_ref, *,
               n_blocks, group):
    i = pl.program_id(2)
    blk = MOBA_BLOCK
    hd = HEAD_DIM
    nsel = km_ref.shape[0]

    @pl.when(i == 0)
    def _():
        km_ref[...] = jnp.zeros(km_ref.shape, f32)

        def mean_body(j, c):
            kk = k2_ref[pl.ds(pl.multiple_of(j * blk, blk), blk), :].astype(f32)
            km_ref[pl.ds(j, 1), :] = jnp.sum(kk, axis=0, keepdims=True) * (1.0 / blk)
            return c
        lax.fori_loop(0, n_blocks, mean_body, 0)

    q_pair = qT_ref[:, pl.ds(pl.multiple_of(i * blk, blk), blk)]
    zq = jnp.zeros((hd, blk), bf16)
    q_rows = jnp.concatenate([jnp.concatenate([q_pair[0:hd], zq], axis=1),
                              jnp.concatenate([zq, q_pair[hd:2 * hd]], axis=1)], axis=0)
    zrest = jnp.zeros((q2f_ref.shape[0] - 2 * hd, 2 * blk), bf16)
    q2f_ref[0:2 * hd, :] = q_rows
    q2f_ref[2 * hd:, :] = zrest
    q2n_ref[0:2 * hd, :] = q_rows
    q2n_ref[2 * hd:, :] = zrest
    km = km_ref[...]
    km_hi = km.astype(bf16)
    km_lo = (km - km_hi.astype(f32)).astype(bf16)
    q2 = q2f_ref[...]
    s_blk = _dot(km_hi, q2) + _dot(km_lo, q2)
    sel = _select_topk_rows(s_blk, i)
    rowsel = lax.broadcasted_iota(jnp.int32, (nsel, 2 * blk), 0)
    far = sel & (rowsel < i - 1)
    near = (sel & (rowsel == i - 1)) | (rowsel == i)
    q2f_ref[2 * hd:2 * hd + nsel, :] = jnp.where(far, 0.0, MASKED).astype(bf16)
    q2n_ref[2 * hd:2 * hd + nsel, :] = jnp.where(near, 0.0, MASKED).astype(bf16)

    def k_rows(j, n):
        return k2_ref[pl.ds(pl.multiple_of(j * blk, blk), n * blk), :]

    vrows = vT_ref.shape[0] // 2

    def v_cols(h, j, n):
        return vT_ref[h * vrows:(h + 1) * vrows, pl.ds(pl.multiple_of(j * blk, blk), n * blk)]

    jp = jnp.maximum(i - 1, 0)
    first_pad = jnp.where(i > 0, 0.0, MASKED).astype(f32)
    s_cur = _dot(k_rows(i, 1), q2n_ref[...])
    s_prev = _dot(k_rows(jp, 1), q2n_ref[...])
    carry = []
    for h in range(2):
        s = jnp.concatenate([s_cur[:, h * blk:(h + 1) * blk] + bd_ref[h],
                             s_prev[:, h * blk:(h + 1) * blk] + (bp_ref[h] + first_pad)], axis=0)
        m = jnp.max(s, axis=0, keepdims=True)
        p = jnp.exp2((s - m).astype(bf16))
        v = jnp.concatenate([v_cols(h, i, 1), v_cols(h, jp, 1)], axis=1)
        carry += [m, _dot(v, p)]

    last_group = n_blocks // group - 1

    def logits_into(s_ref, g):
        s_ref[...] = _dot(k_rows(jnp.minimum(g, last_group) * group, group), q2f_ref[...])

    def consume(s_ref, g, carry):
        j0 = jnp.minimum(g, last_group) * group
        pad = jnp.where(g < n_groups, 0.0, MASKED).astype(f32)
        out = []
        for h in range(2):
            m, acc = carry[2 * h:2 * h + 2]
            m2 = jnp.maximum(m, jnp.max(s_ref[:, h * blk:(h + 1) * blk], axis=0, keepdims=True) + pad)
            a = jnp.exp2(m - m2)
            p = jnp.exp2((s_ref[:, h * blk:(h + 1) * blk] - (m2 - pad)).astype(bf16))
            out += [m2, a * acc + _dot(v_cols(h, j0, group), p)]
        return out

    def far_body(t, carry):
        logits_into(sb_ref, 2 * t + 1)
        carry = consume(sa_ref, 2 * t, list(carry))
        logits_into(sa_ref, 2 * t + 2)
        return tuple(consume(sb_ref, 2 * t + 1, carry))

    n_groups = (jnp.maximum(i - 1, 0) + group - 1) // group
    logits_into(sa_ref, 0)
    carry = lax.fori_loop(0, (n_groups + 1) // 2, far_body, tuple(carry))
    outs = [carry[2 * h + 1][0:hd] / carry[2 * h + 1][hd:hd + 1] for h in range(2)]
    o_ref[...] = jnp.concatenate(outs, axis=0).T.astype(o_ref.dtype)


def _attn_prompt(k2, qT, vT, bd, bp, *, bn, seq_len):
    n_pairs, r, _ = k2.shape
    blk = MOBA_BLOCK
    assert seq_len % blk == 0
    nb = seq_len // blk
    nsel = HEAD_DIM
    assert nb <= nsel
    group = math.gcd(nb, FAR_GROUP)
    body = functools.partial(_attn_body, n_blocks=nb, group=group)
    return pl.pallas_call(
        body,
        grid=(bn, n_pairs, nb),
        in_specs=[pl.BlockSpec((None, seq_len, 2 * LANES), lambda b, p, i: (p, b, 0)),
                  pl.BlockSpec((LANES, seq_len), lambda b, p, i: (p, b)),
                  pl.BlockSpec((vT.shape[0] // n_pairs, seq_len), lambda b, p, i: (p, b)),
                  pl.BlockSpec((2, blk, blk), lambda b, p, i: (p, 0, 0)),
                  pl.BlockSpec((2, blk, blk), lambda b, p, i: (p, 0, 0))],
        out_specs=pl.BlockSpec((blk, LANES), lambda b, p, i: (b * nb + i, p)),
        out_shape=jax.ShapeDtypeStruct((r, ATTN_DIM), bf16),
        scratch_shapes=[pltpu.VMEM((nsel, 2 * LANES), f32),
                        pltpu.VMEM((2 * LANES, 2 * blk), bf16),
                        pltpu.VMEM((2 * LANES, 2 * blk), bf16),
                        pltpu.VMEM((group * blk, 2 * blk), f32),
                        pltpu.VMEM((group * blk, 2 * blk), f32)],
        compiler_params=_params(3),
        name="moba_prompt",
    )(k2, qT, vT, bd, bp)


def _ssd_chunk(conv, dt_raw, dt_raw_t, z, st_ref, dtb, dtb_t, a_row, a_col, expand, dskip, wnorm,
               *, n_valid):
    l = conv.shape[0]
    d_inner = z.shape[1]
    n_state = st_ref.shape[1]
    n_heads = d_inner // SSM_HEAD_DIM
    n_groups = (conv.shape[1] - d_inner) // (2 * n_state)
    hpg = n_heads // n_groups
    gw = hpg * SSM_HEAD_DIM
    last = (l if n_valid is None else n_valid) - 1

    act = _silu(conv)
    xs = act[:, :d_inner]
    bm = act[:, d_inner:d_inner + n_groups * n_state]
    cm = act[:, d_inner + n_groups * n_state:]

    dt = _softplus(dt_raw + dtb)
    dt_t = _softplus(dt_raw_t + dtb_t)
    r_i = lax.broadcasted_iota(jnp.int32, (l, l), 0)
    c_i = lax.broadcasted_iota(jnp.int32, (l, l), 1)
    causal = r_i >= c_i
    tril = causal.astype(bf16)
    triu = (r_i <= c_i).astype(bf16)
    acs = _dot_exact_lhs(tril, dt * a_row)
    acs_t = _dot_exact_rhs(dt_t * a_col, triu)
    eacs = jnp.exp(acs)
    dec = jnp.exp(acs[last:last + 1, :] - acs)
    ea_t = jnp.exp(acs_t)

    dt_full = _dot_exact_rhs(dt, expand)
    eacs_full = _dot_exact_rhs(eacs, expand)
    dec_full = _dot_exact_rhs(dec, expand)
    xd = xs * dt_full
    xdd = xd * dec_full
    if n_valid is not None:
        rows = lax.broadcasted_iota(jnp.int32, xdd.shape, 0)
        xdd = jnp.where(rows < n_valid, xdd, 0.0)

    lane = lax.broadcasted_iota(jnp.int32, (l, 2 * SSM_HEAD_DIM), 1)
    y_parts = []
    for g in range(n_groups):
        bg = bm[:, g * n_state:(g + 1) * n_state].astype(bf16)
        cg = cm[:, g * n_state:(g + 1) * n_state].astype(bf16)
        cb = _dot_nt(cg, bg)
        st_g = st_ref[g * gw:(g + 1) * gw, :]
        y_inter = _dot_nt(cg, st_g.astype(bf16)) * eacs_full[:, g * gw:(g + 1) * gw]
        pair_parts = []
        for q in range(hpg // 2):
            h0 = g * hpg + 2 * q
            xdp = xd[:, h0 * SSM_HEAD_DIM:(h0 + 2) * SSM_HEAD_DIM].astype(bf16)
            res = []
            for h in (h0, h0 + 1):
                seg = acs[:, h:h + 1] - acs_t[h:h + 1, :]
                lm = jnp.exp(jnp.where(causal, seg, MASKED))
                res.append(_dot((cb * lm).astype(bf16), xdp))
            pair_parts.append(jnp.where(lane < SSM_HEAD_DIM, res[0], res[1]))
        y_parts.append(jnp.concatenate(pair_parts, axis=1) + y_inter)
        upd = _dot_tn(xdd[:, g * gw:(g + 1) * gw].astype(bf16), bg)
        for hl in range(hpg):
            h = g * hpg + hl
            rs = slice(g * gw + hl * SSM_HEAD_DIM, g * gw + (hl + 1) * SSM_HEAD_DIM)
            st_ref[rs, :] = (st_g[hl * SSM_HEAD_DIM:(hl + 1) * SSM_HEAD_DIM] * ea_t[h:h + 1, last:last + 1]
                             + upd[hl * SSM_HEAD_DIM:(hl + 1) * SSM_HEAD_DIM])
    y = jnp.concatenate(y_parts, axis=1) + dskip * xs
    y = y * _silu(z.astype(f32))
    normed = []
    for g in range(n_groups):
        yg = y[:, g * gw:(g + 1) * gw]
        normed.append(yg * lax.rsqrt(jnp.mean(yg * yg, axis=-1, keepdims=True) + EPS))
    return jnp.concatenate(normed, axis=1) * wnorm


def _ssd_prompt_body(xbc_ref, dt_ref, dtt_ref, z_ref, wc_ref, bc_ref, dtb_ref, dtbt_ref, a_ref, at_ref,
                     ex_ref, dk_ref, wn_ref, y_ref, st_ref, buf_ref, *, conv_w):
    c = pl.program_id(1)
    l = xbc_ref.shape[0]

    @pl.when(c == 0)
    def _():
        buf_ref[0:SUBLANES, :] = jnp.zeros((SUBLANES, buf_ref.shape[1]), f32)
        st_ref[...] = jnp.zeros(st_ref.shape, f32)

    buf_ref[SUBLANES:SUBLANES + l, :] = xbc_ref[...]
    conv = bc_ref[...]
    for k in range(conv_w):
        off = SUBLANES - (conv_w - 1) + k
        conv = conv + wc_ref[k:k + 1, :] * buf_ref[off:off + l, :]
    buf_ref[0:SUBLANES, :] = buf_ref[l:l + SUBLANES, :]
    y = _ssd_chunk(conv, dt_ref[...], dtt_ref[...], z_ref[...], st_ref, dtb_ref[...], dtbt_ref[...],
                   a_ref[...], at_ref[...], ex_ref[...], dk_ref[...], wn_ref[...], n_valid=None)
    y_ref[...] = y.astype(y_ref.dtype)


def _ssd_prompt(xbc, dt, dtt, z, ssm_w, *, bn, seq_len, n_state):
    r, conv_dim = xbc.shape
    d_inner = z.shape[1]
    l = math.gcd(seq_len, SSM_CHUNK)
    assert l % LANES == 0
    nc = seq_len // l
    wc, bc, dtb, dtbt, a_row, a_col, expand, dskip, wnorm = ssm_w
    conv_w = wc.shape[0]
    assert conv_w - 1 <= SUBLANES
    body = functools.partial(_ssd_prompt_body, conv_w=conv_w)
    consts = [wc, bc, dtb, dtbt, a_row, a_col, expand, dskip, wnorm]
    return pl.pallas_call(
        body,
        grid=(bn, nc),
        in_specs=[pl.BlockSpec((l, conv_dim), lambda b, c: (b * nc + c, 0)),
                  pl.BlockSpec((l, LANES), lambda b, c: (b * nc + c, 0)),
                  pl.BlockSpec((LANES, l), lambda b, c: (0, b * nc + c)),
                  pl.BlockSpec((l, d_inner), lambda b, c: (b * nc + c, 0))]
                 + [_const_spec(w.shape) for w in consts],
        out_specs=[pl.BlockSpec((l, d_inner), lambda b, c: (b * nc + c, 0)),
                   pl.BlockSpec((None, d_inner, n_state), lambda b, c: (b, 0, 0))],
        out_shape=[jax.ShapeDtypeStruct((r, d_inner), bf16),
                   jax.ShapeDtypeStruct((bn, d_inner, n_state), f32)],
        scratch_shapes=[pltpu.VMEM((l + SUBLANES, conv_dim), f32)],
        compiler_params=_params(2),
        name="ssd_prompt",
    )(xbc, dt, dtt, z, *consts)


def _ssd_sample_body(ext_ref, dt_ref, dtt_ref, z_ref, st_in_ref, wc_ref, bc_ref, dtb_ref, dtbt_ref, a_ref,
                     at_ref, ex_ref, dk_ref, wn_ref, y_ref, st_ref, *, conv_w, n_valid):
    lp = dt_ref.shape[0]
    st_ref[...] = st_in_ref[...]
    conv = bc_ref[...]
    for k in range(conv_w):
        conv = conv + wc_ref[k:k + 1, :] * ext_ref[k:k + lp, :]
    y = _ssd_chunk(conv, dt_ref[...], dtt_ref[...], z_ref[...], st_ref, dtb_ref[...], dtbt_ref[...],
                   a_ref[...], at_ref[...], ex_ref[...], dk_ref[...], wn_ref[...], n_valid=n_valid)
    y_ref[...] = y.astype(y_ref.dtype)


def _ssd_sample(ext, dt, dtt, z, state, ssm_w, *, n_valid):
    bs, ext_rows, conv_dim = ext.shape
    lp = dt.shape[1]
    d_inner = z.shape[2]
    n_state = state.shape[2]
    wc, bc, dtb, dtbt, a_row, a_col, expand, dskip, wnorm = ssm_w
    body = functools.partial(_ssd_sample_body, conv_w=wc.shape[0], n_valid=n_valid)
    consts = [wc, bc, dtb, dtbt, a_row, a_col, expand, dskip, wnorm]
    return pl.pallas_call(
        body,
        grid=(bs,),
        in_specs=[pl.BlockSpec((None, ext_rows, conv_dim), lambda s: (s, 0, 0)),
                  pl.BlockSpec((None, lp, LANES), lambda s: (s, 0, 0)),
                  pl.BlockSpec((None, LANES, lp), lambda s: (s, 0, 0)),
                  pl.BlockSpec((None, lp, d_inner), lambda s: (s, 0, 0)),
                  pl.BlockSpec((None, d_inner, n_state), lambda s: (s, 0, 0))]
                 + [_const_spec(w.shape) for w in consts],
        out_specs=[pl.BlockSpec((None, lp, d_inner), lambda s: (s, 0, 0)),
                   pl.BlockSpec((None, d_inner, n_state), lambda s: (s, 0, 0))],
        out_shape=[jax.ShapeDtypeStruct((bs, lp, d_inner), bf16),
                   jax.ShapeDtypeStruct((bs, d_inner, n_state), f32)],
        compiler_params=_params(1),
        name="ssd_sample",
    )(ext, dt, dtt, z, state, *consts)


def _sattn_body(pt_ref, q_ref, kn_ref, vn_ref, bprev_ref, bcur_ref, ck_ref, cv_ref, o_ref,
                km_ref, ms_ref, ls_ref, os_ref, kbuf_ref, vbuf_ref, sem_ref, *, nbs, n_full, n_tok, ppb):
    n_pages = nbs * ppb
    g = pl.program_id(1)
    n_steps = pl.num_programs(1)
    n_rows = N_HEADS * SUBLANES
    d = ATTN_DIM

    step = pl.program_id(0) * n_steps + g
    total = pl.num_programs(0) * n_steps

    def page_copies(n):
        slot = n % PAGE_SLOTS
        seq = n // n_steps
        first = (n % n_steps) * n_pages
        out = []
        for t in range(n_pages):
            pid = pt_ref[seq, first + t]
            out.append(pltpu.make_async_copy(ck_ref.at[pid], kbuf_ref.at[slot, t], sem_ref.at[slot, t]))
            out.append(pltpu.make_async_copy(cv_ref.at[pid], vbuf_ref.at[slot, t], sem_ref.at[slot, n_pages + t]))
        return out

    @pl.when(step == 0)
    def _():
        for n in range(PAGE_SLOTS - 1):
            @pl.when(n < total)
            def _():
                for c in page_copies(n):
                    c.start()

    @pl.when(step + PAGE_SLOTS - 1 < total)
    def _():
        for c in page_copies(step + PAGE_SLOTS - 1):
            c.start()

    for c in page_copies(step):
        c.wait()
    slot = step % PAGE_SLOTS
    kp = [kbuf_ref.at[slot, t] for t in range(n_pages)]
    vp = [vbuf_ref.at[slot, t] for t in range(n_pages)]

    q4 = q_ref[...]
    q8 = jnp.concatenate([q4, jnp.zeros((SUBLANES - n_tok, d), f32)], axis=0)
    r_i = lax.broadcasted_iota(jnp.int32, (n_rows, d), 0)
    c_i = lax.broadcasted_iota(jnp.int32, (n_rows, d), 1)
    head_mask = (r_i // SUBLANES) == (c_i // HEAD_DIM)
    q_rows = jnp.where(head_mask, jnp.concatenate([q8] * N_HEADS, axis=0), 0.0)
    qb = q_rows.astype(bf16)

    @pl.when(g == 0)
    def _():
        km_ref[...] = jnp.zeros(km_ref.shape, f32)
        ms_ref[...] = jnp.zeros(ms_ref.shape, f32)
        ls_ref[...] = jnp.zeros(ls_ref.shape, f32)

    lane_blk = lax.broadcasted_iota(jnp.int32, (1, LANES), 1)
    sc_new = jnp.zeros((n_rows, LANES), f32)
    ms_new = jnp.zeros((n_rows, LANES), f32)
    ls_new = jnp.zeros((n_rows, LANES), f32)
    o_new = []
    k_t = jnp.concatenate([kp[t][...].reshape(d, -1).astype(bf16) for t in range(n_pages)], axis=1)
    s_all = _dot(qb, k_t)
    for b in range(nbs):
        vblk_t = jnp.concatenate([vp[b * ppb + t][...].reshape(d, -1) for t in range(ppb)], axis=1)
        jj = g * nbs + b
        s = s_all[:, b * MOBA_BLOCK:(b + 1) * MOBA_BLOCK]
        here = (lane_blk == jj).astype(f32)
        sc_new = sc_new + (jnp.sum(s, axis=-1, keepdims=True) * (1.0 / MOBA_BLOCK)) * here
        s = s + jnp.where(jj == n_full - 1, 1.0, 0.0).astype(f32) * bprev_ref[...]
        m = jnp.max(s, axis=-1, keepdims=True)
        p = jnp.exp(s - m)
        l = jnp.sum(p, axis=-1, keepdims=True)
        ms_new = ms_new + m * here
        ls_new = ls_new + l * here
        o_new.append(_dot_nt(p.astype(bf16), vblk_t.astype(bf16)))
    km_ref[...] += sc_new
    ms_ref[...] += ms_new
    ls_ref[...] += ls_new
    os_ref[pl.ds(pl.multiple_of(g * nbs, nbs), nbs)] = jnp.stack(o_new, axis=0)

    @pl.when(g == pl.num_programs(1) - 1)
    def _():
        sc = km_ref[...]
        col = lax.broadcasted_iota(jnp.int32, sc.shape, 1)
        low = jnp.float32(-3e38)
        sc = jnp.where(col < n_full, sc, low)
        self_ = jnp.zeros(sc.shape, f32)
        for _ in range(min(MOBA_TOPK, n_full)):
            mx = jnp.max(sc, axis=-1, keepdims=True)
            idx = jnp.min(jnp.where(sc == mx, col, LANES), axis=-1, keepdims=True)
            self_ = jnp.where(col == idx, 1.0, self_)
            sc = jnp.where(col == idx, low, sc)
        sel = self_ > 0.5
        kn = jnp.concatenate([kn_ref[...], jnp.zeros((SAMPLE_ROWS - n_tok, d), f32)], axis=0).astype(bf16)
        vn = jnp.concatenate([vn_ref[...], jnp.zeros((SAMPLE_ROWS - n_tok, d), f32)], axis=0).astype(bf16)
        s_cur = _dot_nt(qb, kn) + bcur_ref[...]
        ms = ms_ref[...]
        m_tot = jnp.maximum(jnp.max(s_cur, axis=-1, keepdims=True),
                            jnp.max(jnp.where(sel, ms, low), axis=-1, keepdims=True))
        p_cur = jnp.exp(s_cur - m_tot)
        w = jnp.where(sel, jnp.exp(ms - m_tot), 0.0)
        l_tot = jnp.sum(p_cur, axis=-1, keepdims=True) + jnp.sum(w * ls_ref[...], axis=-1, keepdims=True)
        acc = _dot(p_cur.astype(bf16), vn)
        for j in range(n_full):
            acc = acc + w[:, j:j + 1] * os_ref[j]
        out = jnp.where(head_mask, acc / l_tot, 0.0)
        out8 = out[0:SUBLANES]
        for h in range(1, N_HEADS):
            out8 = out8 + out[h * SUBLANES:(h + 1) * SUBLANES]
        o_ref[...] = out8[0:n_tok]


def _attn_sample(q, kn, vn, ck, cv, page_table, bprev, bcur):
    bs, n_tok, d = q.shape
    n_pool, n_heads, hd, page = ck.shape
    assert n_heads * hd == d
    ppb = MOBA_BLOCK // page
    n_pages_seq = page_table.shape[1]
    past = n_pages_seq * page
    assert MOBA_BLOCK % page == 0 and past % MOBA_BLOCK == 0 and n_tok <= SUBLANES
    n_full = past // MOBA_BLOCK
    assert n_full <= LANES
    nbs = math.gcd(n_full, SAMPLE_BLOCKS_PER_STEP)
    n_steps = n_full // nbs
    n_pages = nbs * ppb
    n_rows = N_HEADS * SUBLANES

    seq_spec = pl.BlockSpec((None, n_tok, d), lambda s, g, pt: (s, 0, 0))
    body = functools.partial(_sattn_body, nbs=nbs, n_full=n_full, n_tok=n_tok, ppb=ppb)
    grid_spec = pltpu.PrefetchScalarGridSpec(
        num_scalar_prefetch=1,
        grid=(bs, n_steps),
        in_specs=[seq_spec, seq_spec, seq_spec,
                  pl.BlockSpec(bprev.shape, lambda s, g, pt: (0, 0)),
                  pl.BlockSpec(bcur.shape, lambda s, g, pt: (0, 0)),
                  pl.BlockSpec(memory_space=pl.ANY), pl.BlockSpec(memory_space=pl.ANY)],
        out_specs=seq_spec,
        scratch_shapes=[pltpu.VMEM((n_rows, LANES), f32),
                        pltpu.VMEM((n_rows, LANES), f32),
                        pltpu.VMEM((n_rows, LANES), f32),
                        pltpu.VMEM((n_full, n_rows, d), f32),
                        pltpu.VMEM((PAGE_SLOTS, n_pages, n_heads, hd, page), f32),
                        pltpu.VMEM((PAGE_SLOTS, n_pages, n_heads, hd, page), f32),
                        pltpu.SemaphoreType.DMA((PAGE_SLOTS, 2 * n_pages))],
    )
    return pl.pallas_call(
        body,
        grid_spec=grid_spec,
        out_shape=jax.ShapeDtypeStruct((bs, n_tok, d), f32),
        compiler_params=_params(2),
        name="moba_sample",
    )(page_table, q, kn, vn, bprev, bcur, ck, cv)


def _merge_body(at_ref, yn_ref, ga_ref, gb_ref, x_ref, g1_ref, wpa_ref, wps_ref, wo_ref, lg_ref, lb_ref,
                o_ref, *, alpha):
    pa = _dot(at_ref[...].astype(bf16), wpa_ref[...])
    ps = _dot(yn_ref[...].astype(bf16), wps_ref[...])
    merged = jax.nn.sigmoid(ga_ref[...].astype(f32)) * pa + jax.nn.sigmoid(gb_ref[...].astype(f32)) * ps
    mo = _dot(merged.astype(bf16), wo_ref[...])
    o_ref[...] = _layer_norm(alpha * x_ref[...] + g1_ref[...] * mo, lg_ref[...], lb_ref[...])


def _mod_spec(per_row_mod, tm, d, tiles_per_seq):
    if per_row_mod:
        return pl.BlockSpec((tm, d), lambda i: (i, 0))
    return pl.BlockSpec((None, 1, d), lambda i: (i // tiles_per_seq, 0, 0))


def _merge(attn, yn, ga, gb, x2d, gate1, wpa, wps, wo, lg, lb, *, per_row_mod, seq_len, alpha):
    r, d = x2d.shape
    tm = 256
    assert r % tm == 0
    tiles_per_seq = 1 if per_row_mod else seq_len // tm
    ms = _mod_spec(per_row_mod, tm, d, tiles_per_seq)

    def row_spec(w):
        return pl.BlockSpec((tm, w), lambda i: (i, 0))

    return pl.pallas_call(
        functools.partial(_merge_body, alpha=alpha),
        grid=(r // tm,),
        in_specs=[row_spec(attn.shape[1]), row_spec(yn.shape[1]), row_spec(d), row_spec(d), row_spec(d), ms,
                  _const_spec(wpa.shape), _const_spec(wps.shape), _const_spec(wo.shape),
                  _const_spec(lg.shape), _const_spec(lb.shape)],
        out_specs=row_spec(d),
        out_shape=jax.ShapeDtypeStruct((r, d), f32),
        compiler_params=_params(1),
        name="merge_ln1",
    )(attn, yn, ga, gb, x2d, gate1, wpa, wps, wo, lg, lb)


FFN_COL_CHUNKS = 1


def _ffn_body(x_ref, sh_ref, sc_ref, g2_ref, wu_ref, wc_ref, bc_ref, wd_ref, lg_ref, lb_ref, *rest,
              alpha, conv_w, tiles_per_seq, sample_len):
    if sample_len is None:
        y_ref, tail_ref, buf_ref, carry_ref = rest
    else:
        p_refs = rest[:conv_w - 1]
        y_ref, hup_ref, buf_ref, carry_ref = rest[conv_w - 1:]
    x = x_ref[...]
    tm = x.shape[0]
    ff = wd_ref.shape[0]
    cw = ff // FFN_COL_CHUNKS
    t = pl.program_id(0) % tiles_per_seq

    @pl.when(t == 0)
    def _():
        carry_ref[...] = jnp.zeros(carry_ref.shape, f32)

    u = (x * (1.0 + sc_ref[...]) + sh_ref[...]).astype(bf16)
    if sample_len is not None:
        tmod = lax.broadcasted_iota(jnp.int32, (tm, cw), 0) % sample_len
    f = jnp.zeros((tm, x.shape[1]), f32)
    for c in range(FFN_COL_CHUNKS):
        halves = []
        for half in range(2):
            c0 = half * ff + c * cw
            buf_ref[0:SUBLANES, :] = carry_ref[:, c0:c0 + cw]
            hup = _dot(u, wu_ref[:, c0:c0 + cw])
            buf_ref[SUBLANES:SUBLANES + tm, :] = hup
            if sample_len is not None:
                hup_ref[:, c0:c0 + cw] = hup
            hc = bc_ref[:, c0:c0 + cw] + wc_ref[conv_w - 1:conv_w, c0:c0 + cw] * hup
            for k in range(conv_w - 1):
                back = conv_w - 1 - k
                prev = buf_ref[SUBLANES - back:SUBLANES - back + tm, :]
                if sample_len is not None:
                    prev = jnp.where(tmod >= back, prev, p_refs[back - 1][:, c0:c0 + cw])
                hc = hc + wc_ref[k:k + 1, c0:c0 + cw] * prev
            carry_ref[:, c0:c0 + cw] = buf_ref[tm:tm + SUBLANES, :]
            halves.append(hc)
        gact = (_silu(halves[0]) * halves[1]).astype(bf16)
        f = f + _dot(gact, wd_ref[c * cw:(c + 1) * cw, :])
    y_ref[...] = _layer_norm(alpha * x + g2_ref[...] * f, lg_ref[...], lb_ref[...])
    if sample_len is None:
        @pl.when(t == tiles_per_seq - 1)
        def _():
            tail_ref[...] = carry_ref[...]


def _ffn(x2d, shift, scale, gate, wu, wc, bc, wd, lg, lb, prevs, *, per_row_mod, seq_len, bn, alpha,
         sample_len):
    r, d = x2d.shape
    ff2 = wu.shape[1]
    ff = wd.shape[0]
    conv_w = wc.shape[0]
    tm = 256 if sample_len is None else 128
    assert r % tm == 0 and ff % (FFN_COL_CHUNKS * LANES) == 0 and conv_w - 1 <= SUBLANES
    tiles_per_seq = 1 if per_row_mod else seq_len // tm
    ms = _mod_spec(per_row_mod, tm, d, tiles_per_seq)

    def row_spec(w):
        return pl.BlockSpec((tm, w), lambda i: (i, 0))

    in_specs = [row_spec(d), ms, ms, ms, _const_spec(wu.shape), _const_spec(wc.shape), _const_spec(bc.shape),
                _const_spec(wd.shape), _const_spec(lg.shape), _const_spec(lb.shape)]
    args = [x2d, shift, scale, gate, wu, wc, bc, wd, lg, lb]
    if sample_len is None:
        out_shape = [jax.ShapeDtypeStruct((r, d), f32), jax.ShapeDtypeStruct((bn, SUBLANES, ff2), f32)]
        out_specs = [row_spec(d), pl.BlockSpec((None, SUBLANES, ff2), lambda i: (i // tiles_per_seq, 0, 0))]
    else:
        assert tm % sample_len == 0 and len(prevs) == conv_w - 1
        in_specs += [row_spec(ff2)] * len(prevs)
        args += list(prevs)
        out_shape = [jax.ShapeDtypeStruct((r, d), f32), jax.ShapeDtypeStruct((r, ff2), f32)]
        out_specs = [row_spec(d), row_spec(ff2)]
    body = functools.partial(_ffn_body, alpha=alpha, conv_w=conv_w, tiles_per_seq=tiles_per_seq,
                             sample_len=sample_len)
    return pl.pallas_call(
        body,
        grid=(r // tm,),
        in_specs=in_specs,
        out_specs=out_specs,
        out_shape=out_shape,
        scratch_shapes=[pltpu.VMEM((tm + SUBLANES, ff // FFN_COL_CHUNKS), f32),
                        pltpu.VMEM((SUBLANES, ff2), f32)],
        compiler_params=_params(1),
        name="conv_ffn",
    )(*args)


def _rel_bucket(dist):
    n = jnp.maximum(dist, 0)
    max_exact = REL_BUCKETS // 2
    nf = jnp.maximum(n, 1).astype(f32)
    large = max_exact + (jnp.log(nf / max_exact) / math.log(REL_MAX_DIST / max_exact)
                         * (REL_BUCKETS - max_exact)).astype(jnp.int32)
    large = jnp.minimum(large, REL_BUCKETS - 1)
    return jnp.where(n < max_exact, n, large)


def _rel_bias(rel_table, dist):
    rel = rel_table - rel_table[REL_BUCKETS - 1]
    onehot = (_rel_bucket(dist)[..., None] == jnp.arange(REL_BUCKETS)).astype(f32)
    b = jnp.dot(onehot, rel, precision=lax.Precision.HIGHEST)
    return jnp.moveaxis(b, -1, 0)


def _prompt_bias_tables(rel_table):
    a = jnp.arange(MOBA_BLOCK, dtype=jnp.int32)
    dist_d = a[None, :] - a[:, None]
    bd = jnp.where(dist_d >= 0, _rel_bias(rel_table, dist_d) * LOG2E, MASKED)
    bp = _rel_bias(rel_table, dist_d + MOBA_BLOCK) * LOG2E
    return bd.astype(f32), bp.astype(f32)


def _sample_bias_tables(rel_table, n_tok):
    n_rows = N_HEADS * SUBLANES
    t = jnp.arange(SUBLANES, dtype=jnp.int32)
    a = jnp.arange(MOBA_BLOCK, dtype=jnp.int32)
    bprev = _rel_bias(rel_table, MOBA_BLOCK + t[:, None] - a[None, :])
    bprev = jnp.where((t < n_tok)[None, :, None], bprev, 0.0).reshape(n_rows, MOBA_BLOCK)
    tk = jnp.arange(SAMPLE_ROWS, dtype=jnp.int32)
    dist = t[:, None] - tk[None, :]
    ok = (dist >= 0) & (t[:, None] < n_tok) & (tk[None, :] < n_tok)
    bcur = jnp.where(ok[None], _rel_bias(rel_table, dist), MASKED).reshape(n_rows, SAMPLE_ROWS)
    return bprev.astype(f32), bcur.astype(f32)


def _pad_cols(w, n):
    return jnp.pad(w, ((0, 0), (0, n - w.shape[1])))


def kernel(x_prompt, x_sample, cache_k, cache_v, page_table, state_ssm, state_conv_ssm, state_conv_ffn,
           c_prompt, c_sample, rel_table, w_ada, b_ada, w_in, w_conv_ssm, b_conv_ssm, dt_bias, a_log,
           d_skip, w_norm_ssm, w_proj_attn, w_proj_ssm, w_out, ln1_g, ln1_b, w_up, w_conv_ffn, b_conv_ffn,
           w_down, ln2_g, ln2_b):
    depth = w_ada.shape[0]
    alpha = (2 * depth) ** 0.25
    bp_, seq, d = x_prompt.shape
    bs, n_tok, _ = x_sample.shape
    ssm_heads = dt_bias.shape[1]
    d_inner = ssm_heads * SSM_HEAD_DIM
    n_state = state_ssm.shape[-1]
    conv_dim = w_conv_ssm.shape[-1]
    ff2 = w_up.shape[-1]
    ssm_conv = w_conv_ssm.shape[1]
    ffn_conv = w_conv_ffn.shape[1]
    assert ssm_heads <= LANES and n_tok <= SUBLANES
    page = cache_k.shape[2]
    scale = HEAD_DIM ** -0.5

    bd, bpv = _prompt_bias_tables(rel_table)
    bprev_s, bcur_s = _sample_bias_tables(rel_table, n_tok)
    expand = (jnp.arange(LANES)[:, None] == (jnp.arange(d_inner)[None, :] // SSM_HEAD_DIM)).astype(bf16)

    yp = x_prompt.reshape(bp_ * seq, d)
    ys = x_sample.reshape(bs * n_tok, d)
    outs_p = [[] for _ in range(5)]
    outs_s = [[] for _ in range(5)]
    for l in range(depth):
        cuts = [ATTN_DIM, 2 * ATTN_DIM, 3 * ATTN_DIM, 3 * ATTN_DIM + d_inner,
                3 * ATTN_DIM + d_inner + conv_dim, 3 * ATTN_DIM + d_inner + conv_dim + ssm_heads,
                3 * ATTN_DIM + d_inner + conv_dim + ssm_heads + d]
        wq, wk, wv, wz, wxbc, wdt, wga, wgb = jnp.split(w_in[l], cuts, axis=1)
        wdt = _pad_cols(wdt, LANES)
        wq = wq * scale
        nat_p = [(ATTN_DIM, f32), (ATTN_DIM, f32), (d_inner, bf16), (conv_dim, f32), (LANES, f32),
                 (d, bf16), (d, bf16)]
        wn_p = jnp.concatenate([wk, wv, wz, wxbc, wdt, wga, wgb], axis=1).astype(bf16)
        tr_p = [(ATTN_DIM, bf16, 0), (ATTN_DIM, bf16, BF16_ROWS), (LANES, f32, 0)]
        wt_p = jnp.concatenate([wq * LOG2E, wv, wdt], axis=1).T.astype(bf16)
        nat_s = [(ATTN_DIM, f32)] + nat_p
        wn_s = jnp.concatenate([wq.astype(bf16), wn_p], axis=1)
        tr_s = [(LANES, f32, 0)]
        wt_s = wdt.T.astype(bf16)
        ssm_w = (w_conv_ssm[l], b_conv_ssm[l].reshape(1, conv_dim),
                 _pad_cols(dt_bias[l].reshape(1, -1), LANES), _pad_cols(dt_bias[l].reshape(1, -1), LANES).T,
                 -jnp.exp(_pad_cols(a_log[l].reshape(1, -1), LANES)),
                 -jnp.exp(_pad_cols(a_log[l].reshape(1, -1), LANES)).T,
                 expand, jnp.repeat(d_skip[l], SSM_HEAD_DIM).reshape(1, d_inner),
                 w_norm_ssm[l].reshape(1, d_inner))
        wpa, wps, wo = (w_proj_attn[l].astype(bf16), w_proj_ssm[l].astype(bf16), w_out[l].astype(bf16))
        lg1, lb1 = ln1_g[l].reshape(1, d), ln1_b[l].reshape(1, d)
        lg2, lb2 = ln2_g[l].reshape(1, d), ln2_b[l].reshape(1, d)
        wu, wd = w_up[l].astype(bf16), w_down[l].astype(bf16)
        wcf, bcf = w_conv_ffn[l], b_conv_ffn[l].reshape(1, ff2)

        mod = _ada(jnp.concatenate([c_prompt, c_sample], axis=0), w_ada[l], b_ada[l])
        mod_p = mod[:bp_].reshape(bp_, 6, 1, d)
        mod_s = jnp.repeat(mod[bp_:].reshape(bs, 6, 1, d), n_tok, axis=2).reshape(bs, 6, n_tok, d)
        mod_s = jnp.moveaxis(mod_s, 1, 0).reshape(6, bs * n_tok, d)
        sh1p, sc1p, g1p, sh2p, sc2p, g2p = (mod_p[:, i] for i in range(6))
        sh1s, sc1s, g1s, sh2s, sc2s, g2s = (mod_s[i] for i in range(6))

        k_p, v_p, z_p, xbc_p, dt_p, ga_p, gb_p, qT_p, vT_p, dtT_p, k2_p = _inproj(
            yp, sh1p, sc1p, wn_p, wt_p, nat_p, tr_p, per_row_mod=False, seq_len=seq, attn_layout=True)
        attn_p = _attn_prompt(k2_p, qT_p, vT_p, bd, bpv, bn=bp_, seq_len=seq)
        yn_p, st_p = _ssd_prompt(xbc_p, dt_p, dtT_p, z_p, ssm_w, bn=bp_, seq_len=seq, n_state=n_state)
        x1_p = _merge(attn_p, yn_p, ga_p, gb_p, yp, g1p, wpa, wps, wo, lg1, lb1,
                      per_row_mod=False, seq_len=seq, alpha=alpha)
        yp, tail_p = _ffn(x1_p, sh2p, sc2p, g2p, wu, wcf, bcf, wd, lg2, lb2, (),
                          per_row_mod=False, seq_len=seq, bn=bp_, alpha=alpha, sample_len=None)
        outs_p[0].append(k_p.reshape(bp_, seq, N_HEADS, HEAD_DIM))
        outs_p[1].append(v_p.reshape(bp_, seq, N_HEADS, HEAD_DIM))
        outs_p[2].append(st_p.reshape(bp_, ssm_heads, SSM_HEAD_DIM, n_state))
        outs_p[3].append(xbc_p.reshape(bp_, seq, conv_dim)[:, seq - (ssm_conv - 1):])
        outs_p[4].append(tail_p[:, SUBLANES - (ffn_conv - 1):])

        r_s = bs * n_tok
        r_pad = -(-r_s // 256) * 256

        def pad_rows(a):
            return jnp.pad(a, ((0, r_pad - r_s), (0, 0)))

        q_s, k_s, v_s, z_s, xbc_s, dt_s, ga_s, gb_s, dtT_s = _inproj(
            pad_rows(ys), pad_rows(sh1s), pad_rows(sc1s), wn_s, wt_s, nat_s, tr_s,
            per_row_mod=True, seq_len=None, attn_layout=False)
        q_s, k_s, v_s = (a[:r_s].reshape(bs, n_tok, ATTN_DIM) for a in (q_s, k_s, v_s))
        attn_s = _attn_sample(q_s, k_s, v_s, jnp.transpose(cache_k[l], (0, 2, 3, 1)),
                              jnp.transpose(cache_v[l], (0, 2, 3, 1)), page_table, bprev_s, bcur_s)
        row_pad = SAMPLE_ROWS - n_tok
        xbc_s3 = xbc_s[:r_s].reshape(bs, n_tok, conv_dim)
        ext = jnp.concatenate([state_conv_ssm[l], xbc_s3,
                               jnp.zeros((bs, row_pad + SUBLANES - (ssm_conv - 1), conv_dim), f32)], axis=1)
        dt_s3 = jnp.pad(dt_s[:r_s].reshape(bs, n_tok, LANES), ((0, 0), (0, row_pad), (0, 0)))
        dtT_s3 = jnp.pad(jnp.moveaxis(dtT_s[:, :r_s].reshape(LANES, bs, n_tok), 0, 1),
                         ((0, 0), (0, 0), (0, row_pad)))
        z_s3 = jnp.pad(z_s[:r_s].reshape(bs, n_tok, d_inner), ((0, 0), (0, row_pad), (0, 0)))
        yn_s3, st_s = _ssd_sample(ext, dt_s3, dtT_s3, z_s3, state_ssm[l].reshape(bs, d_inner, n_state),
                                  ssm_w, n_valid=n_tok)
        yn_s = yn_s3[:, :n_tok].reshape(r_s, d_inner)
        x1_s = _merge(pad_rows(attn_s.reshape(r_s, ATTN_DIM)), pad_rows(yn_s), ga_s, gb_s, pad_rows(ys),
                      pad_rows(g1s), wpa, wps, wo, lg1, lb1, per_row_mod=True, seq_len=None, alpha=alpha)
        cf = state_conv_ffn[l]
        prevs = []
        for back in range(1, ffn_conv):
            rows = [cf[:, ffn_conv - 1 - back + t] if t < back else jnp.zeros((bs, ff2), f32)
                    for t in range(n_tok)]
            prevs.append(pad_rows(jnp.stack(rows, axis=1).reshape(r_s, ff2)))
        y_s, hup_s = _ffn(x1_s, pad_rows(sh2s), pad_rows(sc2s), pad_rows(g2s), wu, wcf, bcf, wd, lg2, lb2,
                          prevs, per_row_mod=True, seq_len=None, bn=bs, alpha=alpha, sample_len=n_tok)
        ys = y_s[:r_s]
        outs_s[0].append(k_s.reshape(bs, n_tok, N_HEADS, HEAD_DIM))
        outs_s[1].append(v_s.reshape(bs, n_tok, N_HEADS, HEAD_DIM))
        outs_s[2].append(st_s.reshape(bs, ssm_heads, SSM_HEAD_DIM, n_state))
        cs_ext = jnp.concatenate([state_conv_ssm[l], xbc_s3], axis=1)
        outs_s[3].append(cs_ext[:, -(ssm_conv - 1):])
        cf_ext = jnp.concatenate([cf, hup_s[:r_s].reshape(bs, n_tok, ff2)], axis=1)
        outs_s[4].append(cf_ext[:, -(ffn_conv - 1):])

    return (yp.reshape(bp_, seq, d), ys.reshape(bs, n_tok, d),
            *(jnp.stack(o) for o in outs_p), *(jnp.stack(o) for o in outs_s))
```

```python
import functools
import math

import jax
import jax.numpy as jnp
from jax import lax
from jax.experimental import pallas as pl
from jax.experimental.pallas import tpu as pltpu

N_HEADS = 8
HEAD_DIM = 64
ATTN_DIM = N_HEADS * HEAD_DIM
MOBA_BLOCK = 256
MOBA_TOPK = 3
REL_BUCKETS = 32
REL_MAX_DIST = 128
SSM_HEAD_DIM = 64
SSM_CHUNK = 256
EPS = 1e-5

LANES = 128
SUBLANES = 8
BF16_ROWS = 16
VMEM_LIMIT = 56 * 1024 * 1024

MASKED = -1e30
SAMPLE_ROWS = 16
SAMPLE_BLOCKS_PER_STEP = 4
PAGE_SLOTS = 3
SSD_SAMPLE_SEQS = 2
FAR_GROUP = 4
LOG2E = math.log2(math.e)

f32 = jnp.float32
bf16 = jnp.bfloat16

_NT = (((1,), (1,)), ((), ()))
_TN = (((0,), (0,)), ((), ()))


def _dot(a, b):
    return jnp.dot(a, b, preferred_element_type=f32)


def _dot_nt(a, b):
    return lax.dot_general(a, b, _NT, preferred_element_type=f32)


def _dot_tn(a, b):
    return lax.dot_general(a, b, _TN, preferred_element_type=f32)


def _split3(x):
    hi = x.astype(bf16)
    r = x - hi.astype(f32)
    mid = r.astype(bf16)
    lo = (r - mid.astype(f32)).astype(bf16)
    return hi, mid, lo


def _dot_exact_rhs(a, m):
    hi, mid, lo = _split3(a)
    return _dot(hi, m) + _dot(mid, m) + _dot(lo, m)


def _dot_exact_lhs(m, a):
    hi, mid, lo = _split3(a)
    return _dot(m, hi) + _dot(m, mid) + _dot(m, lo)


def _silu(x):
    return x * jax.nn.sigmoid(x)


def _softplus(x):
    return jnp.maximum(x, 0.0) + jnp.log1p(jnp.exp(-jnp.abs(x)))


def _layer_norm(x, g, b):
    mu = jnp.mean(x, axis=-1, keepdims=True)
    xc = x - mu
    var = jnp.mean(xc * xc, axis=-1, keepdims=True)
    return xc * lax.rsqrt(var + EPS) * g + b


def _const_spec(shape):
    nd = len(shape)
    return pl.BlockSpec(shape, lambda *_: (0,) * nd, pipeline_mode=pl.Buffered(1))


def _params(n_grid, flags=None):
    return pltpu.CompilerParams(
        dimension_semantics=("arbitrary",) * n_grid, vmem_limit_bytes=VMEM_LIMIT, flags=flags)


def _ada_body(c_ref, w_ref, b_ref, o_ref):
    a = _silu(c_ref[...])
    hi, mid, lo = _split3(a)
    whi, wmid, wlo = _split3(w_ref[...])
    acc = _dot(hi, whi) + (_dot(hi, wmid) + _dot(mid, whi))
    acc = acc + (_dot(hi, wlo) + _dot(mid, wmid) + _dot(lo, whi))
    o_ref[...] = acc + b_ref[...]


def _ada(c, w, b):
    n, d = c.shape
    dn = w.shape[1]
    tn = 1024 if dn % 1024 == 0 else dn
    return pl.pallas_call(
        _ada_body,
        grid=(dn // tn,),
        in_specs=[pl.BlockSpec((n, d), lambda j: (0, 0)),
                  pl.BlockSpec((d, tn), lambda j: (0, j)),
                  pl.BlockSpec((1, tn), lambda j: (0, j))],
        out_specs=pl.BlockSpec((n, tn), lambda j: (0, j)),
        out_shape=jax.ShapeDtypeStruct((n, dn), f32),
        compiler_params=_params(1),
        name="ada_mod",
    )(c, w, b.reshape(1, dn))


def _inproj_body(x_ref, sh_ref, sc_ref, wn_ref, wt_ref, *outs, nat, tr, k_col, tiles_per_seq):
    u = (x_ref[...] * (1.0 + sc_ref[...]) + sh_ref[...]).astype(bf16)
    tm = u.shape[0]
    o = 0
    col = 0
    for width, dt in nat:
        res = _dot(u, wn_ref[:, col:col + width])
        if k_col is not None and col == k_col:
            kf = res
        if dt is not None:
            outs[o][...] = res.astype(dt)
            o += 1
        col += width
    if tr:
        t = _dot_nt(wt_ref[...], u)
        for row, height, _, ones_rows, _ in tr:
            if ones_rows:
                step = HEAD_DIM + ones_rows
                for h in range(height // HEAD_DIM):
                    outs[o][h * step:h * step + HEAD_DIM, :] = (
                        t[row + h * HEAD_DIM:row + (h + 1) * HEAD_DIM].astype(outs[o].dtype))
                    outs[o][h * step + HEAD_DIM:(h + 1) * step, :] = jnp.ones((ones_rows, tm), outs[o].dtype)
            else:
                outs[o][...] = t[row:row + height].astype(outs[o].dtype)
            o += 1
    if k_col is not None:
        pos = (pl.program_id(0) % tiles_per_seq) * tm
        blk = pos // MOBA_BLOCK
        onehot = (lax.broadcasted_iota(jnp.int32, (tm, LANES), 1) == blk).astype(bf16)
        k2_ref = outs[o]
        for p in range(ATTN_DIM // LANES):
            k2_ref[p] = jnp.concatenate([kf[:, p * LANES:(p + 1) * LANES].astype(bf16), onehot], axis=1)


def _inproj(x2d, shift, scale, wn, wt, nat, tr, *, per_row_mod, seq_len, attn_layout):
    r, d = x2d.shape
    tm = 256
    assert r % tm == 0
    n_tiles = r // tm
    if per_row_mod:
        mod_spec = pl.BlockSpec((tm, d), lambda i: (i, 0))
        tiles_per_seq = 1
    else:
        assert seq_len % tm == 0 and MOBA_BLOCK % tm == 0
        tiles_per_seq = seq_len // tm
        mod_spec = pl.BlockSpec((None, 1, d), lambda i: (i // tiles_per_seq, 0, 0))
    out_shape, out_specs = [], []
    for width, dt in nat:
        if dt is not None:
            out_shape.append(jax.ShapeDtypeStruct((r, width), dt))
            out_specs.append(pl.BlockSpec((tm, width), lambda i: (i, 0)))
    for _, height, dt, ones_rows, per_seq in tr:
        rows = height + (height // HEAD_DIM) * ones_rows
        if per_seq:
            out_shape.append(jax.ShapeDtypeStruct((r // seq_len, rows, seq_len), dt))
            out_specs.append(pl.BlockSpec((None, rows, tm), lambda i: (i // tiles_per_seq, 0, i % tiles_per_seq)))
        else:
            out_shape.append(jax.ShapeDtypeStruct((rows, r), dt))
            out_specs.append(pl.BlockSpec((rows, tm), lambda i: (0, i)))
    k_col = None
    if attn_layout:
        k_col = 0
        n_pairs = ATTN_DIM // LANES
        out_shape.append(jax.ShapeDtypeStruct((n_pairs, r, 2 * LANES), bf16))
        out_specs.append(pl.BlockSpec((n_pairs, tm, 2 * LANES), lambda i: (0, i, 0)))
    body = functools.partial(_inproj_body, nat=tuple(nat), tr=tuple(tr), k_col=k_col,
                             tiles_per_seq=tiles_per_seq)
    return pl.pallas_call(
        body,
        grid=(n_tiles,),
        in_specs=[pl.BlockSpec((tm, d), lambda i: (i, 0)), mod_spec, mod_spec,
                  _const_spec(wn.shape), _const_spec(wt.shape)],
        out_specs=out_specs,
        out_shape=out_shape,
        compiler_params=_params(1),
        name="in_proj",
    )(x2d, shift, scale, wn, wt)


def _select_topk_rows(s, n_valid_rows):
    nblk = s.shape[0]
    row = lax.broadcasted_iota(jnp.int32, s.shape, 0)
    low = jnp.float32(-3e38)
    s = jnp.where(row < n_valid_rows, s, low)
    sel = jnp.zeros(s.shape, jnp.bool_)
    for _ in range(MOBA_TOPK):
        m = jnp.max(s, axis=0, keepdims=True)
        idx = jnp.min(jnp.where(s == m, row, nblk), axis=0, keepdims=True)
        hit = (row == idx) & (m > low)
        sel = sel | hit
        s = jnp.where(row == idx, low, s)
    return sel


def _attn_body(k2_ref, qT_ref, vT_ref, bd_ref, bp_ref, o_ref, km_ref, q2n_ref, q2f_ref, sa_ref, sb_ref, *,
               n_blocks, group):
    i = pl.program_id(2)
    blk = MOBA_BLOCK
    hd = HEAD_DIM
    nsel = km_ref.shape[0]

    @pl.when(i == 0)
    def _():
        km_ref[...] = jnp.zeros(km_ref.shape, f32)

        def mean_body(j, c):
            kk = k2_ref[pl.ds(pl.multiple_of(j * blk, blk), blk), :].astype(f32)
            km_ref[pl.ds(j, 1), :] = jnp.sum(kk, axis=0, keepdims=True) * (1.0 / blk)
            return c
        lax.fori_loop(0, n_blocks, mean_body, 0)

    q_pair = qT_ref[:, pl.ds(pl.multiple_of(i * blk, blk), blk)]
    zq = jnp.zeros((hd, blk), bf16)
    q_rows = jnp.concatenate([jnp.concatenate([q_pair[0:hd], zq], axis=1),
                              jnp.concatenate([zq, q_pair[hd:2 * hd]], axis=1)], axis=0)
    zrest = jnp.zeros((q2f_ref.shape[0] - 2 * hd, 2 * blk), bf16)
    q2f_ref[0:2 * hd, :] = q_rows
    q2f_ref[2 * hd:, :] = zrest
    q2n_ref[0:2 * hd, :] = q_rows
    q2n_ref[2 * hd:, :] = zrest
    km = km_ref[...]
    km_hi = km.astype(bf16)
    km_lo = (km - km_hi.astype(f32)).astype(bf16)
    q2 = q2f_ref[...]
    s_blk = _dot(km_hi, q2) + _dot(km_lo, q2)
    sel = _select_topk_rows(s_blk, i)
    rowsel = lax.broadcasted_iota(jnp.int32, (nsel, 2 * blk), 0)
    far = sel & (rowsel < i - 1)
    near = (sel & (rowsel == i - 1)) | (rowsel == i)
    q2f_ref[2 * hd:2 * hd + nsel, :] = jnp.where(far, 0.0, MASKED).astype(bf16)
    q2n_ref[2 * hd:2 * hd + nsel, :] = jnp.where(near, 0.0, MASKED).astype(bf16)

    def k_rows(j, n):
        return k2_ref[pl.ds(pl.multiple_of(j * blk, blk), n * blk), :]

    vrows = vT_ref.shape[0] // 2

    def v_cols(h, j, n):
        return vT_ref[h * vrows:(h + 1) * vrows, pl.ds(pl.multiple_of(j * blk, blk), n * blk)]

    jp = jnp.maximum(i - 1, 0)
    first_pad = jnp.where(i > 0, 0.0, MASKED).astype(f32)
    s_cur = _dot(k_rows(i, 1), q2n_ref[...])
    s_prev = _dot(k_rows(jp, 1), q2n_ref[...])
    carry = []
    for h in range(2):
        s = jnp.concatenate([s_cur[:, h * blk:(h + 1) * blk] + bd_ref[h],
                             s_prev[:, h * blk:(h + 1) * blk] + (bp_ref[h] + first_pad)], axis=0)
        m = jnp.max(s, axis=0, keepdims=True)
        p = jnp.exp2((s - m).astype(bf16))
        v = jnp.concatenate([v_cols(h, i, 1), v_cols(h, jp, 1)], axis=1)
        carry += [m, _dot(v, p)]

    last_group = n_blocks // group - 1

    def logits_into(s_ref, g):
        s_ref[...] = _dot(k_rows(jnp.minimum(g, last_group) * group, group), q2f_ref[...])

    def consume(s_ref, g, carry):
        j0 = jnp.minimum(g, last_group) * group
        pad = jnp.where(g < n_groups, 0.0, MASKED).astype(f32)
        out = []
        for h in range(2):
            m, acc = carry[2 * h:2 * h + 2]
            m2 = jnp.maximum(m, jnp.max(s_ref[:, h * blk:(h + 1) * blk], axis=0, keepdims=True) + pad)
            a = jnp.exp2(m - m2)
            p = jnp.exp2((s_ref[:, h * blk:(h + 1) * blk] - (m2 - pad)).astype(bf16))
            out += [m2, a * acc + _dot(v_cols(h, j0, group), p)]
        return out

    def far_body(t, carry):
        logits_into(sb_ref, 2 * t + 1)
        carry = consume(sa_ref, 2 * t, list(carry))
        logits_into(sa_ref, 2 * t + 2)
        return tuple(consume(sb_ref, 2 * t + 1, carry))

    n_groups = (jnp.maximum(i - 1, 0) + group - 1) // group
    logits_into(sa_ref, 0)
    carry = lax.fori_loop(0, (n_groups + 1) // 2, far_body, tuple(carry))
    outs = [carry[2 * h + 1][0:hd] / carry[2 * h + 1][hd:hd + 1] for h in range(2)]
    o_ref[...] = jnp.concatenate(outs, axis=0).T.astype(o_ref.dtype)


def _attn_prompt(k2, qT, vT, bd, bp, *, bn, seq_len):
    n_pairs, r, _ = k2.shape
    blk = MOBA_BLOCK
    assert seq_len % blk == 0
    nb = seq_len // blk
    nsel = HEAD_DIM
    assert nb <= nsel
    group = math.gcd(nb, FAR_GROUP)
    body = functools.partial(_attn_body, n_blocks=nb, group=group)
    return pl.pallas_call(
        body,
        grid=(bn, n_pairs, nb),
        in_specs=[pl.BlockSpec((None, seq_len, 2 * LANES), lambda b, p, i: (p, b, 0)),
                  pl.BlockSpec((LANES, seq_len), lambda b, p, i: (p, b)),
                  pl.BlockSpec((vT.shape[0] // n_pairs, seq_len), lambda b, p, i: (p, b)),
                  pl.BlockSpec((2, blk, blk), lambda b, p, i: (p, 0, 0)),
                  pl.BlockSpec((2, blk, blk), lambda b, p, i: (p, 0, 0))],
        out_specs=pl.BlockSpec((blk, LANES), lambda b, p, i: (b * nb + i, p)),
        out_shape=jax.ShapeDtypeStruct((r, ATTN_DIM), bf16),
        scratch_shapes=[pltpu.VMEM((nsel, 2 * LANES), f32),
                        pltpu.VMEM((2 * LANES, 2 * blk), bf16),
                        pltpu.VMEM((2 * LANES, 2 * blk), bf16),
                        pltpu.VMEM((group * blk, 2 * blk), f32),
                        pltpu.VMEM((group * blk, 2 * blk), f32)],
        compiler_params=_params(3),
        name="moba_prompt",
    )(k2, qT, vT, bd, bp)


def _ssd_chunk(conv, dt_raw, dt_raw_t, z, st_ref, dtb, dtb_t, a_row, a_col, expand, dskip, wnorm,
               *, n_valid):
    l = conv.shape[0]
    d_inner = z.shape[1]
    n_state = st_ref.shape[1]
    n_heads = d_inner // SSM_HEAD_DIM
    n_groups = (conv.shape[1] - d_inner) // (2 * n_state)
    hpg = n_heads // n_groups
    gw = hpg * SSM_HEAD_DIM
    last = (l if n_valid is None else n_valid) - 1

    act = _silu(conv)
    xs = act[:, :d_inner]
    bm = act[:, d_inner:d_inner + n_groups * n_state]
    cm = act[:, d_inner + n_groups * n_state:]

    dt = _softplus(dt_raw + dtb)
    dt_t = _softplus(dt_raw_t + dtb_t)
    r_i = lax.broadcasted_iota(jnp.int32, (l, l), 0)
    c_i = lax.broadcasted_iota(jnp.int32, (l, l), 1)
    causal = r_i >= c_i
    tril = causal.astype(bf16)
    triu = (r_i <= c_i).astype(bf16)
    acs = _dot_exact_lhs(tril, dt * a_row)
    acs_t = _dot_exact_rhs(dt_t * a_col, triu)
    eacs = jnp.exp(acs)
    dec = jnp.exp(acs[last:last + 1, :] - acs)
    ea_t = jnp.exp(acs_t)

    dt_full = _dot_exact_rhs(dt, expand)
    eacs_full = _dot_exact_rhs(eacs, expand)
    dec_full = _dot_exact_rhs(dec, expand)
    xd = xs * dt_full
    xdd = xd * dec_full
    if n_valid is not None:
        rows = lax.broadcasted_iota(jnp.int32, xdd.shape, 0)
        xdd = jnp.where(rows < n_valid, xdd, 0.0)

    lane = lax.broadcasted_iota(jnp.int32, (l, 2 * SSM_HEAD_DIM), 1)
    y_parts = []
    for g in range(n_groups):
        bg = bm[:, g * n_state:(g + 1) * n_state].astype(bf16)
        cg = cm[:, g * n_state:(g + 1) * n_state].astype(bf16)
        cb = _dot_nt(cg, bg)
        st_g = st_ref[g * gw:(g + 1) * gw, :]
        y_inter = _dot_nt(cg, st_g.astype(bf16)) * eacs_full[:, g * gw:(g + 1) * gw]
        pair_parts = []
        for q in range(hpg // 2):
            h0 = g * hpg + 2 * q
            xdp = xd[:, h0 * SSM_HEAD_DIM:(h0 + 2) * SSM_HEAD_DIM].astype(bf16)
            res = []
            for h in (h0, h0 + 1):
                seg = acs[:, h:h + 1] - acs_t[h:h + 1, :]
                lm = jnp.exp(jnp.where(causal, seg, MASKED))
                res.append(_dot((cb * lm).astype(bf16), xdp))
            pair_parts.append(jnp.where(lane < SSM_HEAD_DIM, res[0], res[1]))
        y_parts.append(jnp.concatenate(pair_parts, axis=1) + y_inter)
        upd = _dot_tn(xdd[:, g * gw:(g + 1) * gw].astype(bf16), bg)
        for hl in range(hpg):
            h = g * hpg + hl
            rs = slice(g * gw + hl * SSM_HEAD_DIM, g * gw + (hl + 1) * SSM_HEAD_DIM)
            st_ref[rs, :] = (st_g[hl * SSM_HEAD_DIM:(hl + 1) * SSM_HEAD_DIM] * ea_t[h:h + 1, last:last + 1]
                             + upd[hl * SSM_HEAD_DIM:(hl + 1) * SSM_HEAD_DIM])
    y = jnp.concatenate(y_parts, axis=1) + dskip * xs
    y = y * _silu(z.astype(f32))
    normed = []
    for g in range(n_groups):
        yg = y[:, g * gw:(g + 1) * gw]
        normed.append(yg * lax.rsqrt(jnp.mean(yg * yg, axis=-1, keepdims=True) + EPS))
    return jnp.concatenate(normed, axis=1) * wnorm


def _ssd_prompt_body(xbc_ref, dt_ref, dtt_ref, z_ref, wc_ref, bc_ref, dtb_ref, dtbt_ref, a_ref, at_ref,
                     ex_ref, dk_ref, wn_ref, y_ref, st_ref, buf_ref, *, conv_w):
    c = pl.program_id(1)
    l = xbc_ref.shape[0]

    @pl.when(c == 0)
    def _():
        buf_ref[0:SUBLANES, :] = jnp.zeros((SUBLANES, buf_ref.shape[1]), f32)
        st_ref[...] = jnp.zeros(st_ref.shape, f32)

    buf_ref[SUBLANES:SUBLANES + l, :] = xbc_ref[...]
    conv = bc_ref[...]
    for k in range(conv_w):
        off = SUBLANES - (conv_w - 1) + k
        conv = conv + wc_ref[k:k + 1, :] * buf_ref[off:off + l, :]
    buf_ref[0:SUBLANES, :] = buf_ref[l:l + SUBLANES, :]
    y = _ssd_chunk(conv, dt_ref[...], dtt_ref[...], z_ref[...], st_ref, dtb_ref[...], dtbt_ref[...],
                   a_ref[...], at_ref[...], ex_ref[...], dk_ref[...], wn_ref[...], n_valid=None)
    y_ref[...] = y.astype(y_ref.dtype)


def _ssd_prompt(xbc, dt, dtt, z, ssm_w, *, bn, seq_len, n_state):
    r, conv_dim = xbc.shape
    d_inner = z.shape[1]
    l = math.gcd(seq_len, SSM_CHUNK)
    assert l % LANES == 0
    nc = seq_len // l
    wc, bc, dtb, dtbt, a_row, a_col, expand, dskip, wnorm = ssm_w
    conv_w = wc.shape[0]
    assert conv_w - 1 <= SUBLANES
    body = functools.partial(_ssd_prompt_body, conv_w=conv_w)
    consts = [wc, bc, dtb, dtbt, a_row, a_col, expand, dskip, wnorm]
    return pl.pallas_call(
        body,
        grid=(bn, nc),
        in_specs=[pl.BlockSpec((l, conv_dim), lambda b, c: (b * nc + c, 0)),
                  pl.BlockSpec((l, LANES), lambda b, c: (b * nc + c, 0)),
                  pl.BlockSpec((LANES, l), lambda b, c: (0, b * nc + c)),
                  pl.BlockSpec((l, d_inner), lambda b, c: (b * nc + c, 0))]
                 + [_const_spec(w.shape) for w in consts],
        out_specs=[pl.BlockSpec((l, d_inner), lambda b, c: (b * nc + c, 0)),
                   pl.BlockSpec((None, d_inner, n_state), lambda b, c: (b, 0, 0))],
        out_shape=[jax.ShapeDtypeStruct((r, d_inner), bf16),
                   jax.ShapeDtypeStruct((bn, d_inner, n_state), f32)],
        scratch_shapes=[pltpu.VMEM((l + SUBLANES, conv_dim), f32)],
        compiler_params=_params(2),
        name="ssd_prompt",
    )(xbc, dt, dtt, z, *consts)


def _ssd_sample_body(ext_ref, dt_ref, dtt_ref, z_ref, st_in_ref, wc_ref, bc_ref, dtb_ref, dtbt_ref, a_ref,
                     at_ref, ex_ref, dk_ref, wn_ref, y_ref, st_ref, *, conv_w, n_valid):
    lp = dt_ref.shape[1]
    st_ref[...] = st_in_ref[...]
    for u in range(dt_ref.shape[0]):
        conv = bc_ref[...]
        for k in range(conv_w):
            conv = conv + wc_ref[k:k + 1, :] * ext_ref[u, k:k + lp, :]
        y = _ssd_chunk(conv, dt_ref[u], dtt_ref[u], z_ref[u], st_ref.at[u], dtb_ref[...], dtbt_ref[...],
                       a_ref[...], at_ref[...], ex_ref[...], dk_ref[...], wn_ref[...], n_valid=n_valid)
        y_ref[u] = y.astype(y_ref.dtype)


def _ssd_sample(ext, dt, dtt, z, state, ssm_w, *, n_valid):
    bs, ext_rows, conv_dim = ext.shape
    lp = dt.shape[1]
    d_inner = z.shape[2]
    n_state = state.shape[2]
    wc, bc, dtb, dtbt, a_row, a_col, expand, dskip, wnorm = ssm_w
    body = functools.partial(_ssd_sample_body, conv_w=wc.shape[0], n_valid=n_valid)
    consts = [wc, bc, dtb, dtbt, a_row, a_col, expand, dskip, wnorm]
    per = math.gcd(bs, SSD_SAMPLE_SEQS)

    def seq_spec(rows, cols):
        return pl.BlockSpec((per, rows, cols), lambda s: (s, 0, 0))

    return pl.pallas_call(
        body,
        grid=(bs // per,),
        in_specs=[seq_spec(ext_rows, conv_dim), seq_spec(lp, LANES), seq_spec(LANES, lp), seq_spec(lp, d_inner),
                  seq_spec(d_inner, n_state)]
                 + [_const_spec(w.shape) for w in consts],
        out_specs=[seq_spec(lp, d_inner), seq_spec(d_inner, n_state)],
        out_shape=[jax.ShapeDtypeStruct((bs, lp, d_inner), bf16),
                   jax.ShapeDtypeStruct((bs, d_inner, n_state), f32)],
        compiler_params=_params(1),
        name="ssd_sample",
    )(ext, dt, dtt, z, state, *consts)


def _sattn_body(pt_ref, q_ref, kn_ref, vn_ref, bprev_ref, bcur_ref, ck_ref, cv_ref, o_ref,
                km_ref, ms_ref, ls_ref, os_ref, kbuf_ref, vbuf_ref, sem_ref, *, nbs, n_full, n_tok, ppb):
    n_pages = nbs * ppb
    g = pl.program_id(1)
    n_steps = pl.num_programs(1)
    n_rows = N_HEADS * SUBLANES
    d = ATTN_DIM

    step = pl.program_id(0) * n_steps + g
    total = pl.num_programs(0) * n_steps

    def page_copies(n):
        slot = n % PAGE_SLOTS
        seq = n // n_steps
        first = (n % n_steps) * n_pages
        out = []
        for t in range(n_pages):
            pid = pt_ref[seq, first + t]
            out.append(pltpu.make_async_copy(ck_ref.at[pid], kbuf_ref.at[slot, t], sem_ref.at[slot, t]))
            out.append(pltpu.make_async_copy(cv_ref.at[pid], vbuf_ref.at[slot, t], sem_ref.at[slot, n_pages + t]))
        return out

    @pl.when(step == 0)
    def _():
        for n in range(PAGE_SLOTS - 1):
            @pl.when(n < total)
            def _():
                for c in page_copies(n):
                    c.start()

    @pl.when(step + PAGE_SLOTS - 1 < total)
    def _():
        for c in page_copies(step + PAGE_SLOTS - 1):
            c.start()

    for c in page_copies(step):
        c.wait()
    slot = step % PAGE_SLOTS
    kp = [kbuf_ref.at[slot, t] for t in range(n_pages)]
    vp = [vbuf_ref.at[slot, t] for t in range(n_pages)]

    q4 = q_ref[...]
    q8 = jnp.concatenate([q4, jnp.zeros((SUBLANES - n_tok, d), f32)], axis=0)
    r_i = lax.broadcasted_iota(jnp.int32, (n_rows, d), 0)
    c_i = lax.broadcasted_iota(jnp.int32, (n_rows, d), 1)
    head_mask = (r_i // SUBLANES) == (c_i // HEAD_DIM)
    q_rows = jnp.where(head_mask, jnp.concatenate([q8] * N_HEADS, axis=0), 0.0)
    qb = q_rows.astype(bf16)

    @pl.when(g == 0)
    def _():
        km_ref[...] = jnp.zeros(km_ref.shape, f32)
        ms_ref[...] = jnp.zeros(ms_ref.shape, f32)
        ls_ref[...] = jnp.zeros(ls_ref.shape, f32)

    lane_blk = lax.broadcasted_iota(jnp.int32, (1, LANES), 1)
    sc_new = jnp.zeros((n_rows, LANES), f32)
    ms_new = jnp.zeros((n_rows, LANES), f32)
    ls_new = jnp.zeros((n_rows, LANES), f32)
    o_new = []
    k_t = jnp.concatenate([kp[t][...].reshape(d, -1).astype(bf16) for t in range(n_pages)], axis=1)
    s_all = _dot(qb, k_t)
    for b in range(nbs):
        vblk_t = jnp.concatenate([vp[b * ppb + t][...].reshape(d, -1) for t in range(ppb)], axis=1)
        jj = g * nbs + b
        s = s_all[:, b * MOBA_BLOCK:(b + 1) * MOBA_BLOCK]
        here = (lane_blk == jj).astype(f32)
        sc_new = sc_new + (jnp.sum(s, axis=-1, keepdims=True) * (1.0 / MOBA_BLOCK)) * here
        s = s + jnp.where(jj == n_full - 1, 1.0, 0.0).astype(f32) * bprev_ref[...]
        m = jnp.max(s, axis=-1, keepdims=True)
        p = jnp.exp(s - m)
        l = jnp.sum(p, axis=-1, keepdims=True)
        ms_new = ms_new + m * here
        ls_new = ls_new + l * here
        o_new.append(_dot_nt(p.astype(bf16), vblk_t.astype(bf16)))
    km_ref[...] += sc_new
    ms_ref[...] += ms_new
    ls_ref[...] += ls_new
    os_ref[pl.ds(pl.multiple_of(g * nbs, nbs), nbs)] = jnp.stack(o_new, axis=0)

    @pl.when(g == pl.num_programs(1) - 1)
    def _():
        sc = km_ref[...]
        col = lax.broadcasted_iota(jnp.int32, sc.shape, 1)
        low = jnp.float32(-3e38)
        sc = jnp.where(col < n_full, sc, low)
        self_ = jnp.zeros(sc.shape, f32)
        for _ in range(min(MOBA_TOPK, n_full)):
            mx = jnp.max(sc, axis=-1, keepdims=True)
            idx = jnp.min(jnp.where(sc == mx, col, LANES), axis=-1, keepdims=True)
            self_ = jnp.where(col == idx, 1.0, self_)
            sc = jnp.where(col == idx, low, sc)
        sel = self_ > 0.5
        kn = jnp.concatenate([kn_ref[...], jnp.zeros((SAMPLE_ROWS - n_tok, d), f32)], axis=0).astype(bf16)
        vn = jnp.concatenate([vn_ref[...], jnp.zeros((SAMPLE_ROWS - n_tok, d), f32)], axis=0).astype(bf16)
        s_cur = _dot_nt(qb, kn) + bcur_ref[...]
        ms = ms_ref[...]
        m_tot = jnp.maximum(jnp.max(s_cur, axis=-1, keepdims=True),
                            jnp.max(jnp.where(sel, ms, low), axis=-1, keepdims=True))
        p_cur = jnp.exp(s_cur - m_tot)
        w = jnp.where(sel, jnp.exp(ms - m_tot), 0.0)
        l_tot = jnp.sum(p_cur, axis=-1, keepdims=True) + jnp.sum(w * ls_ref[...], axis=-1, keepdims=True)
        acc = _dot(p_cur.astype(bf16), vn)
        for j in range(n_full):
            acc = acc + w[:, j:j + 1] * os_ref[j]
        out = jnp.where(head_mask, acc / l_tot, 0.0)
        out8 = out[0:SUBLANES]
        for h in range(1, N_HEADS):
            out8 = out8 + out[h * SUBLANES:(h + 1) * SUBLANES]
        o_ref[...] = out8[0:n_tok]


def _attn_sample(q, kn, vn, ck, cv, page_table, bprev, bcur):
    bs, n_tok, d = q.shape
    n_pool, n_heads, hd, page = ck.shape
    assert n_heads * hd == d
    ppb = MOBA_BLOCK // page
    n_pages_seq = page_table.shape[1]
    past = n_pages_seq * page
    assert MOBA_BLOCK % page == 0 and past % MOBA_BLOCK == 0 and n_tok <= SUBLANES
    n_full = past // MOBA_BLOCK
    assert n_full <= LANES
    nbs = math.gcd(n_full, SAMPLE_BLOCKS_PER_STEP)
    n_steps = n_full // nbs
    n_pages = nbs * ppb
    n_rows = N_HEADS * SUBLANES

    seq_spec = pl.BlockSpec((None, n_tok, d), lambda s, g, pt: (s, 0, 0))
    body = functools.partial(_sattn_body, nbs=nbs, n_full=n_full, n_tok=n_tok, ppb=ppb)
    grid_spec = pltpu.PrefetchScalarGridSpec(
        num_scalar_prefetch=1,
        grid=(bs, n_steps),
        in_specs=[seq_spec, seq_spec, seq_spec,
                  pl.BlockSpec(bprev.shape, lambda s, g, pt: (0, 0)),
                  pl.BlockSpec(bcur.shape, lambda s, g, pt: (0, 0)),
                  pl.BlockSpec(memory_space=pl.ANY), pl.BlockSpec(memory_space=pl.ANY)],
        out_specs=seq_spec,
        scratch_shapes=[pltpu.VMEM((n_rows, LANES), f32),
                        pltpu.VMEM((n_rows, LANES), f32),
                        pltpu.VMEM((n_rows, LANES), f32),
                        pltpu.VMEM((n_full, n_rows, d), f32),
                        pltpu.VMEM((PAGE_SLOTS, n_pages, n_heads, hd, page), f32),
                        pltpu.VMEM((PAGE_SLOTS, n_pages, n_heads, hd, page), f32),
                        pltpu.SemaphoreType.DMA((PAGE_SLOTS, 2 * n_pages))],
    )
    return pl.pallas_call(
        body,
        grid_spec=grid_spec,
        out_shape=jax.ShapeDtypeStruct((bs, n_tok, d), f32),
        compiler_params=_params(2),
        name="moba_sample",
    )(page_table, q, kn, vn, bprev, bcur, ck, cv)


def _merge_body(at_ref, yn_ref, ga_ref, gb_ref, x_ref, g1_ref, wpa_ref, wps_ref, wo_ref, lg_ref, lb_ref,
                o_ref, *, alpha):
    pa = _dot(at_ref[...].astype(bf16), wpa_ref[...])
    ps = _dot(yn_ref[...].astype(bf16), wps_ref[...])
    merged = jax.nn.sigmoid(ga_ref[...].astype(f32)) * pa + jax.nn.sigmoid(gb_ref[...].astype(f32)) * ps
    mo = _dot(merged.astype(bf16), wo_ref[...])
    o_ref[...] = _layer_norm(alpha * x_ref[...] + g1_ref[...] * mo, lg_ref[...], lb_ref[...])


def _mod_spec(per_row_mod, tm, d, tiles_per_seq):
    if per_row_mod:
        return pl.BlockSpec((tm, d), lambda i: (i, 0))
    return pl.BlockSpec((None, 1, d), lambda i: (i // tiles_per_seq, 0, 0))


def _merge(attn, yn, ga, gb, x2d, gate1, wpa, wps, wo, lg, lb, *, per_row_mod, seq_len, alpha):
    r, d = x2d.shape
    tm = 256
    assert r % tm == 0
    tiles_per_seq = 1 if per_row_mod else seq_len // tm
    ms = _mod_spec(per_row_mod, tm, d, tiles_per_seq)

    def row_spec(w):
        return pl.BlockSpec((tm, w), lambda i: (i, 0))

    return pl.pallas_call(
        functools.partial(_merge_body, alpha=alpha),
        grid=(r // tm,),
        in_specs=[row_spec(attn.shape[1]), row_spec(yn.shape[1]), row_spec(d), row_spec(d), row_spec(d), ms,
                  _const_spec(wpa.shape), _const_spec(wps.shape), _const_spec(wo.shape),
                  _const_spec(lg.shape), _const_spec(lb.shape)],
        out_specs=row_spec(d),
        out_shape=jax.ShapeDtypeStruct((r, d), f32),
        compiler_params=_params(1),
        name="merge_ln1",
    )(attn, yn, ga, gb, x2d, gate1, wpa, wps, wo, lg, lb)


FFN_COL_CHUNKS = 1


def _ffn_body(x_ref, sh_ref, sc_ref, g2_ref, wu_ref, wc_ref, bc_ref, wd_ref, lg_ref, lb_ref, *rest,
              alpha, conv_w, tiles_per_seq, sample_len):
    if sample_len is None:
        y_ref, tail_ref, buf_ref, carry_ref = rest
    else:
        p_refs = rest[:conv_w - 1]
        y_ref, hup_ref, buf_ref, carry_ref = rest[conv_w - 1:]
    x = x_ref[...]
    tm = x.shape[0]
    ff = wd_ref.shape[0]
    cw = ff // FFN_COL_CHUNKS
    t = pl.program_id(0) % tiles_per_seq

    @pl.when(t == 0)
    def _():
        carry_ref[...] = jnp.zeros(carry_ref.shape, f32)

    u = (x * (1.0 + sc_ref[...]) + sh_ref[...]).astype(bf16)
    if sample_len is not None:
        tmod = lax.broadcasted_iota(jnp.int32, (tm, cw), 0) % sample_len
    f = jnp.zeros((tm, x.shape[1]), f32)
    for c in range(FFN_COL_CHUNKS):
        halves = []
        for half in range(2):
            c0 = half * ff + c * cw
            buf_ref[0:SUBLANES, :] = carry_ref[:, c0:c0 + cw]
            hup = _dot(u, wu_ref[:, c0:c0 + cw])
            buf_ref[SUBLANES:SUBLANES + tm, :] = hup
            if sample_len is not None:
                hup_ref[:, c0:c0 + cw] = hup
            hc = bc_ref[:, c0:c0 + cw] + wc_ref[conv_w - 1:conv_w, c0:c0 + cw] * hup
            for k in range(conv_w - 1):
                back = conv_w - 1 - k
                prev = buf_ref[SUBLANES - back:SUBLANES - back + tm, :]
                if sample_len is not None:
                    prev = jnp.where(tmod >= back, prev, p_refs[back - 1][:, c0:c0 + cw])
                hc = hc + wc_ref[k:k + 1, c0:c0 + cw] * prev
            carry_ref[:, c0:c0 + cw] = buf_ref[tm:tm + SUBLANES, :]
            halves.append(hc)
        gact = (_silu(halves[0]) * halves[1]).astype(bf16)
        f = f + _dot(gact, wd_ref[c * cw:(c + 1) * cw, :])
    y_ref[...] = _layer_norm(alpha * x + g2_ref[...] * f, lg_ref[...], lb_ref[...])
    if sample_len is None:
        @pl.when(t == tiles_per_seq - 1)
        def _():
            tail_ref[...] = carry_ref[...]


def _ffn(x2d, shift, scale, gate, wu, wc, bc, wd, lg, lb, prevs, *, per_row_mod, seq_len, bn, alpha,
         sample_len):
    r, d = x2d.shape
    ff2 = wu.shape[1]
    ff = wd.shape[0]
    conv_w = wc.shape[0]
    tm = 256 if sample_len is None else 128
    assert r % tm == 0 and ff % (FFN_COL_CHUNKS * LANES) == 0 and conv_w - 1 <= SUBLANES
    tiles_per_seq = 1 if per_row_mod else seq_len // tm
    ms = _mod_spec(per_row_mod, tm, d, tiles_per_seq)

    def row_spec(w):
        return pl.BlockSpec((tm, w), lambda i: (i, 0))

    in_specs = [row_spec(d), ms, ms, ms, _const_spec(wu.shape), _const_spec(wc.shape), _const_spec(bc.shape),
                _const_spec(wd.shape), _const_spec(lg.shape), _const_spec(lb.shape)]
    args = [x2d, shift, scale, gate, wu, wc, bc, wd, lg, lb]
    if sample_len is None:
        out_shape = [jax.ShapeDtypeStruct((r, d), f32), jax.ShapeDtypeStruct((bn, SUBLANES, ff2), f32)]
        out_specs = [row_spec(d), pl.BlockSpec((None, SUBLANES, ff2), lambda i: (i // tiles_per_seq, 0, 0))]
    else:
        assert tm % sample_len == 0 and len(prevs) == conv_w - 1
        in_specs += [row_spec(ff2)] * len(prevs)
        args += list(prevs)
        out_shape = [jax.ShapeDtypeStruct((r, d), f32), jax.ShapeDtypeStruct((r, ff2), f32)]
        out_specs = [row_spec(d), row_spec(ff2)]
    body = functools.partial(_ffn_body, alpha=alpha, conv_w=conv_w, tiles_per_seq=tiles_per_seq,
                             sample_len=sample_len)
    return pl.pallas_call(
        body,
        grid=(r // tm,),
        in_specs=in_specs,
        out_specs=out_specs,
        out_shape=out_shape,
        scratch_shapes=[pltpu.VMEM((tm + SUBLANES, ff // FFN_COL_CHUNKS), f32),
                        pltpu.VMEM((SUBLANES, ff2), f32)],
        compiler_params=_params(1),
        name="conv_ffn",
    )(*args)


def _rel_bucket(dist):
    n = jnp.maximum(dist, 0)
    max_exact = REL_BUCKETS // 2
    nf = jnp.maximum(n, 1).astype(f32)
    large = max_exact + (jnp.log(nf / max_exact) / math.log(REL_MAX_DIST / max_exact)
                         * (REL_BUCKETS - max_exact)).astype(jnp.int32)
    large = jnp.minimum(large, REL_BUCKETS - 1)
    return jnp.where(n < max_exact, n, large)


def _rel_bias(rel_table, dist):
    rel = rel_table - rel_table[REL_BUCKETS - 1]
    onehot = (_rel_bucket(dist)[..., None] == jnp.arange(REL_BUCKETS)).astype(f32)
    b = jnp.dot(onehot, rel, precision=lax.Precision.HIGHEST)
    return jnp.moveaxis(b, -1, 0)


def _prompt_bias_tables(rel_table):
    n = MOBA_BLOCK
    by_dist = _rel_bias(rel_table, jnp.arange(2 * n, dtype=jnp.int32)) * LOG2E
    masked = jnp.full((by_dist.shape[0], n - 1), MASKED, f32)
    bd = _toeplitz(jnp.concatenate([masked, by_dist[:, :n]], axis=1), n)
    bp = _toeplitz(by_dist[:, 1:], n)
    return bd.astype(f32), bp.astype(f32)


def _toeplitz(f, n):
    g = jnp.concatenate([f, jnp.zeros((f.shape[0], 1), f.dtype)], axis=1)
    y = jnp.tile(g, (1, n))[:, :n * (2 * n - 1)].reshape(f.shape[0], n, 2 * n - 1)
    return y[:, :, n - 1:]


def _sample_bias_tables(rel_table, n_tok):
    n_rows = N_HEADS * SUBLANES
    t = jnp.arange(SUBLANES, dtype=jnp.int32)
    a = jnp.arange(MOBA_BLOCK, dtype=jnp.int32)
    bprev = _rel_bias(rel_table, MOBA_BLOCK + t[:, None] - a[None, :])
    bprev = jnp.where((t < n_tok)[None, :, None], bprev, 0.0).reshape(n_rows, MOBA_BLOCK)
    tk = jnp.arange(SAMPLE_ROWS, dtype=jnp.int32)
    dist = t[:, None] - tk[None, :]
    ok = (dist >= 0) & (t[:, None] < n_tok) & (tk[None, :] < n_tok)
    bcur = jnp.where(ok[None], _rel_bias(rel_table, dist), MASKED).reshape(n_rows, SAMPLE_ROWS)
    return bprev.astype(f32), bcur.astype(f32)


def _pad_cols(w, n):
    return jnp.pad(w, ((0, 0), (0, n - w.shape[1])))


def kernel(x_prompt, x_sample, cache_k, cache_v, page_table, state_ssm, state_conv_ssm, state_conv_ffn,
           c_prompt, c_sample, rel_table, w_ada, b_ada, w_in, w_conv_ssm, b_conv_ssm, dt_bias, a_log,
           d_skip, w_norm_ssm, w_proj_attn, w_proj_ssm, w_out, ln1_g, ln1_b, w_up, w_conv_ffn, b_conv_ffn,
           w_down, ln2_g, ln2_b):
    depth = w_ada.shape[0]
    alpha = (2 * depth) ** 0.25
    bp_, seq, d = x_prompt.shape
    bs, n_tok, _ = x_sample.shape
    ssm_heads = dt_bias.shape[1]
    d_inner = ssm_heads * SSM_HEAD_DIM
    n_state = state_ssm.shape[-1]
    conv_dim = w_conv_ssm.shape[-1]
    ff2 = w_up.shape[-1]
    ssm_conv = w_conv_ssm.shape[1]
    ffn_conv = w_conv_ffn.shape[1]
    assert ssm_heads <= LANES and n_tok <= SUBLANES
    page = cache_k.shape[2]
    scale = HEAD_DIM ** -0.5

    bd, bpv = _prompt_bias_tables(rel_table)
    bprev_s, bcur_s = _sample_bias_tables(rel_table, n_tok)
    expand = (jnp.arange(LANES)[:, None] == (jnp.arange(d_inner)[None, :] // SSM_HEAD_DIM)).astype(bf16)

    yp = x_prompt.reshape(bp_ * seq, d)
    ys = x_sample.reshape(bs * n_tok, d)
    outs_p = [[] for _ in range(5)]
    outs_s = [[] for _ in range(5)]
    for l in range(depth):
        cuts = [ATTN_DIM, 2 * ATTN_DIM, 3 * ATTN_DIM, 3 * ATTN_DIM + d_inner,
                3 * ATTN_DIM + d_inner + conv_dim, 3 * ATTN_DIM + d_inner + conv_dim + ssm_heads,
                3 * ATTN_DIM + d_inner + conv_dim + ssm_heads + d]
        wq, wk, wv, wz, wxbc, wdt, wga, wgb = jnp.split(w_in[l], cuts, axis=1)
        wdt = _pad_cols(wdt, LANES)
        wq = wq * scale
        nat_rest = [(d_inner, bf16), (conv_dim, f32), (LANES, f32), (d, bf16), (d, bf16)]
        w_rest = [wz, wxbc, wdt, wga, wgb]
        nat_p = [(ATTN_DIM, None)] + nat_rest
        wn_p = jnp.concatenate([wk] + w_rest, axis=1).astype(bf16)
        a_ = ATTN_DIM
        tr_p = [(0, a_, bf16, 0, False), (a_, a_, f32, 0, True), (2 * a_, a_, f32, 0, True),
                (2 * a_, a_, bf16, BF16_ROWS, False), (3 * a_, LANES, f32, 0, False)]
        wt_p = jnp.concatenate([wq * LOG2E, wk, wv, wdt], axis=1).T.astype(bf16)
        nat_s = [(ATTN_DIM, f32)] * 3 + nat_rest
        wn_s = jnp.concatenate([wq, wk, wv] + w_rest, axis=1).astype(bf16)
        tr_s = [(0, LANES, f32, 0, False)]
        wt_s = wdt.T.astype(bf16)
        ssm_w = (w_conv_ssm[l], b_conv_ssm[l].reshape(1, conv_dim),
                 _pad_cols(dt_bias[l].reshape(1, -1), LANES), _pad_cols(dt_bias[l].reshape(1, -1), LANES).T,
                 -jnp.exp(_pad_cols(a_log[l].reshape(1, -1), LANES)),
                 -jnp.exp(_pad_cols(a_log[l].reshape(1, -1), LANES)).T,
                 expand, jnp.repeat(d_skip[l], SSM_HEAD_DIM).reshape(1, d_inner),
                 w_norm_ssm[l].reshape(1, d_inner))
        wpa, wps, wo = (w_proj_attn[l].astype(bf16), w_proj_ssm[l].astype(bf16), w_out[l].astype(bf16))
        lg1, lb1 = ln1_g[l].reshape(1, d), ln1_b[l].reshape(1, d)
        lg2, lb2 = ln2_g[l].reshape(1, d), ln2_b[l].reshape(1, d)
        wu, wd = w_up[l].astype(bf16), w_down[l].astype(bf16)
        wcf, bcf = w_conv_ffn[l], b_conv_ffn[l].reshape(1, ff2)

        mod = _ada(jnp.concatenate([c_prompt, c_sample], axis=0), w_ada[l], b_ada[l])
        mod_p = mod[:bp_].reshape(bp_, 6, 1, d)
        mod_s = jnp.repeat(mod[bp_:].reshape(bs, 6, 1, d), n_tok, axis=2).reshape(bs, 6, n_tok, d)
        mod_s = jnp.moveaxis(mod_s, 1, 0).reshape(6, bs * n_tok, d)
        sh1p, sc1p, g1p, sh2p, sc2p, g2p = (mod_p[:, i] for i in range(6))
        sh1s, sc1s, g1s, sh2s, sc2s, g2s = (mod_s[i] for i in range(6))

        z_p, xbc_p, dt_p, ga_p, gb_p, qT_p, kT_p, vTf_p, vT_p, dtT_p, k2_p = _inproj(
            yp, sh1p, sc1p, wn_p, wt_p, nat_p, tr_p, per_row_mod=False, seq_len=seq, attn_layout=True)
        attn_p = _attn_prompt(k2_p, qT_p, vT_p, bd, bpv, bn=bp_, seq_len=seq)
        yn_p, st_p = _ssd_prompt(xbc_p, dt_p, dtT_p, z_p, ssm_w, bn=bp_, seq_len=seq, n_state=n_state)
        x1_p = _merge(attn_p, yn_p, ga_p, gb_p, yp, g1p, wpa, wps, wo, lg1, lb1,
                      per_row_mod=False, seq_len=seq, alpha=alpha)
        yp, tail_p = _ffn(x1_p, sh2p, sc2p, g2p, wu, wcf, bcf, wd, lg2, lb2, (),
                          per_row_mod=False, seq_len=seq, bn=bp_, alpha=alpha, sample_len=None)
        outs_p[0].append(jnp.transpose(kT_p.reshape(bp_, N_HEADS, HEAD_DIM, seq), (0, 3, 1, 2)))
        outs_p[1].append(jnp.transpose(vTf_p.reshape(bp_, N_HEADS, HEAD_DIM, seq), (0, 3, 1, 2)))
        outs_p[2].append(st_p.reshape(bp_, ssm_heads, SSM_HEAD_DIM, n_state))
        outs_p[3].append(xbc_p.reshape(bp_, seq, conv_dim)[:, seq - (ssm_conv - 1):])
        outs_p[4].append(tail_p[:, SUBLANES - (ffn_conv - 1):])

        r_s = bs * n_tok
        r_pad = -(-r_s // 256) * 256

        def pad_rows(a):
            return jnp.pad(a, ((0, r_pad - r_s), (0, 0)))

        q_s, k_s, v_s, z_s, xbc_s, dt_s, ga_s, gb_s, dtT_s = _inproj(
            pad_rows(ys), pad_rows(sh1s), pad_rows(sc1s), wn_s, wt_s, nat_s, tr_s,
            per_row_mod=True, seq_len=None, attn_layout=False)
        q_s, k_s, v_s = (a[:r_s].reshape(bs, n_tok, ATTN_DIM) for a in (q_s, k_s, v_s))
        attn_s = _attn_sample(q_s, k_s, v_s, jnp.transpose(cache_k[l], (0, 2, 3, 1)),
                              jnp.transpose(cache_v[l], (0, 2, 3, 1)), page_table, bprev_s, bcur_s)
        row_pad = SAMPLE_ROWS - n_tok
        xbc_s3 = xbc_s[:r_s].reshape(bs, n_tok, conv_dim)
        ext = jnp.concatenate([state_conv_ssm[l], xbc_s3,
                               jnp.zeros((bs, row_pad + SUBLANES - (ssm_conv - 1), conv_dim), f32)], axis=1)
        dt_s3 = jnp.pad(dt_s[:r_s].reshape(bs, n_tok, LANES), ((0, 0), (0, row_pad), (0, 0)))
        dtT_s3 = jnp.pad(jnp.moveaxis(dtT_s[:, :r_s].reshape(LANES, bs, n_tok), 0, 1),
                         ((0, 0), (0, 0), (0, row_pad)))
        z_s3 = jnp.pad(z_s[:r_s].reshape(bs, n_tok, d_inner), ((0, 0), (0, row_pad), (0, 0)))
        yn_s3, st_s = _ssd_sample(ext, dt_s3, dtT_s3, z_s3, state_ssm[l].reshape(bs, d_inner, n_state),
                                  ssm_w, n_valid=n_tok)
        yn_s = yn_s3[:, :n_tok].reshape(r_s, d_inner)
        x1_s = _merge(pad_rows(attn_s.reshape(r_s, ATTN_DIM)), pad_rows(yn_s), ga_s, gb_s, pad_rows(ys),
                      pad_rows(g1s), wpa, wps, wo, lg1, lb1, per_row_mod=True, seq_len=None, alpha=alpha)
        cf = state_conv_ffn[l]
        prevs = []
        for back in range(1, ffn_conv):
            rows = [cf[:, ffn_conv - 1 - back + t] if t < back else jnp.zeros((bs, ff2), f32)
                    for t in range(n_tok)]
            prevs.append(pad_rows(jnp.stack(rows, axis=1).reshape(r_s, ff2)))
        y_s, hup_s = _ffn(x1_s, pad_rows(sh2s), pad_rows(sc2s), pad_rows(g2s), wu, wcf, bcf, wd, lg2, lb2,
                          prevs, per_row_mod=True, seq_len=None, bn=bs, alpha=alpha, sample_len=n_tok)
        ys = y_s[:r_s]
        outs_s[0].append(k_s.reshape(bs, n_tok, N_HEADS, HEAD_DIM))
        outs_s[1].append(v_s.reshape(bs, n_tok, N_HEADS, HEAD_DIM))
        outs_s[2].append(st_s.reshape(bs, ssm_heads, SSM_HEAD_DIM, n_state))
        cs_ext = jnp.concatenate([state_conv_ssm[l], xbc_s3], axis=1)
        outs_s[3].append(cs_ext[:, -(ssm_conv - 1):])
        cf_ext = jnp.concatenate([cf, hup_s[:r_s].reshape(bs, n_tok, ff2)], axis=1)
        outs_s[4].append(cf_ext[:, -(ffn_conv - 1):])

    return (yp.reshape(bp_, seq, d), ys.reshape(bs, n_tok, d),
            *(jnp.stack(o) for o in outs_p), *(jnp.stack(o) for o in outs_s))
```

```python
import functools
import math

import jax
import jax.numpy as jnp
from jax import lax
from jax.experimental import pallas as pl
from jax.experimental.pallas import tpu as pltpu

N_HEADS = 8
HEAD_DIM = 64
ATTN_DIM = N_HEADS * HEAD_DIM
MOBA_BLOCK = 256
MOBA_TOPK = 3
REL_BUCKETS = 32
REL_MAX_DIST = 128
SSM_HEAD_DIM = 64
SSM_CHUNK = 256
EPS = 1e-5

LANES = 128
SUBLANES = 8
BF16_ROWS = 16
VMEM_LIMIT = 56 * 1024 * 1024

MASKED = -1e30
SAMPLE_ROWS = 16
SAMPLE_BLOCKS_PER_STEP = 4
PAGE_SLOTS = 3
SSD_SAMPLE_SEQS = 2
FAR_GROUP = 4
LOG2E = math.log2(math.e)

f32 = jnp.float32
bf16 = jnp.bfloat16

_NT = (((1,), (1,)), ((), ()))
_TN = (((0,), (0,)), ((), ()))


def _dot(a, b):
    return jnp.dot(a, b, preferred_element_type=f32)


def _dot_nt(a, b):
    return lax.dot_general(a, b, _NT, preferred_element_type=f32)


def _dot_tn(a, b):
    return lax.dot_general(a, b, _TN, preferred_element_type=f32)


def _split3(x):
    hi = x.astype(bf16)
    r = x - hi.astype(f32)
    mid = r.astype(bf16)
    lo = (r - mid.astype(f32)).astype(bf16)
    return hi, mid, lo


def _dot_exact_rhs(a, m):
    return _dot(jnp.concatenate(_split3(a), axis=1), jnp.concatenate([m, m, m], axis=0))


def _dot_exact_lhs(m, a):
    return _dot(jnp.concatenate([m, m, m], axis=1), jnp.concatenate(_split3(a), axis=0))


def _rows_back(cur, before, back):
    rows, cols = cur.shape
    sub = lax.broadcasted_iota(jnp.int32, cur.shape, 0) % SUBLANES
    mixed = jnp.where(sub >= SUBLANES - back, before, cur)
    return pltpu.roll(mixed.reshape(rows // SUBLANES, SUBLANES, cols), back, axis=1).reshape(rows, cols)


def _silu(x):
    return x * jax.nn.sigmoid(x)


def _softplus(x):
    return jnp.maximum(x, 0.0) + jnp.log1p(jnp.exp(-jnp.abs(x)))


def _layer_norm(x, g, b):
    mu = jnp.mean(x, axis=-1, keepdims=True)
    xc = x - mu
    var = jnp.mean(xc * xc, axis=-1, keepdims=True)
    return xc * lax.rsqrt(var + EPS) * g + b


def _const_spec(shape):
    nd = len(shape)
    return pl.BlockSpec(shape, lambda *_: (0,) * nd, pipeline_mode=pl.Buffered(1))


def _params(n_grid, flags=None):
    return pltpu.CompilerParams(
        dimension_semantics=("arbitrary",) * n_grid, vmem_limit_bytes=VMEM_LIMIT, flags=flags)


def _ada_body(c_ref, w_ref, b_ref, o_ref):
    a = _silu(c_ref[...])
    hi, mid, lo = _split3(a)
    whi, wmid, wlo = _split3(w_ref[...])
    acc = _dot(hi, whi) + (_dot(hi, wmid) + _dot(mid, whi))
    acc = acc + (_dot(hi, wlo) + _dot(mid, wmid) + _dot(lo, whi))
    o_ref[...] = acc + b_ref[...]


def _ada(c, w, b):
    n, d = c.shape
    dn = w.shape[1]
    tn = 1024 if dn % 1024 == 0 else dn
    return pl.pallas_call(
        _ada_body,
        grid=(dn // tn,),
        in_specs=[pl.BlockSpec((n, d), lambda j: (0, 0)),
                  pl.BlockSpec((d, tn), lambda j: (0, j)),
                  pl.BlockSpec((1, tn), lambda j: (0, j))],
        out_specs=pl.BlockSpec((n, tn), lambda j: (0, j)),
        out_shape=jax.ShapeDtypeStruct((n, dn), f32),
        compiler_params=_params(1),
        name="ada_mod",
    )(c, w, b.reshape(1, dn))


def _inproj_body(x_ref, sh_ref, sc_ref, wn_ref, wt_ref, *outs, nat, tr, k_col, tiles_per_seq):
    u = (x_ref[...] * (1.0 + sc_ref[...]) + sh_ref[...]).astype(bf16)
    tm = u.shape[0]
    o = 0
    col = 0
    for width, dt in nat:
        res = _dot(u, wn_ref[:, col:col + width])
        if k_col is not None and col == k_col:
            kf = res
        if dt is not None:
            outs[o][...] = res.astype(dt)
            o += 1
        col += width
    if tr:
        t = _dot_nt(wt_ref[...], u)
        for row, height, _, ones_rows, _ in tr:
            if ones_rows:
                step = HEAD_DIM + ones_rows
                for h in range(height // HEAD_DIM):
                    outs[o][h * step:h * step + HEAD_DIM, :] = (
                        t[row + h * HEAD_DIM:row + (h + 1) * HEAD_DIM].astype(outs[o].dtype))
                    outs[o][h * step + HEAD_DIM:(h + 1) * step, :] = jnp.ones((ones_rows, tm), outs[o].dtype)
            else:
                outs[o][...] = t[row:row + height].astype(outs[o].dtype)
            o += 1
    if k_col is not None:
        pos = (pl.program_id(0) % tiles_per_seq) * tm
        blk = pos // MOBA_BLOCK
        onehot = (lax.broadcasted_iota(jnp.int32, (tm, LANES), 1) == blk).astype(bf16)
        k2_ref = outs[o]
        for p in range(ATTN_DIM // LANES):
            k2_ref[p] = jnp.concatenate([kf[:, p * LANES:(p + 1) * LANES].astype(bf16), onehot], axis=1)


def _inproj(x2d, shift, scale, wn, wt, nat, tr, *, per_row_mod, seq_len, attn_layout):
    r, d = x2d.shape
    tm = 256
    assert r % tm == 0
    n_tiles = r // tm
    if per_row_mod:
        mod_spec = pl.BlockSpec((tm, d), lambda i: (i, 0))
        tiles_per_seq = 1
    else:
        assert seq_len % tm == 0 and MOBA_BLOCK % tm == 0
        tiles_per_seq = seq_len // tm
        mod_spec = pl.BlockSpec((None, 1, d), lambda i: (i // tiles_per_seq, 0, 0))
    out_shape, out_specs = [], []
    for width, dt in nat:
        if dt is not None:
            out_shape.append(jax.ShapeDtypeStruct((r, width), dt))
            out_specs.append(pl.BlockSpec((tm, width), lambda i: (i, 0)))
    for _, height, dt, ones_rows, per_seq in tr:
        rows = height + (height // HEAD_DIM) * ones_rows
        if per_seq:
            out_shape.append(jax.ShapeDtypeStruct((r // seq_len, rows, seq_len), dt))
            out_specs.append(pl.BlockSpec((None, rows, tm), lambda i: (i // tiles_per_seq, 0, i % tiles_per_seq)))
        else:
            out_shape.append(jax.ShapeDtypeStruct((rows, r), dt))
            out_specs.append(pl.BlockSpec((rows, tm), lambda i: (0, i)))
    k_col = None
    if attn_layout:
        k_col = 0
        n_pairs = ATTN_DIM // LANES
        out_shape.append(jax.ShapeDtypeStruct((n_pairs, r, 2 * LANES), bf16))
        out_specs.append(pl.BlockSpec((n_pairs, tm, 2 * LANES), lambda i: (0, i, 0)))
    body = functools.partial(_inproj_body, nat=tuple(nat), tr=tuple(tr), k_col=k_col,
                             tiles_per_seq=tiles_per_seq)
    return pl.pallas_call(
        body,
        grid=(n_tiles,),
        in_specs=[pl.BlockSpec((tm, d), lambda i: (i, 0)), mod_spec, mod_spec,
                  _const_spec(wn.shape), _const_spec(wt.shape)],
        out_specs=out_specs,
        out_shape=out_shape,
        compiler_params=_params(1),
        name="in_proj",
    )(x2d, shift, scale, wn, wt)


def _select_topk_rows(s, n_valid_rows):
    nblk = s.shape[0]
    row = lax.broadcasted_iota(jnp.int32, s.shape, 0)
    low = jnp.float32(-3e38)
    s = jnp.where(row < n_valid_rows, s, low)
    sel = jnp.zeros(s.shape, jnp.bool_)
    for _ in range(MOBA_TOPK):
        m = jnp.max(s, axis=0, keepdims=True)
        idx = jnp.min(jnp.where(s == m, row, nblk), axis=0, keepdims=True)
        hit = (row == idx) & (m > low)
        sel = sel | hit
        s = jnp.where(row == idx, low, s)
    return sel


def _attn_body(k2_ref, qT_ref, vT_ref, bd_ref, bp_ref, o_ref, km_ref, q2n_ref, q2f_ref, sa_ref, sb_ref, *,
               n_blocks, group):
    i = pl.program_id(2)
    blk = MOBA_BLOCK
    hd = HEAD_DIM
    nsel = km_ref.shape[0]

    @pl.when(i == 0)
    def _():
        km_ref[...] = jnp.zeros(km_ref.shape, f32)

        def mean_body(j, c):
            kk = k2_ref[pl.ds(pl.multiple_of(j * blk, blk), blk), :].astype(f32)
            km_ref[pl.ds(j, 1), :] = jnp.sum(kk, axis=0, keepdims=True) * (1.0 / blk)
            return c
        lax.fori_loop(0, n_blocks, mean_body, 0)

    q_pair = qT_ref[:, pl.ds(pl.multiple_of(i * blk, blk), blk)]
    zq = jnp.zeros((hd, blk), bf16)
    q_rows = jnp.concatenate([jnp.concatenate([q_pair[0:hd], zq], axis=1),
                              jnp.concatenate([zq, q_pair[hd:2 * hd]], axis=1)], axis=0)
    zrest = jnp.zeros((q2f_ref.shape[0] - 2 * hd, 2 * blk), bf16)
    q2f_ref[0:2 * hd, :] = q_rows
    q2f_ref[2 * hd:, :] = zrest
    q2n_ref[0:2 * hd, :] = q_rows
    q2n_ref[2 * hd:, :] = zrest
    km = km_ref[...]
    km_hi = km.astype(bf16)
    km_lo = (km - km_hi.astype(f32)).astype(bf16)
    q2 = q2f_ref[...]
    s_blk = _dot(km_hi, q2) + _dot(km_lo, q2)
    sel = _select_topk_rows(s_blk, i)
    rowsel = lax.broadcasted_iota(jnp.int32, (nsel, 2 * blk), 0)
    far = sel & (rowsel < i - 1)
    near = (sel & (rowsel == i - 1)) | (rowsel == i)
    q2f_ref[2 * hd:2 * hd + nsel, :] = jnp.where(far, 0.0, MASKED).astype(bf16)
    q2n_ref[2 * hd:2 * hd + nsel, :] = jnp.where(near, 0.0, MASKED).astype(bf16)

    def k_rows(j, n):
        return k2_ref[pl.ds(pl.multiple_of(j * blk, blk), n * blk), :]

    vrows = vT_ref.shape[0] // 2

    def v_cols(h, j, n):
        return vT_ref[h * vrows:(h + 1) * vrows, pl.ds(pl.multiple_of(j * blk, blk), n * blk)]

    jp = jnp.maximum(i - 1, 0)
    first_pad = jnp.where(i > 0, 0.0, MASKED).astype(f32)
    s_cur = _dot(k_rows(i, 1), q2n_ref[...])
    s_prev = _dot(k_rows(jp, 1), q2n_ref[...])
    carry = []
    for h in range(2):
        s = jnp.concatenate([s_cur[:, h * blk:(h + 1) * blk] + bd_ref[h],
                             s_prev[:, h * blk:(h + 1) * blk] + (bp_ref[h] + first_pad)], axis=0)
        m = jnp.max(s, axis=0, keepdims=True)
        p = jnp.exp2((s - m).astype(bf16))
        v = jnp.concatenate([v_cols(h, i, 1), v_cols(h, jp, 1)], axis=1)
        carry += [m, _dot(v, p)]

    last_group = n_blocks // group - 1

    def logits_into(s_ref, g):
        s_ref[...] = _dot(k_rows(jnp.minimum(g, last_group) * group, group), q2f_ref[...])

    def consume(s_ref, g, carry):
        j0 = jnp.minimum(g, last_group) * group
        pad = jnp.where(g < n_groups, 0.0, MASKED).astype(f32)
        out = []
        for h in range(2):
            m, acc = carry[2 * h:2 * h + 2]
            m2 = jnp.maximum(m, jnp.max(s_ref[:, h * blk:(h + 1) * blk], axis=0, keepdims=True) + pad)
            a = jnp.exp2(m - m2)
            p = jnp.exp2((s_ref[:, h * blk:(h + 1) * blk] - (m2 - pad)).astype(bf16))
            out += [m2, a * acc + _dot(v_cols(h, j0, group), p)]
        return out

    def far_body(t, carry):
        logits_into(sb_ref, 2 * t + 1)
        carry = consume(sa_ref, 2 * t, list(carry))
        logits_into(sa_ref, 2 * t + 2)
        return tuple(consume(sb_ref, 2 * t + 1, carry))

    n_groups = (jnp.maximum(i - 1, 0) + group - 1) // group
    logits_into(sa_ref, 0)
    carry = lax.fori_loop(0, (n_groups + 1) // 2, far_body, tuple(carry))
    outs = [carry[2 * h + 1][0:hd] / carry[2 * h + 1][hd:hd + 1] for h in range(2)]
    o_ref[...] = jnp.concatenate(outs, axis=0).T.astype(o_ref.dtype)


def _attn_prompt(k2, qT, vT, bd, bp, *, bn, seq_len):
    n_pairs, r, _ = k2.shape
    blk = MOBA_BLOCK
    assert seq_len % blk == 0
    nb = seq_len // blk
    nsel = HEAD_DIM
    assert nb <= nsel
    group = math.gcd(nb, FAR_GROUP)
    body = functools.partial(_attn_body, n_blocks=nb, group=group)
    return pl.pallas_call(
        body,
        grid=(bn, n_pairs, nb),
        in_specs=[pl.BlockSpec((None, seq_len, 2 * LANES), lambda b, p, i: (p, b, 0)),
                  pl.BlockSpec((LANES, seq_len), lambda b, p, i: (p, b)),
                  pl.BlockSpec((vT.shape[0] // n_pairs, seq_len), lambda b, p, i: (p, b)),
                  pl.BlockSpec((2, blk, blk), lambda b, p, i: (p, 0, 0)),
                  pl.BlockSpec((2, blk, blk), lambda b, p, i: (p, 0, 0))],
        out_specs=pl.BlockSpec((blk, LANES), lambda b, p, i: (b * nb + i, p)),
        out_shape=jax.ShapeDtypeStruct((r, ATTN_DIM), bf16),
        scratch_shapes=[pltpu.VMEM((nsel, 2 * LANES), f32),
                        pltpu.VMEM((2 * LANES, 2 * blk), bf16),
                        pltpu.VMEM((2 * LANES, 2 * blk), bf16),
                        pltpu.VMEM((group * blk, 2 * blk), f32),
                        pltpu.VMEM((group * blk, 2 * blk), f32)],
        compiler_params=_params(3),
        name="moba_prompt",
    )(k2, qT, vT, bd, bp)


def _ssd_chunk(conv, dt_raw, dt_raw_t, z, st_ref, dtb, dtb_t, a_row, a_col, expand, dskip, wnorm,
               *, n_valid):
    l = conv.shape[0]
    d_inner = z.shape[1]
    n_state = st_ref.shape[1]
    n_heads = d_inner // SSM_HEAD_DIM
    n_groups = (conv.shape[1] - d_inner) // (2 * n_state)
    hpg = n_heads // n_groups
    gw = hpg * SSM_HEAD_DIM
    last = (l if n_valid is None else n_valid) - 1

    act = _silu(conv)
    xs = act[:, :d_inner]
    bm = act[:, d_inner:d_inner + n_groups * n_state]
    cm = act[:, d_inner + n_groups * n_state:]

    dt = _softplus(dt_raw + dtb)
    dt_t = _softplus(dt_raw_t + dtb_t)
    r_i = lax.broadcasted_iota(jnp.int32, (l, l), 0)
    c_i = lax.broadcasted_iota(jnp.int32, (l, l), 1)
    causal = r_i >= c_i
    tril = causal.astype(bf16)
    triu = (r_i <= c_i).astype(bf16)
    acs = _dot_exact_lhs(tril, dt * a_row)
    acs_t = _dot_exact_rhs(dt_t * a_col, triu)
    eacs = jnp.exp(acs)
    dec = jnp.exp(acs[last:last + 1, :] - acs)
    ea_t = jnp.exp(acs_t)

    dt_full = _dot_exact_rhs(dt, expand)
    eacs_full = _dot_exact_rhs(eacs, expand)
    dec_full = _dot_exact_rhs(dec, expand)
    xd = xs * dt_full
    xdd = xd * dec_full
    if n_valid is not None:
        rows = lax.broadcasted_iota(jnp.int32, xdd.shape, 0)
        xdd = jnp.where(rows < n_valid, xdd, 0.0)

    lane = lax.broadcasted_iota(jnp.int32, (l, 2 * SSM_HEAD_DIM), 1)
    y_parts = []
    for g in range(n_groups):
        bg = bm[:, g * n_state:(g + 1) * n_state].astype(bf16)
        cg = cm[:, g * n_state:(g + 1) * n_state].astype(bf16)
        cb = _dot_nt(cg, bg)
        st_g = st_ref[g * gw:(g + 1) * gw, :]
        y_inter = _dot_nt(cg, st_g.astype(bf16)) * eacs_full[:, g * gw:(g + 1) * gw]
        pair_parts = []
        for q in range(hpg // 2):
            h0 = g * hpg + 2 * q
            xdp = xd[:, h0 * SSM_HEAD_DIM:(h0 + 2) * SSM_HEAD_DIM].astype(bf16)
            res = []
            for h in (h0, h0 + 1):
                seg = acs[:, h:h + 1] - acs_t[h:h + 1, :]
                lm = jnp.exp(jnp.where(causal, seg, MASKED))
                res.append(_dot((cb * lm).astype(bf16), xdp))
            pair_parts.append(jnp.where(lane < SSM_HEAD_DIM, res[0], res[1]))
        y_parts.append(jnp.concatenate(pair_parts, axis=1) + y_inter)
        upd = _dot_tn(xdd[:, g * gw:(g + 1) * gw].astype(bf16), bg)
        for hl in range(hpg):
            h = g * hpg + hl
            rs = slice(g * gw + hl * SSM_HEAD_DIM, g * gw + (hl + 1) * SSM_HEAD_DIM)
            st_ref[rs, :] = (st_g[hl * SSM_HEAD_DIM:(hl + 1) * SSM_HEAD_DIM] * ea_t[h:h + 1, last:last + 1]
                             + upd[hl * SSM_HEAD_DIM:(hl + 1) * SSM_HEAD_DIM])
    y = jnp.concatenate(y_parts, axis=1) + dskip * xs
    y = y * _silu(z.astype(f32))
    normed = []
    for g in range(n_groups):
        yg = y[:, g * gw:(g + 1) * gw]
        normed.append(yg * lax.rsqrt(jnp.mean(yg * yg, axis=-1, keepdims=True) + EPS))
    return jnp.concatenate(normed, axis=1) * wnorm


def _ssd_prompt_body(xbc_ref, dt_ref, dtt_ref, z_ref, wc_ref, bc_ref, dtb_ref, dtbt_ref, a_ref, at_ref,
                     ex_ref, dk_ref, wn_ref, y_ref, st_ref, buf_ref, *, conv_w):
    c = pl.program_id(1)
    l = xbc_ref.shape[0]

    @pl.when(c == 0)
    def _():
        buf_ref[0:SUBLANES, :] = jnp.zeros((SUBLANES, buf_ref.shape[1]), f32)
        st_ref[...] = jnp.zeros(st_ref.shape, f32)

    cur = xbc_ref[...]
    buf_ref[SUBLANES:SUBLANES + l, :] = cur
    before = buf_ref[0:l, :]
    conv = bc_ref[...] + wc_ref[conv_w - 1:conv_w, :] * cur
    for back in range(1, conv_w):
        conv = conv + wc_ref[conv_w - 1 - back:conv_w - back, :] * _rows_back(cur, before, back)
    buf_ref[0:SUBLANES, :] = buf_ref[l:l + SUBLANES, :]
    y = _ssd_chunk(conv, dt_ref[...], dtt_ref[...], z_ref[...], st_ref, dtb_ref[...], dtbt_ref[...],
                   a_ref[...], at_ref[...], ex_ref[...], dk_ref[...], wn_ref[...], n_valid=None)
    y_ref[...] = y.astype(y_ref.dtype)


def _ssd_prompt(xbc, dt, dtt, z, ssm_w, *, bn, seq_len, n_state):
    r, conv_dim = xbc.shape
    d_inner = z.shape[1]
    l = math.gcd(seq_len, SSM_CHUNK)
    assert l % LANES == 0
    nc = seq_len // l
    wc, bc, dtb, dtbt, a_row, a_col, expand, dskip, wnorm = ssm_w
    conv_w = wc.shape[0]
    assert conv_w - 1 <= SUBLANES
    body = functools.partial(_ssd_prompt_body, conv_w=conv_w)
    consts = [wc, bc, dtb, dtbt, a_row, a_col, expand, dskip, wnorm]
    return pl.pallas_call(
        body,
        grid=(bn, nc),
        in_specs=[pl.BlockSpec((l, conv_dim), lambda b, c: (b * nc + c, 0)),
                  pl.BlockSpec((l, LANES), lambda b, c: (b * nc + c, 0)),
                  pl.BlockSpec((LANES, l), lambda b, c: (0, b * nc + c)),
                  pl.BlockSpec((l, d_inner), lambda b, c: (b * nc + c, 0))]
                 + [_const_spec(w.shape) for w in consts],
        out_specs=[pl.BlockSpec((l, d_inner), lambda b, c: (b * nc + c, 0)),
                   pl.BlockSpec((None, d_inner, n_state), lambda b, c: (b, 0, 0))],
        out_shape=[jax.ShapeDtypeStruct((r, d_inner), bf16),
                   jax.ShapeDtypeStruct((bn, d_inner, n_state), f32)],
        scratch_shapes=[pltpu.VMEM((l + SUBLANES, conv_dim), f32)],
        compiler_params=_params(2),
        name="ssd_prompt",
    )(xbc, dt, dtt, z, *consts)


def _ssd_sample_body(ext_ref, dt_ref, dtt_ref, z_ref, st_in_ref, wc_ref, bc_ref, dtb_ref, dtbt_ref, a_ref,
                     at_ref, ex_ref, dk_ref, wn_ref, y_ref, st_ref, *, conv_w, n_valid):
    lp = dt_ref.shape[1]
    st_ref[...] = st_in_ref[...]
    for u in range(dt_ref.shape[0]):
        conv = bc_ref[...]
        for k in range(conv_w):
            conv = conv + wc_ref[k:k + 1, :] * ext_ref[u, k:k + lp, :]
        y = _ssd_chunk(conv, dt_ref[u], dtt_ref[u], z_ref[u], st_ref.at[u], dtb_ref[...], dtbt_ref[...],
                       a_ref[...], at_ref[...], ex_ref[...], dk_ref[...], wn_ref[...], n_valid=n_valid)
        y_ref[u] = y.astype(y_ref.dtype)


def _ssd_sample(ext, dt, dtt, z, state, ssm_w, *, n_valid):
    bs, ext_rows, conv_dim = ext.shape
    lp = dt.shape[1]
    d_inner = z.shape[2]
    n_state = state.shape[2]
    wc, bc, dtb, dtbt, a_row, a_col, expand, dskip, wnorm = ssm_w
    body = functools.partial(_ssd_sample_body, conv_w=wc.shape[0], n_valid=n_valid)
    consts = [wc, bc, dtb, dtbt, a_row, a_col, expand, dskip, wnorm]
    per = math.gcd(bs, SSD_SAMPLE_SEQS)

    def seq_spec(rows, cols):
        return pl.BlockSpec((per, rows, cols), lambda s: (s, 0, 0))

    return pl.pallas_call(
        body,
        grid=(bs // per,),
        in_specs=[seq_spec(ext_rows, conv_dim), seq_spec(lp, LANES), seq_spec(LANES, lp), seq_spec(lp, d_inner),
                  seq_spec(d_inner, n_state)]
                 + [_const_spec(w.shape) for w in consts],
        out_specs=[seq_spec(lp, d_inner), seq_spec(d_inner, n_state)],
        out_shape=[jax.ShapeDtypeStruct((bs, lp, d_inner), bf16),
                   jax.ShapeDtypeStruct((bs, d_inner, n_state), f32)],
        compiler_params=_params(1),
        name="ssd_sample",
    )(ext, dt, dtt, z, state, *consts)


def _sattn_body(pt_ref, q_ref, kn_ref, vn_ref, bprev_ref, bcur_ref, ck_ref, cv_ref, o_ref,
                km_ref, ms_ref, ls_ref, os_ref, kbuf_ref, vbuf_ref, sem_ref, *, nbs, n_full, n_tok, ppb):
    n_pages = nbs * ppb
    g = pl.program_id(1)
    n_steps = pl.num_programs(1)
    n_rows = N_HEADS * SUBLANES
    d = ATTN_DIM

    step = pl.program_id(0) * n_steps + g
    total = pl.num_programs(0) * n_steps

    def page_copies(n):
        slot = n % PAGE_SLOTS
        seq = n // n_steps
        first = (n % n_steps) * n_pages
        out = []
        for t in range(n_pages):
            pid = pt_ref[seq, first + t]
            out.append(pltpu.make_async_copy(ck_ref.at[pid], kbuf_ref.at[slot, t], sem_ref.at[slot, t]))
            out.append(pltpu.make_async_copy(cv_ref.at[pid], vbuf_ref.at[slot, t], sem_ref.at[slot, n_pages + t]))
        return out

    @pl.when(step == 0)
    def _():
        for n in range(PAGE_SLOTS - 1):
            @pl.when(n < total)
            def _():
                for c in page_copies(n):
                    c.start()

    @pl.when(step + PAGE_SLOTS - 1 < total)
    def _():
        for c in page_copies(step + PAGE_SLOTS - 1):
            c.start()

    for c in page_copies(step):
        c.wait()
    slot = step % PAGE_SLOTS
    kp = [kbuf_ref.at[slot, t] for t in range(n_pages)]
    vp = [vbuf_ref.at[slot, t] for t in range(n_pages)]

    q4 = q_ref[...]
    q8 = jnp.concatenate([q4, jnp.zeros((SUBLANES - n_tok, d), f32)], axis=0)
    r_i = lax.broadcasted_iota(jnp.int32, (n_rows, d), 0)
    c_i = lax.broadcasted_iota(jnp.int32, (n_rows, d), 1)
    head_mask = (r_i // SUBLANES) == (c_i // HEAD_DIM)
    q_rows = jnp.where(head_mask, jnp.concatenate([q8] * N_HEADS, axis=0), 0.0)
    qb = q_rows.astype(bf16)

    @pl.when(g == 0)
    def _():
        km_ref[...] = jnp.zeros(km_ref.shape, f32)
        ms_ref[...] = jnp.zeros(ms_ref.shape, f32)
        ls_ref[...] = jnp.zeros(ls_ref.shape, f32)

    lane_blk = lax.broadcasted_iota(jnp.int32, (1, LANES), 1)
    sc_new = jnp.zeros((n_rows, LANES), f32)
    ms_new = jnp.zeros((n_rows, LANES), f32)
    ls_new = jnp.zeros((n_rows, LANES), f32)
    o_new = []
    k_t = jnp.concatenate([kp[t][...].reshape(d, -1).astype(bf16) for t in range(n_pages)], axis=1)
    s_all = _dot(qb, k_t)
    for b in range(nbs):
        vblk_t = jnp.concatenate([vp[b * ppb + t][...].reshape(d, -1) for t in range(ppb)], axis=1)
        jj = g * nbs + b
        s = s_all[:, b * MOBA_BLOCK:(b + 1) * MOBA_BLOCK]
        here = (lane_blk == jj).astype(f32)
        sc_new = sc_new + (jnp.sum(s, axis=-1, keepdims=True) * (1.0 / MOBA_BLOCK)) * here
        s = s + jnp.where(jj == n_full - 1, 1.0, 0.0).astype(f32) * bprev_ref[...]
        m = jnp.max(s, axis=-1, keepdims=True)
        p = jnp.exp(s - m)
        l = jnp.sum(p, axis=-1, keepdims=True)
        ms_new = ms_new + m * here
        ls_new = ls_new + l * here
        o_new.append(_dot_nt(p.astype(bf16), vblk_t.astype(bf16)))
    km_ref[...] += sc_new
    ms_ref[...] += ms_new
    ls_ref[...] += ls_new
    os_ref[pl.ds(pl.multiple_of(g * nbs, nbs), nbs)] = jnp.stack(o_new, axis=0)

    @pl.when(g == pl.num_programs(1) - 1)
    def _():
        sc = km_ref[...]
        col = lax.broadcasted_iota(jnp.int32, sc.shape, 1)
        low = jnp.float32(-3e38)
        sc = jnp.where(col < n_full, sc, low)
        self_ = jnp.zeros(sc.shape, f32)
        for _ in range(min(MOBA_TOPK, n_full)):
            mx = jnp.max(sc, axis=-1, keepdims=True)
            idx = jnp.min(jnp.where(sc == mx, col, LANES), axis=-1, keepdims=True)
            self_ = jnp.where(col == idx, 1.0, self_)
            sc = jnp.where(col == idx, low, sc)
        sel = self_ > 0.5
        kn = jnp.concatenate([kn_ref[...], jnp.zeros((SAMPLE_ROWS - n_tok, d), f32)], axis=0).astype(bf16)
        vn = jnp.concatenate([vn_ref[...], jnp.zeros((SAMPLE_ROWS - n_tok, d), f32)], axis=0).astype(bf16)
        s_cur = _dot_nt(qb, kn) + bcur_ref[...]
        ms = ms_ref[...]
        m_tot = jnp.maximum(jnp.max(s_cur, axis=-1, keepdims=True),
                            jnp.max(jnp.where(sel, ms, low), axis=-1, keepdims=True))
        p_cur = jnp.exp(s_cur - m_tot)
        w = jnp.where(sel, jnp.exp(ms - m_tot), 0.0)
        l_tot = jnp.sum(p_cur, axis=-1, keepdims=True) + jnp.sum(w * ls_ref[...], axis=-1, keepdims=True)
        acc = _dot(p_cur.astype(bf16), vn)
        for j in range(n_full):
            acc = acc + w[:, j:j + 1] * os_ref[j]
        out = jnp.where(head_mask, acc / l_tot, 0.0)
        out8 = out[0:SUBLANES]
        for h in range(1, N_HEADS):
            out8 = out8 + out[h * SUBLANES:(h + 1) * SUBLANES]
        o_ref[...] = out8[0:n_tok]


def _attn_sample(q, kn, vn, ck, cv, page_table, bprev, bcur):
    bs, n_tok, d = q.shape
    n_pool, n_heads, hd, page = ck.shape
    assert n_heads * hd == d
    ppb = MOBA_BLOCK // page
    n_pages_seq = page_table.shape[1]
    past = n_pages_seq * page
    assert MOBA_BLOCK % page == 0 and past % MOBA_BLOCK == 0 and n_tok <= SUBLANES
    n_full = past // MOBA_BLOCK
    assert n_full <= LANES
    nbs = math.gcd(n_full, SAMPLE_BLOCKS_PER_STEP)
    n_steps = n_full // nbs
    n_pages = nbs * ppb
    n_rows = N_HEADS * SUBLANES

    seq_spec = pl.BlockSpec((None, n_tok, d), lambda s, g, pt: (s, 0, 0))
    body = functools.partial(_sattn_body, nbs=nbs, n_full=n_full, n_tok=n_tok, ppb=ppb)
    grid_spec = pltpu.PrefetchScalarGridSpec(
        num_scalar_prefetch=1,
        grid=(bs, n_steps),
        in_specs=[seq_spec, seq_spec, seq_spec,
                  pl.BlockSpec(bprev.shape, lambda s, g, pt: (0, 0)),
                  pl.BlockSpec(bcur.shape, lambda s, g, pt: (0, 0)),
                  pl.BlockSpec(memory_space=pl.ANY), pl.BlockSpec(memory_space=pl.ANY)],
        out_specs=seq_spec,
        scratch_shapes=[pltpu.VMEM((n_rows, LANES), f32),
                        pltpu.VMEM((n_rows, LANES), f32),
                        pltpu.VMEM((n_rows, LANES), f32),
                        pltpu.VMEM((n_full, n_rows, d), f32),
                        pltpu.VMEM((PAGE_SLOTS, n_pages, n_heads, hd, page), f32),
                        pltpu.VMEM((PAGE_SLOTS, n_pages, n_heads, hd, page), f32),
                        pltpu.SemaphoreType.DMA((PAGE_SLOTS, 2 * n_pages))],
    )
    return pl.pallas_call(
        body,
        grid_spec=grid_spec,
        out_shape=jax.ShapeDtypeStruct((bs, n_tok, d), f32),
        compiler_params=_params(2),
        name="moba_sample",
    )(page_table, q, kn, vn, bprev, bcur, ck, cv)


def _merge_body(at_ref, yn_ref, ga_ref, gb_ref, x_ref, g1_ref, wpa_ref, wps_ref, wo_ref, lg_ref, lb_ref,
                o_ref, *, alpha):
    pa = _dot(at_ref[...].astype(bf16), wpa_ref[...])
    ps = _dot(yn_ref[...].astype(bf16), wps_ref[...])
    merged = jax.nn.sigmoid(ga_ref[...].astype(f32)) * pa + jax.nn.sigmoid(gb_ref[...].astype(f32)) * ps
    mo = _dot(merged.astype(bf16), wo_ref[...])
    o_ref[...] = _layer_norm(alpha * x_ref[...] + g1_ref[...] * mo, lg_ref[...], lb_ref[...])


def _mod_spec(per_row_mod, tm, d, tiles_per_seq):
    if per_row_mod:
        return pl.BlockSpec((tm, d), lambda i: (i, 0))
    return pl.BlockSpec((None, 1, d), lambda i: (i // tiles_per_seq, 0, 0))


def _merge(attn, yn, ga, gb, x2d, gate1, wpa, wps, wo, lg, lb, *, per_row_mod, seq_len, alpha):
    r, d = x2d.shape
    tm = 256
    assert r % tm == 0
    tiles_per_seq = 1 if per_row_mod else seq_len // tm
    ms = _mod_spec(per_row_mod, tm, d, tiles_per_seq)

    def row_spec(w):
        return pl.BlockSpec((tm, w), lambda i: (i, 0))

    return pl.pallas_call(
        functools.partial(_merge_body, alpha=alpha),
        grid=(r // tm,),
        in_specs=[row_spec(attn.shape[1]), row_spec(yn.shape[1]), row_spec(d), row_spec(d), row_spec(d), ms,
                  _const_spec(wpa.shape), _const_spec(wps.shape), _const_spec(wo.shape),
                  _const_spec(lg.shape), _const_spec(lb.shape)],
        out_specs=row_spec(d),
        out_shape=jax.ShapeDtypeStruct((r, d), f32),
        compiler_params=_params(1),
        name="merge_ln1",
    )(attn, yn, ga, gb, x2d, gate1, wpa, wps, wo, lg, lb)


FFN_COL_CHUNKS = 1


def _ffn_body(x_ref, sh_ref, sc_ref, g2_ref, wu_ref, wc_ref, bc_ref, wd_ref, lg_ref, lb_ref, *rest,
              alpha, conv_w, tiles_per_seq, sample_len):
    if sample_len is None:
        y_ref, tail_ref, buf_ref, carry_ref = rest
    else:
        p_refs = rest[:conv_w - 1]
        y_ref, hup_ref, buf_ref, carry_ref = rest[conv_w - 1:]
    x = x_ref[...]
    tm = x.shape[0]
    ff = wd_ref.shape[0]
    cw = ff // FFN_COL_CHUNKS
    t = pl.program_id(0) % tiles_per_seq

    @pl.when(t == 0)
    def _():
        carry_ref[...] = jnp.zeros(carry_ref.shape, f32)

    u = (x * (1.0 + sc_ref[...]) + sh_ref[...]).astype(bf16)
    if sample_len is not None:
        tmod = lax.broadcasted_iota(jnp.int32, (tm, cw), 0) % sample_len
    f = jnp.zeros((tm, x.shape[1]), f32)
    for c in range(FFN_COL_CHUNKS):
        halves = []
        for half in range(2):
            c0 = half * ff + c * cw
            buf_ref[0:SUBLANES, :] = carry_ref[:, c0:c0 + cw]
            hup = _dot(u, wu_ref[:, c0:c0 + cw])
            buf_ref[SUBLANES:SUBLANES + tm, :] = hup
            if sample_len is not None:
                hup_ref[:, c0:c0 + cw] = hup
            hc = bc_ref[:, c0:c0 + cw] + wc_ref[conv_w - 1:conv_w, c0:c0 + cw] * hup
            for k in range(conv_w - 1):
                back = conv_w - 1 - k
                prev = _rows_back(hup, buf_ref[0:tm, :], back)
                if sample_len is not None:
                    prev = jnp.where(tmod >= back, prev, p_refs[back - 1][:, c0:c0 + cw])
                hc = hc + wc_ref[k:k + 1, c0:c0 + cw] * prev
            carry_ref[:, c0:c0 + cw] = buf_ref[tm:tm + SUBLANES, :]
            halves.append(hc)
        gact = (_silu(halves[0]) * halves[1]).astype(bf16)
        f = f + _dot(gact, wd_ref[c * cw:(c + 1) * cw, :])
    y_ref[...] = _layer_norm(alpha * x + g2_ref[...] * f, lg_ref[...], lb_ref[...])
    if sample_len is None:
        @pl.when(t == tiles_per_seq - 1)
        def _():
            tail_ref[...] = carry_ref[...]


def _ffn(x2d, shift, scale, gate, wu, wc, bc, wd, lg, lb, prevs, *, per_row_mod, seq_len, bn, alpha,
         sample_len):
    r, d = x2d.shape
    ff2 = wu.shape[1]
    ff = wd.shape[0]
    conv_w = wc.shape[0]
    tm = 256 if sample_len is None else 128
    assert r % tm == 0 and ff % (FFN_COL_CHUNKS * LANES) == 0 and conv_w - 1 <= SUBLANES
    tiles_per_seq = 1 if per_row_mod else seq_len // tm
    ms = _mod_spec(per_row_mod, tm, d, tiles_per_seq)

    def row_spec(w):
        return pl.BlockSpec((tm, w), lambda i: (i, 0))

    in_specs = [row_spec(d), ms, ms, ms, _const_spec(wu.shape), _const_spec(wc.shape), _const_spec(bc.shape),
                _const_spec(wd.shape), _const_spec(lg.shape), _const_spec(lb.shape)]
    args = [x2d, shift, scale, gate, wu, wc, bc, wd, lg, lb]
    if sample_len is None:
        out_shape = [jax.ShapeDtypeStruct((r, d), f32), jax.ShapeDtypeStruct((bn, SUBLANES, ff2), f32)]
        out_specs = [row_spec(d), pl.BlockSpec((None, SUBLANES, ff2), lambda i: (i // tiles_per_seq, 0, 0))]
    else:
        assert tm % sample_len == 0 and len(prevs) == conv_w - 1
        in_specs += [row_spec(ff2)] * len(prevs)
        args += list(prevs)
        out_shape = [jax.ShapeDtypeStruct((r, d), f32), jax.ShapeDtypeStruct((r, ff2), f32)]
        out_specs = [row_spec(d), row_spec(ff2)]
    body = functools.partial(_ffn_body, alpha=alpha, conv_w=conv_w, tiles_per_seq=tiles_per_seq,
                             sample_len=sample_len)
    return pl.pallas_call(
        body,
        grid=(r // tm,),
        in_specs=in_specs,
        out_specs=out_specs,
        out_shape=out_shape,
        scratch_shapes=[pltpu.VMEM((tm + SUBLANES, ff // FFN_COL_CHUNKS), f32),
                        pltpu.VMEM((SUBLANES, ff2), f32)],
        compiler_params=_params(1),
        name="conv_ffn",
    )(*args)


def _rel_bucket(dist):
    n = jnp.maximum(dist, 0)
    max_exact = REL_BUCKETS // 2
    nf = jnp.maximum(n, 1).astype(f32)
    large = max_exact + (jnp.log(nf / max_exact) / math.log(REL_MAX_DIST / max_exact)
                         * (REL_BUCKETS - max_exact)).astype(jnp.int32)
    large = jnp.minimum(large, REL_BUCKETS - 1)
    return jnp.where(n < max_exact, n, large)


def _rel_bias(rel_table, dist):
    rel = rel_table - rel_table[REL_BUCKETS - 1]
    onehot = (_rel_bucket(dist)[..., None] == jnp.arange(REL_BUCKETS)).astype(f32)
    b = jnp.dot(onehot, rel, precision=lax.Precision.HIGHEST)
    return jnp.moveaxis(b, -1, 0)


def _prompt_bias_tables(rel_table):
    n = MOBA_BLOCK
    by_dist = _rel_bias(rel_table, jnp.arange(2 * n, dtype=jnp.int32)) * LOG2E
    masked = jnp.full((by_dist.shape[0], n - 1), MASKED, f32)
    bd = _toeplitz(jnp.concatenate([masked, by_dist[:, :n]], axis=1), n)
    bp = _toeplitz(by_dist[:, 1:], n)
    return bd.astype(f32), bp.astype(f32)


def _toeplitz(f, n):
    g = jnp.concatenate([f, jnp.zeros((f.shape[0], 1), f.dtype)], axis=1)
    y = jnp.tile(g, (1, n))[:, :n * (2 * n - 1)].reshape(f.shape[0], n, 2 * n - 1)
    return y[:, :, n - 1:]


def _sample_bias_tables(rel_table, n_tok):
    n_rows = N_HEADS * SUBLANES
    t = jnp.arange(SUBLANES, dtype=jnp.int32)
    a = jnp.arange(MOBA_BLOCK, dtype=jnp.int32)
    bprev = _rel_bias(rel_table, MOBA_BLOCK + t[:, None] - a[None, :])
    bprev = jnp.where((t < n_tok)[None, :, None], bprev, 0.0).reshape(n_rows, MOBA_BLOCK)
    tk = jnp.arange(SAMPLE_ROWS, dtype=jnp.int32)
    dist = t[:, None] - tk[None, :]
    ok = (dist >= 0) & (t[:, None] < n_tok) & (tk[None, :] < n_tok)
    bcur = jnp.where(ok[None], _rel_bias(rel_table, dist), MASKED).reshape(n_rows, SAMPLE_ROWS)
    return bprev.astype(f32), bcur.astype(f32)


def _pad_cols(w, n):
    return jnp.pad(w, ((0, 0), (0, n - w.shape[1])))


def kernel(x_prompt, x_sample, cache_k, cache_v, page_table, state_ssm, state_conv_ssm, state_conv_ffn,
           c_prompt, c_sample, rel_table, w_ada, b_ada, w_in, w_conv_ssm, b_conv_ssm, dt_bias, a_log,
           d_skip, w_norm_ssm, w_proj_attn, w_proj_ssm, w_out, ln1_g, ln1_b, w_up, w_conv_ffn, b_conv_ffn,
           w_down, ln2_g, ln2_b):
    depth = w_ada.shape[0]
    alpha = (2 * depth) ** 0.25
    bp_, seq, d = x_prompt.shape
    bs, n_tok, _ = x_sample.shape
    ssm_heads = dt_bias.shape[1]
    d_inner = ssm_heads * SSM_HEAD_DIM
    n_state = state_ssm.shape[-1]
    conv_dim = w_conv_ssm.shape[-1]
    ff2 = w_up.shape[-1]
    ssm_conv = w_conv_ssm.shape[1]
    ffn_conv = w_conv_ffn.shape[1]
    assert ssm_heads <= LANES and n_tok <= SUBLANES
    page = cache_k.shape[2]
    scale = HEAD_DIM ** -0.5

    bd, bpv = _prompt_bias_tables(rel_table)
    bprev_s, bcur_s = _sample_bias_tables(rel_table, n_tok)
    expand = (jnp.arange(LANES)[:, None] == (jnp.arange(d_inner)[None, :] // SSM_HEAD_DIM)).astype(bf16)

    yp = x_prompt.reshape(bp_ * seq, d)
    ys = x_sample.reshape(bs * n_tok, d)
    outs_p = [[] for _ in range(5)]
    outs_s = [[] for _ in range(5)]
    for l in range(depth):
        cuts = [ATTN_DIM, 2 * ATTN_DIM, 3 * ATTN_DIM, 3 * ATTN_DIM + d_inner,
                3 * ATTN_DIM + d_inner + conv_dim, 3 * ATTN_DIM + d_inner + conv_dim + ssm_heads,
                3 * ATTN_DIM + d_inner + conv_dim + ssm_heads + d]
        wq, wk, wv, wz, wxbc, wdt, wga, wgb = jnp.split(w_in[l], cuts, axis=1)
        wdt = _pad_cols(wdt, LANES)
        wq = wq * scale
        nat_rest = [(d_inner, bf16), (conv_dim, f32), (LANES, f32), (d, bf16), (d, bf16)]
        w_rest = [wz, wxbc, wdt, wga, wgb]
        nat_p = [(ATTN_DIM, None)] + nat_rest
        wn_p = jnp.concatenate([wk] + w_rest, axis=1).astype(bf16)
        a_ = ATTN_DIM
        tr_p = [(0, a_, bf16, 0, False), (a_, a_, f32, 0, True), (2 * a_, a_, f32, 0, True),
                (2 * a_, a_, bf16, BF16_ROWS, False), (3 * a_, LANES, f32, 0, False)]
        wt_p = jnp.concatenate([wq * LOG2E, wk, wv, wdt], axis=1).T.astype(bf16)
        nat_s = [(ATTN_DIM, f32)] * 3 + nat_rest
        wn_s = jnp.concatenate([wq, wk, wv] + w_rest, axis=1).astype(bf16)
        tr_s = [(0, LANES, f32, 0, False)]
        wt_s = wdt.T.astype(bf16)
        ssm_w = (w_conv_ssm[l], b_conv_ssm[l].reshape(1, conv_dim),
                 _pad_cols(dt_bias[l].reshape(1, -1), LANES), _pad_cols(dt_bias[l].reshape(1, -1), LANES).T,
                 -jnp.exp(_pad_cols(a_log[l].reshape(1, -1), LANES)),
                 -jnp.exp(_pad_cols(a_log[l].reshape(1, -1), LANES)).T,
                 expand, jnp.repeat(d_skip[l], SSM_HEAD_DIM).reshape(1, d_inner),
                 w_norm_ssm[l].reshape(1, d_inner))
        wpa, wps, wo = (w_proj_attn[l].astype(bf16), w_proj_ssm[l].astype(bf16), w_out[l].astype(bf16))
        lg1, lb1 = ln1_g[l].reshape(1, d), ln1_b[l].reshape(1, d)
        lg2, lb2 = ln2_g[l].reshape(1, d), ln2_b[l].reshape(1, d)
        wu, wd = w_up[l].astype(bf16), w_down[l].astype(bf16)
        wcf, bcf = w_conv_ffn[l], b_conv_ffn[l].reshape(1, ff2)

        mod = _ada(jnp.concatenate([c_prompt, c_sample], axis=0), w_ada[l], b_ada[l])
        mod_p = mod[:bp_].reshape(bp_, 6, 1, d)
        mod_s = jnp.repeat(mod[bp_:].reshape(bs, 6, 1, d), n_tok, axis=2).reshape(bs, 6, n_tok, d)
        mod_s = jnp.moveaxis(mod_s, 1, 0).reshape(6, bs * n_tok, d)
        sh1p, sc1p, g1p, sh2p, sc2p, g2p = (mod_p[:, i] for i in range(6))
        sh1s, sc1s, g1s, sh2s, sc2s, g2s = (mod_s[i] for i in range(6))

        z_p, xbc_p, dt_p, ga_p, gb_p, qT_p, kT_p, vTf_p, vT_p, dtT_p, k2_p = _inproj(
            yp, sh1p, sc1p, wn_p, wt_p, nat_p, tr_p, per_row_mod=False, seq_len=seq, attn_layout=True)
        attn_p = _attn_prompt(k2_p, qT_p, vT_p, bd, bpv, bn=bp_, seq_len=seq)
        yn_p, st_p = _ssd_prompt(xbc_p, dt_p, dtT_p, z_p, ssm_w, bn=bp_, seq_len=seq, n_state=n_state)
        x1_p = _merge(attn_p, yn_p, ga_p, gb_p, yp, g1p, wpa, wps, wo, lg1, lb1,
                      per_row_mod=False, seq_len=seq, alpha=alpha)
        yp, tail_p = _ffn(x1_p, sh2p, sc2p, g2p, wu, wcf, bcf, wd, lg2, lb2, (),
                          per_row_mod=False, seq_len=seq, bn=bp_, alpha=alpha, sample_len=None)
        outs_p[0].append(jnp.transpose(kT_p.reshape(bp_, N_HEADS, HEAD_DIM, seq), (0, 3, 1, 2)))
        outs_p[1].append(jnp.transpose(vTf_p.reshape(bp_, N_HEADS, HEAD_DIM, seq), (0, 3, 1, 2)))
        outs_p[2].append(st_p.reshape(bp_, ssm_heads, SSM_HEAD_DIM, n_state))
        outs_p[3].append(xbc_p.reshape(bp_, seq, conv_dim)[:, seq - (ssm_conv - 1):])
        outs_p[4].append(tail_p[:, SUBLANES - (ffn_conv - 1):])

        r_s = bs * n_tok
        r_pad = -(-r_s // 256) * 256

        def pad_rows(a):
            return jnp.pad(a, ((0, r_pad - r_s), (0, 0)))

        q_s, k_s, v_s, z_s, xbc_s, dt_s, ga_s, gb_s, dtT_s = _inproj(
            pad_rows(ys), pad_rows(sh1s), pad_rows(sc1s), wn_s, wt_s, nat_s, tr_s,
            per_row_mod=True, seq_len=None, attn_layout=False)
        q_s, k_s, v_s = (a[:r_s].reshape(bs, n_tok, ATTN_DIM) for a in (q_s, k_s, v_s))
        attn_s = _attn_sample(q_s, k_s, v_s, jnp.transpose(cache_k[l], (0, 2, 3, 1)),
                              jnp.transpose(cache_v[l], (0, 2, 3, 1)), page_table, bprev_s, bcur_s)
        row_pad = SAMPLE_ROWS - n_tok
        xbc_s3 = xbc_s[:r_s].reshape(bs, n_tok, conv_dim)
        ext = jnp.concatenate([state_conv_ssm[l], xbc_s3,
                               jnp.zeros((bs, row_pad + SUBLANES - (ssm_conv - 1), conv_dim), f32)], axis=1)
        dt_s3 = jnp.pad(dt_s[:r_s].reshape(bs, n_tok, LANES), ((0, 0), (0, row_pad), (0, 0)))
        dtT_s3 = jnp.pad(jnp.moveaxis(dtT_s[:, :r_s].reshape(LANES, bs, n_tok), 0, 1),
                         ((0, 0), (0, 0), (0, row_pad)))
        z_s3 = jnp.pad(z_s[:r_s].reshape(bs, n_tok, d_inner), ((0, 0), (0, row_pad), (0, 0)))
        yn_s3, st_s = _ssd_sample(ext, dt_s3, dtT_s3, z_s3, state_ssm[l].reshape(bs, d_inner, n_state),
                                  ssm_w, n_valid=n_tok)
        yn_s = yn_s3[:, :n_tok].reshape(r_s, d_inner)
        x1_s = _merge(pad_rows(attn_s.reshape(r_s, ATTN_DIM)), pad_rows(yn_s), ga_s, gb_s, pad_rows(ys),
                      pad_rows(g1s), wpa, wps, wo, lg1, lb1, per_row_mod=True, seq_len=None, alpha=alpha)
        cf = state_conv_ffn[l]
        prevs = []
        for back in range(1, ffn_conv):
            rows = [cf[:, ffn_conv - 1 - back + t] if t < back else jnp.zeros((bs, ff2), f32)
                    for t in range(n_tok)]
            prevs.append(pad_rows(jnp.stack(rows, axis=1).reshape(r_s, ff2)))
        y_s, hup_s = _ffn(x1_s, pad_rows(sh2s), pad_rows(sc2s), pad_rows(g2s), wu, wcf, bcf, wd, lg2, lb2,
                          prevs, per_row_mod=True, seq_len=None, bn=bs, alpha=alpha, sample_len=n_tok)
        ys = y_s[:r_s]
        outs_s[0].append(k_s.reshape(bs, n_tok, N_HEADS, HEAD_DIM))
        outs_s[1].append(v_s.reshape(bs, n_tok, N_HEADS, HEAD_DIM))
        outs_s[2].append(st_s.reshape(bs, ssm_heads, SSM_HEAD_DIM, n_state))
        cs_ext = jnp.concatenate([state_conv_ssm[l], xbc_s3], axis=1)
        outs_s[3].append(cs_ext[:, -(ssm_conv - 1):])
        cf_ext = jnp.concatenate([cf, hup_s[:r_s].reshape(bs, n_tok, ff2)], axis=1)
        outs_s[4].append(cf_ext[:, -(ffn_conv - 1):])

    return (yp.reshape(bp_, seq, d), ys.reshape(bs, n_tok, d),
            *(jnp.stack(o) for o in outs_p), *(jnp.stack(o) for o in outs_s))
```

```python
import functools
import math

import jax
import jax.numpy as jnp
from jax import lax
from jax.experimental import pallas as pl
from jax.experimental.pallas import tpu as pltpu

N_HEADS = 8
HEAD_DIM = 64
ATTN_DIM = N_HEADS * HEAD_DIM
MOBA_BLOCK = 256
MOBA_TOPK = 3
REL_BUCKETS = 32
REL_MAX_DIST = 128
SSM_HEAD_DIM = 64
SSM_CHUNK = 256
EPS = 1e-5

LANES = 128
SUBLANES = 8
BF16_ROWS = 16
VMEM_LIMIT = 56 * 1024 * 1024

MASKED = -1e30
SAMPLE_ROWS = 16
SAMPLE_BLOCKS_PER_STEP = 4
PAGE_SLOTS = 3
SSD_SAMPLE_SEQS = 2
FAR_GROUP = 4
QUERY_BLOCKS_PER_STEP = 4
LOG2E = math.log2(math.e)

f32 = jnp.float32
bf16 = jnp.bfloat16

_NT = (((1,), (1,)), ((), ()))
_TN = (((0,), (0,)), ((), ()))


def _dot(a, b):
    return jnp.dot(a, b, preferred_element_type=f32)


def _dot_nt(a, b):
    return lax.dot_general(a, b, _NT, preferred_element_type=f32)


def _dot_tn(a, b):
    return lax.dot_general(a, b, _TN, preferred_element_type=f32)


def _split3(x):
    hi = x.astype(bf16)
    r = x - hi.astype(f32)
    mid = r.astype(bf16)
    lo = (r - mid.astype(f32)).astype(bf16)
    return hi, mid, lo


def _dot_exact_rhs(a, m):
    return _dot(jnp.concatenate(_split3(a), axis=1), jnp.concatenate([m, m, m], axis=0))


def _dot_exact_lhs(m, a):
    return _dot(jnp.concatenate([m, m, m], axis=1), jnp.concatenate(_split3(a), axis=0))


def _rows_back(cur, before, back):
    rows, cols = cur.shape
    sub = lax.broadcasted_iota(jnp.int32, cur.shape, 0) % SUBLANES
    mixed = jnp.where(sub >= SUBLANES - back, before, cur)
    return pltpu.roll(mixed.reshape(rows // SUBLANES, SUBLANES, cols), back, axis=1).reshape(rows, cols)


def _silu(x):
    return x * jax.nn.sigmoid(x)


def _softplus(x):
    return jnp.maximum(x, 0.0) + jnp.log1p(jnp.exp(-jnp.abs(x)))


def _layer_norm(x, g, b):
    mu = jnp.mean(x, axis=-1, keepdims=True)
    xc = x - mu
    var = jnp.mean(xc * xc, axis=-1, keepdims=True)
    return xc * lax.rsqrt(var + EPS) * g + b


def _const_spec(shape):
    nd = len(shape)
    return pl.BlockSpec(shape, lambda *_: (0,) * nd, pipeline_mode=pl.Buffered(1))


def _params(n_grid, flags=None):
    return pltpu.CompilerParams(
        dimension_semantics=("arbitrary",) * n_grid, vmem_limit_bytes=VMEM_LIMIT, flags=flags)


def _ada_body(c_ref, w_ref, b_ref, o_ref):
    a = _silu(c_ref[...])
    hi, mid, lo = _split3(a)
    whi, wmid, wlo = _split3(w_ref[...])
    acc = _dot(hi, whi) + (_dot(hi, wmid) + _dot(mid, whi))
    acc = acc + (_dot(hi, wlo) + _dot(mid, wmid) + _dot(lo, whi))
    o_ref[...] = acc + b_ref[...]


def _ada(c, w, b):
    n, d = c.shape
    dn = w.shape[1]
    tn = 1024 if dn % 1024 == 0 else dn
    return pl.pallas_call(
        _ada_body,
        grid=(dn // tn,),
        in_specs=[pl.BlockSpec((n, d), lambda j: (0, 0)),
                  pl.BlockSpec((d, tn), lambda j: (0, j)),
                  pl.BlockSpec((1, tn), lambda j: (0, j))],
        out_specs=pl.BlockSpec((n, tn), lambda j: (0, j)),
        out_shape=jax.ShapeDtypeStruct((n, dn), f32),
        compiler_params=_params(1),
        name="ada_mod",
    )(c, w, b.reshape(1, dn))


def _inproj_body(x_ref, sh_ref, sc_ref, wn_ref, wt_ref, *outs, nat, tr, k_col, tiles_per_seq):
    u = (x_ref[...] * (1.0 + sc_ref[...]) + sh_ref[...]).astype(bf16)
    tm = u.shape[0]
    o = 0
    col = 0
    for width, dt in nat:
        res = _dot(u, wn_ref[:, col:col + width])
        if k_col is not None and col == k_col:
            kf = res
        if dt is not None:
            outs[o][...] = res.astype(dt)
            o += 1
        col += width
    if tr:
        t = _dot_nt(wt_ref[...], u)
        for row, height, _, ones_rows, _ in tr:
            if ones_rows:
                step = HEAD_DIM + ones_rows
                for h in range(height // HEAD_DIM):
                    outs[o][h * step:h * step + HEAD_DIM, :] = (
                        t[row + h * HEAD_DIM:row + (h + 1) * HEAD_DIM].astype(outs[o].dtype))
                    outs[o][h * step + HEAD_DIM:(h + 1) * step, :] = jnp.ones((ones_rows, tm), outs[o].dtype)
            else:
                outs[o][...] = t[row:row + height].astype(outs[o].dtype)
            o += 1
    if k_col is not None:
        pos = (pl.program_id(0) % tiles_per_seq) * tm
        blk = pos // MOBA_BLOCK
        onehot = (lax.broadcasted_iota(jnp.int32, (tm, LANES), 1) == blk).astype(bf16)
        k2_ref = outs[o]
        for p in range(ATTN_DIM // LANES):
            k2_ref[p] = jnp.concatenate([kf[:, p * LANES:(p + 1) * LANES].astype(bf16), onehot], axis=1)


def _inproj(x2d, shift, scale, wn, wt, nat, tr, *, per_row_mod, seq_len, attn_layout):
    r, d = x2d.shape
    tm = 256
    assert r % tm == 0
    n_tiles = r // tm
    if per_row_mod:
        mod_spec = pl.BlockSpec((tm, d), lambda i: (i, 0))
        tiles_per_seq = 1
    else:
        assert seq_len % tm == 0 and MOBA_BLOCK % tm == 0
        tiles_per_seq = seq_len // tm
        mod_spec = pl.BlockSpec((None, 1, d), lambda i: (i // tiles_per_seq, 0, 0))
    out_shape, out_specs = [], []
    for width, dt in nat:
        if dt is not None:
            out_shape.append(jax.ShapeDtypeStruct((r, width), dt))
            out_specs.append(pl.BlockSpec((tm, width), lambda i: (i, 0)))
    for _, height, dt, ones_rows, per_seq in tr:
        rows = height + (height // HEAD_DIM) * ones_rows
        if per_seq:
            out_shape.append(jax.ShapeDtypeStruct((r // seq_len, rows, seq_len), dt))
            out_specs.append(pl.BlockSpec((None, rows, tm), lambda i: (i // tiles_per_seq, 0, i % tiles_per_seq)))
        else:
            out_shape.append(jax.ShapeDtypeStruct((rows, r), dt))
            out_specs.append(pl.BlockSpec((rows, tm), lambda i: (0, i)))
    k_col = None
    if attn_layout:
        k_col = 0
        n_pairs = ATTN_DIM // LANES
        out_shape.append(jax.ShapeDtypeStruct((n_pairs, r, 2 * LANES), bf16))
        out_specs.append(pl.BlockSpec((n_pairs, tm, 2 * LANES), lambda i: (0, i, 0)))
    body = functools.partial(_inproj_body, nat=tuple(nat), tr=tuple(tr), k_col=k_col,
                             tiles_per_seq=tiles_per_seq)
    return pl.pallas_call(
        body,
        grid=(n_tiles,),
        in_specs=[pl.BlockSpec((tm, d), lambda i: (i, 0)), mod_spec, mod_spec,
                  _const_spec(wn.shape), _const_spec(wt.shape)],
        out_specs=out_specs,
        out_shape=out_shape,
        compiler_params=_params(1),
        name="in_proj",
    )(x2d, shift, scale, wn, wt)


def _select_topk_rows(s, n_valid_rows):
    nblk = s.shape[0]
    row = lax.broadcasted_iota(jnp.int32, s.shape, 0)
    low = jnp.float32(-3e38)
    s = jnp.where(row < n_valid_rows, s, low)
    sel = jnp.zeros(s.shape, jnp.bool_)
    for _ in range(MOBA_TOPK):
        m = jnp.max(s, axis=0, keepdims=True)
        idx = jnp.min(jnp.where(s == m, row, nblk), axis=0, keepdims=True)
        hit = (row == idx) & (m > low)
        sel = sel | hit
        s = jnp.where(row == idx, low, s)
    return sel


def _attn_body(k2_ref, qT_ref, vT_ref, bd_ref, bp_ref, o_ref, km_ref, *bufs, n_blocks, group, per):
    step = pl.program_id(2)
    blk = MOBA_BLOCK
    hd = HEAD_DIM
    nsel = km_ref.shape[0]

    @pl.when(step == 0)
    def _():
        km_ref[...] = jnp.zeros(km_ref.shape, f32)

        def mean_body(j, c):
            kk = k2_ref[pl.ds(pl.multiple_of(j * blk, blk), blk), :].astype(f32)
            km_ref[pl.ds(j, 1), :] = jnp.sum(kk, axis=0, keepdims=True) * (1.0 / blk)
            return c
        lax.fori_loop(0, n_blocks, mean_body, 0)

    km = km_ref[...]
    km_hi = km.astype(bf16)
    km_lo = (km - km_hi.astype(f32)).astype(bf16)
    vrows = vT_ref.shape[0] // 2
    last_group = n_blocks // group - 1

    def k_rows(j, n):
        return k2_ref[pl.ds(pl.multiple_of(j * blk, blk), n * blk), :]

    def v_cols(h, j, n):
        return vT_ref[h * vrows:(h + 1) * vrows, pl.ds(pl.multiple_of(j * blk, blk), n * blk)]

    def logits_into(s_ref, q2f_ref, g):
        s_ref[...] = _dot(k_rows(jnp.minimum(g, last_group) * group, group), q2f_ref[...])

    def start(i, q2n_ref, q2f_ref, sa_ref):
        q_pair = qT_ref[:, pl.ds(pl.multiple_of(i * blk, blk), blk)]
        zq = jnp.zeros((hd, blk), bf16)
        q_rows = jnp.concatenate([jnp.concatenate([q_pair[0:hd], zq], axis=1),
                                  jnp.concatenate([zq, q_pair[hd:2 * hd]], axis=1)], axis=0)
        zrest = jnp.zeros((q2f_ref.shape[0] - 2 * hd, 2 * blk), bf16)
        q2f_ref[0:2 * hd, :] = q_rows
        q2f_ref[2 * hd:, :] = zrest
        q2n_ref[0:2 * hd, :] = q_rows
        q2n_ref[2 * hd:, :] = zrest
        q2 = q2f_ref[...]
        s_blk = _dot(km_hi, q2) + _dot(km_lo, q2)
        sel = _select_topk_rows(s_blk, i)
        rowsel = lax.broadcasted_iota(jnp.int32, (nsel, 2 * blk), 0)
        far = sel & (rowsel < i - 1)
        near = (sel & (rowsel == i - 1)) | (rowsel == i)
        q2f_ref[2 * hd:2 * hd + nsel, :] = jnp.where(far, 0.0, MASKED).astype(bf16)
        q2n_ref[2 * hd:2 * hd + nsel, :] = jnp.where(near, 0.0, MASKED).astype(bf16)

        jp = jnp.maximum(i - 1, 0)
        first_pad = jnp.where(i > 0, 0.0, MASKED).astype(f32)
        s_cur = _dot(k_rows(i, 1), q2n_ref[...])
        s_prev = _dot(k_rows(jp, 1), q2n_ref[...])
        carry = []
        for h in range(2):
            s = jnp.concatenate([s_cur[:, h * blk:(h + 1) * blk] + bd_ref[h],
                                 s_prev[:, h * blk:(h + 1) * blk] + (bp_ref[h] + first_pad)], axis=0)
            m = jnp.max(s, axis=0, keepdims=True)
            p = jnp.exp2((s - m).astype(bf16))
            v = jnp.concatenate([v_cols(h, i, 1), v_cols(h, jp, 1)], axis=1)
            carry += [m, _dot(v, p)]
        logits_into(sa_ref, q2f_ref, 0)
        return carry

    def finish(i, q2f_ref, sa_ref, sb_ref, carry):
        n_groups = (jnp.maximum(i - 1, 0) + group - 1) // group

        def consume(s_ref, g, carry):
            j0 = jnp.minimum(g, last_group) * group
            pad = jnp.where(g < n_groups, 0.0, MASKED).astype(f32)
            out = []
            for h in range(2):
                m, acc = carry[2 * h:2 * h + 2]
                m2 = jnp.maximum(m, jnp.max(s_ref[:, h * blk:(h + 1) * blk], axis=0, keepdims=True) + pad)
                a = jnp.exp2(m - m2)
                p = jnp.exp2((s_ref[:, h * blk:(h + 1) * blk] - (m2 - pad)).astype(bf16))
                out += [m2, a * acc + _dot(v_cols(h, j0, group), p)]
            return out

        def far_body(t, carry):
            logits_into(sb_ref, q2f_ref, 2 * t + 1)
            carry = consume(sa_ref, 2 * t, list(carry))
            logits_into(sa_ref, q2f_ref, 2 * t + 2)
            return tuple(consume(sb_ref, 2 * t + 1, carry))

        carry = lax.fori_loop(0, (n_groups + 1) // 2, far_body, tuple(carry))
        outs = [carry[2 * h + 1][0:hd] / carry[2 * h + 1][hd:hd + 1] for h in range(2)]
        return jnp.concatenate(outs, axis=0).T

    sets = [bufs[4 * u:4 * u + 4] for u in range(per)]
    blocks = [step * per + u for u in range(per)]
    carries = [start(blocks[u], *sets[u][:3]) for u in range(per)]
    outs = [finish(blocks[u], *sets[u][1:], carries[u]) for u in range(per)]
    o_ref[...] = jnp.concatenate(outs, axis=0).astype(o_ref.dtype)


def _attn_prompt(k2, qT, vT, bd, bp, *, bn, seq_len):
    n_pairs, r, _ = k2.shape
    blk = MOBA_BLOCK
    assert seq_len % blk == 0
    nb = seq_len // blk
    nsel = HEAD_DIM
    assert nb <= nsel
    group = math.gcd(nb, FAR_GROUP)
    per = math.gcd(nb, QUERY_BLOCKS_PER_STEP)
    steps = nb // per
    body = functools.partial(_attn_body, n_blocks=nb, group=group, per=per)
    per_block = [pltpu.VMEM((2 * LANES, 2 * blk), bf16), pltpu.VMEM((2 * LANES, 2 * blk), bf16),
                 pltpu.VMEM((group * blk, 2 * blk), f32), pltpu.VMEM((group * blk, 2 * blk), f32)]
    return pl.pallas_call(
        body,
        grid=(bn, n_pairs, steps),
        in_specs=[pl.BlockSpec((None, seq_len, 2 * LANES), lambda b, p, i: (p, b, 0)),
                  pl.BlockSpec((LANES, seq_len), lambda b, p, i: (p, b)),
                  pl.BlockSpec((vT.shape[0] // n_pairs, seq_len), lambda b, p, i: (p, b)),
                  pl.BlockSpec((2, blk, blk), lambda b, p, i: (p, 0, 0)),
                  pl.BlockSpec((2, blk, blk), lambda b, p, i: (p, 0, 0))],
        out_specs=pl.BlockSpec((per * blk, LANES), lambda b, p, i: (b * steps + i, p)),
        out_shape=jax.ShapeDtypeStruct((r, ATTN_DIM), bf16),
        scratch_shapes=[pltpu.VMEM((nsel, 2 * LANES), f32)] + per_block * per,
        compiler_params=_params(3),
        name="moba_prompt",
    )(k2, qT, vT, bd, bp)


def _ssd_chunk(conv, dt_raw, dt_raw_t, z, st_ref, dtb, dtb_t, a_row, a_col, expand, dskip, wnorm,
               *, n_valid):
    l = conv.shape[0]
    d_inner = z.shape[1]
    n_state = st_ref.shape[1]
    n_heads = d_inner // SSM_HEAD_DIM
    n_groups = (conv.shape[1] - d_inner) // (2 * n_state)
    hpg = n_heads // n_groups
    gw = hpg * SSM_HEAD_DIM
    last = (l if n_valid is None else n_valid) - 1

    act = _silu(conv)
    xs = act[:, :d_inner]
    bm = act[:, d_inner:d_inner + n_groups * n_state]
    cm = act[:, d_inner + n_groups * n_state:]

    dt = _softplus(dt_raw + dtb)
    dt_t = _softplus(dt_raw_t + dtb_t)
    r_i = lax.broadcasted_iota(jnp.int32, (l, l), 0)
    c_i = lax.broadcasted_iota(jnp.int32, (l, l), 1)
    causal = r_i >= c_i
    tril = causal.astype(bf16)
    triu = (r_i <= c_i).astype(bf16)
    acs = _dot_exact_lhs(tril, dt * a_row)
    acs_t = _dot_exact_rhs(dt_t * a_col, triu)
    eacs = jnp.exp(acs)
    dec = jnp.exp(acs[last:last + 1, :] - acs)
    ea_t = jnp.exp(acs_t)

    dt_full = _dot_exact_rhs(dt, expand)
    eacs_full = _dot_exact_rhs(eacs, expand)
    dec_full = _dot_exact_rhs(dec, expand)
    xd = xs * dt_full
    xdd = xd * dec_full
    if n_valid is not None:
        rows = lax.broadcasted_iota(jnp.int32, xdd.shape, 0)
        xdd = jnp.where(rows < n_valid, xdd, 0.0)

    lane = lax.broadcasted_iota(jnp.int32, (l, 2 * SSM_HEAD_DIM), 1)
    y_parts = []
    for g in range(n_groups):
        bg = bm[:, g * n_state:(g + 1) * n_state].astype(bf16)
        cg = cm[:, g * n_state:(g + 1) * n_state].astype(bf16)
        cb = _dot_nt(cg, bg)
        st_g = st_ref[g * gw:(g + 1) * gw, :]
        y_inter = _dot_nt(cg, st_g.astype(bf16)) * eacs_full[:, g * gw:(g + 1) * gw]
        pair_parts = []
        for q in range(hpg // 2):
            h0 = g * hpg + 2 * q
            xdp = xd[:, h0 * SSM_HEAD_DIM:(h0 + 2) * SSM_HEAD_DIM].astype(bf16)
            res = []
            for h in (h0, h0 + 1):
                seg = acs[:, h:h + 1] - acs_t[h:h + 1, :]
                lm = jnp.exp(jnp.where(causal, seg, MASKED))
                res.append(_dot((cb * lm).astype(bf16), xdp))
            pair_parts.append(jnp.where(lane < SSM_HEAD_DIM, res[0], res[1]))
        y_parts.append(jnp.concatenate(pair_parts, axis=1) + y_inter)
        upd = _dot_tn(xdd[:, g * gw:(g + 1) * gw].astype(bf16), bg)
        for hl in range(hpg):
            h = g * hpg + hl
            rs = slice(g * gw + hl * SSM_HEAD_DIM, g * gw + (hl + 1) * SSM_HEAD_DIM)
            st_ref[rs, :] = (st_g[hl * SSM_HEAD_DIM:(hl + 1) * SSM_HEAD_DIM] * ea_t[h:h + 1, last:last + 1]
                             + upd[hl * SSM_HEAD_DIM:(hl + 1) * SSM_HEAD_DIM])
    y = jnp.concatenate(y_parts, axis=1) + dskip * xs
    y = y * _silu(z.astype(f32))
    normed = []
    for g in range(n_groups):
        yg = y[:, g * gw:(g + 1) * gw]
        normed.append(yg * lax.rsqrt(jnp.mean(yg * yg, axis=-1, keepdims=True) + EPS))
    return jnp.concatenate(normed, axis=1) * wnorm


def _ssd_prompt_body(xbc_ref, dt_ref, dtt_ref, z_ref, wc_ref, bc_ref, dtb_ref, dtbt_ref, a_ref, at_ref,
                     ex_ref, dk_ref, wn_ref, y_ref, st_ref, buf_ref, *, conv_w):
    c = pl.program_id(1)
    l = xbc_ref.shape[0]

    @pl.when(c == 0)
    def _():
        buf_ref[0:SUBLANES, :] = jnp.zeros((SUBLANES, buf_ref.shape[1]), f32)
        st_ref[...] = jnp.zeros(st_ref.shape, f32)

    cur = xbc_ref[...]
    buf_ref[SUBLANES:SUBLANES + l, :] = cur
    before = buf_ref[0:l, :]
    conv = bc_ref[...] + wc_ref[conv_w - 1:conv_w, :] * cur
    for back in range(1, conv_w):
        conv = conv + wc_ref[conv_w - 1 - back:conv_w - back, :] * _rows_back(cur, before, back)
    buf_ref[0:SUBLANES, :] = buf_ref[l:l + SUBLANES, :]
    y = _ssd_chunk(conv, dt_ref[...], dtt_ref[...], z_ref[...], st_ref, dtb_ref[...], dtbt_ref[...],
                   a_ref[...], at_ref[...], ex_ref[...], dk_ref[...], wn_ref[...], n_valid=None)
    y_ref[...] = y.astype(y_ref.dtype)


def _ssd_prompt(xbc, dt, dtt, z, ssm_w, *, bn, seq_len, n_state):
    r, conv_dim = xbc.shape
    d_inner = z.shape[1]
    l = math.gcd(seq_len, SSM_CHUNK)
    assert l % LANES == 0
    nc = seq_len // l
    wc, bc, dtb, dtbt, a_row, a_col, expand, dskip, wnorm = ssm_w
    conv_w = wc.shape[0]
    assert conv_w - 1 <= SUBLANES
    body = functools.partial(_ssd_prompt_body, conv_w=conv_w)
    consts = [wc, bc, dtb, dtbt, a_row, a_col, expand, dskip, wnorm]
    return pl.pallas_call(
        body,
        grid=(bn, nc),
        in_specs=[pl.BlockSpec((l, conv_dim), lambda b, c: (b * nc + c, 0)),
                  pl.BlockSpec((l, LANES), lambda b, c: (b * nc + c, 0)),
                  pl.BlockSpec((LANES, l), lambda b, c: (0, b * nc + c)),
                  pl.BlockSpec((l, d_inner), lambda b, c: (b * nc + c, 0))]
                 + [_const_spec(w.shape) for w in consts],
        out_specs=[pl.BlockSpec((l, d_inner), lambda b, c: (b * nc + c, 0)),
                   pl.BlockSpec((None, d_inner, n_state), lambda b, c: (b, 0, 0))],
        out_shape=[jax.ShapeDtypeStruct((r, d_inner), bf16),
                   jax.ShapeDtypeStruct((bn, d_inner, n_state), f32)],
        scratch_shapes=[pltpu.VMEM((l + SUBLANES, conv_dim), f32)],
        compiler_params=_params(2),
        name="ssd_prompt",
    )(xbc, dt, dtt, z, *consts)


def _ssd_sample_body(ext_ref, dt_ref, dtt_ref, z_ref, st_in_ref, wc_ref, bc_ref, dtb_ref, dtbt_ref, a_ref,
                     at_ref, ex_ref, dk_ref, wn_ref, y_ref, st_ref, *, conv_w, n_valid):
    lp = dt_ref.shape[1]
    st_ref[...] = st_in_ref[...]
    for u in range(dt_ref.shape[0]):
        conv = bc_ref[...]
        for k in range(conv_w):
            conv = conv + wc_ref[k:k + 1, :] * ext_ref[u, k:k + lp, :]
        y = _ssd_chunk(conv, dt_ref[u], dtt_ref[u], z_ref[u], st_ref.at[u], dtb_ref[...], dtbt_ref[...],
                       a_ref[...], at_ref[...], ex_ref[...], dk_ref[...], wn_ref[...], n_valid=n_valid)
        y_ref[u] = y.astype(y_ref.dtype)


def _ssd_sample(ext, dt, dtt, z, state, ssm_w, *, n_valid):
    bs, ext_rows, conv_dim = ext.shape
    lp = dt.shape[1]
    d_inner = z.shape[2]
    n_state = state.shape[2]
    wc, bc, dtb, dtbt, a_row, a_col, expand, dskip, wnorm = ssm_w
    body = functools.partial(_ssd_sample_body, conv_w=wc.shape[0], n_valid=n_valid)
    consts = [wc, bc, dtb, dtbt, a_row, a_col, expand, dskip, wnorm]
    per = math.gcd(bs, SSD_SAMPLE_SEQS)

    def seq_spec(rows, cols):
        return pl.BlockSpec((per, rows, cols), lambda s: (s, 0, 0))

    return pl.pallas_call(
        body,
        grid=(bs // per,),
        in_specs=[seq_spec(ext_rows, conv_dim), seq_spec(lp, LANES), seq_spec(LANES, lp), seq_spec(lp, d_inner),
                  seq_spec(d_inner, n_state)]
                 + [_const_spec(w.shape) for w in consts],
        out_specs=[seq_spec(lp, d_inner), seq_spec(d_inner, n_state)],
        out_shape=[jax.ShapeDtypeStruct((bs, lp, d_inner), bf16),
                   jax.ShapeDtypeStruct((bs, d_inner, n_state), f32)],
        compiler_params=_params(1),
        name="ssd_sample",
    )(ext, dt, dtt, z, state, *consts)


def _sattn_body(pt_ref, q_ref, kn_ref, vn_ref, bprev_ref, bcur_ref, ck_ref, cv_ref, o_ref,
                km_ref, ms_ref, ls_ref, os_ref, kbuf_ref, vbuf_ref, sem_ref, *, nbs, n_full, n_tok, ppb):
    n_pages = nbs * ppb
    g = pl.program_id(1)
    n_steps = pl.num_programs(1)
    n_rows = N_HEADS * SUBLANES
    d = ATTN_DIM

    step = pl.program_id(0) * n_steps + g
    total = pl.num_programs(0) * n_steps

    def page_copies(n):
        slot = n % PAGE_SLOTS
        seq = n // n_steps
        first = (n % n_steps) * n_pages
        out = []
        for t in range(n_pages):
            pid = pt_ref[seq, first + t]
            out.append(pltpu.make_async_copy(ck_ref.at[pid], kbuf_ref.at[slot, t], sem_ref.at[slot, t]))
            out.append(pltpu.make_async_copy(cv_ref.at[pid], vbuf_ref.at[slot, t], sem_ref.at[slot, n_pages + t]))
        return out

    @pl.when(step == 0)
    def _():
        for n in range(PAGE_SLOTS - 1):
            @pl.when(n < total)
            def _():
                for c in page_copies(n):
                    c.start()

    @pl.when(step + PAGE_SLOTS - 1 < total)
    def _():
        for c in page_copies(step + PAGE_SLOTS - 1):
            c.start()

    for c in page_copies(step):
        c.wait()
    slot = step % PAGE_SLOTS
    kp = [kbuf_ref.at[slot, t] for t in range(n_pages)]
    vp = [vbuf_ref.at[slot, t] for t in range(n_pages)]

    q4 = q_ref[...]
    q8 = jnp.concatenate([q4, jnp.zeros((SUBLANES - n_tok, d), f32)], axis=0)
    r_i = lax.broadcasted_iota(jnp.int32, (n_rows, d), 0)
    c_i = lax.broadcasted_iota(jnp.int32, (n_rows, d), 1)
    head_mask = (r_i // SUBLANES) == (c_i // HEAD_DIM)
    q_rows = jnp.where(head_mask, jnp.concatenate([q8] * N_HEADS, axis=0), 0.0)
    qb = q_rows.astype(bf16)

    @pl.when(g == 0)
    def _():
        km_ref[...] = jnp.zeros(km_ref.shape, f32)
        ms_ref[...] = jnp.zeros(ms_ref.shape, f32)
        ls_ref[...] = jnp.zeros(ls_ref.shape, f32)

    lane_blk = lax.broadcasted_iota(jnp.int32, (1, LANES), 1)
    sc_new = jnp.zeros((n_rows, LANES), f32)
    ms_new = jnp.zeros((n_rows, LANES), f32)
    ls_new = jnp.zeros((n_rows, LANES), f32)
    o_new = []
    k_t = jnp.concatenate([kp[t][...].reshape(d, -1).astype(bf16) for t in range(n_pages)], axis=1)
    s_all = _dot(qb, k_t)
    for b in range(nbs):
        vblk_t = jnp.concatenate([vp[b * ppb + t][...].reshape(d, -1) for t in range(ppb)], axis=1)
        jj = g * nbs + b
        s = s_all[:, b * MOBA_BLOCK:(b + 1) * MOBA_BLOCK]
        here = (lane_blk == jj).astype(f32)
        sc_new = sc_new + (jnp.sum(s, axis=-1, keepdims=True) * (1.0 / MOBA_BLOCK)) * here
        s = s + jnp.where(jj == n_full - 1, 1.0, 0.0).astype(f32) * bprev_ref[...]
        m = jnp.max(s, axis=-1, keepdims=True)
        p = jnp.exp(s - m)
        l = jnp.sum(p, axis=-1, keepdims=True)
        ms_new = ms_new + m * here
        ls_new = ls_new + l * here
        o_new.append(_dot_nt(p.astype(bf16), vblk_t.astype(bf16)))
    km_ref[...] += sc_new
    ms_ref[...] += ms_new
    ls_ref[...] += ls_new
    os_ref[pl.ds(pl.multiple_of(g * nbs, nbs), nbs)] = jnp.stack(o_new, axis=0)

    @pl.when(g == pl.num_programs(1) - 1)
    def _():
        sc = km_ref[...]
        col = lax.broadcasted_iota(jnp.int32, sc.shape, 1)
        low = jnp.float32(-3e38)
        sc = jnp.where(col < n_full, sc, low)
        self_ = jnp.zeros(sc.shape, f32)
        for _ in range(min(MOBA_TOPK, n_full)):
            mx = jnp.max(sc, axis=-1, keepdims=True)
            idx = jnp.min(jnp.where(sc == mx, col, LANES), axis=-1, keepdims=True)
            self_ = jnp.where(col == idx, 1.0, self_)
            sc = jnp.where(col == idx, low, sc)
        sel = self_ > 0.5
        kn = jnp.concatenate([kn_ref[...], jnp.zeros((SAMPLE_ROWS - n_tok, d), f32)], axis=0).astype(bf16)
        vn = jnp.concatenate([vn_ref[...], jnp.zeros((SAMPLE_ROWS - n_tok, d), f32)], axis=0).astype(bf16)
        s_cur = _dot_nt(qb, kn) + bcur_ref[...]
        ms = ms_ref[...]
        m_tot = jnp.maximum(jnp.max(s_cur, axis=-1, keepdims=True),
                            jnp.max(jnp.where(sel, ms, low), axis=-1, keepdims=True))
        p_cur = jnp.exp(s_cur - m_tot)
        w = jnp.where(sel, jnp.exp(ms - m_tot), 0.0)
        l_tot = jnp.sum(p_cur, axis=-1, keepdims=True) + jnp.sum(w * ls_ref[...], axis=-1, keepdims=True)
        acc = _dot(p_cur.astype(bf16), vn)
        for j in range(n_full):
            acc = acc + w[:, j:j + 1] * os_ref[j]
        out = jnp.where(head_mask, acc / l_tot, 0.0)
        out8 = out[0:SUBLANES]
        for h in range(1, N_HEADS):
            out8 = out8 + out[h * SUBLANES:(h + 1) * SUBLANES]
        o_ref[...] = out8[0:n_tok]


def _attn_sample(q, kn, vn, ck, cv, page_table, bprev, bcur):
    bs, n_tok, d = q.shape
    n_pool, n_heads, hd, page = ck.shape
    assert n_heads * hd == d
    ppb = MOBA_BLOCK // page
    n_pages_seq = page_table.shape[1]
    past = n_pages_seq * page
    assert MOBA_BLOCK % page == 0 and past % MOBA_BLOCK == 0 and n_tok <= SUBLANES
    n_full = past // MOBA_BLOCK
    assert n_full <= LANES
    nbs = math.gcd(n_full, SAMPLE_BLOCKS_PER_STEP)
    n_steps = n_full // nbs
    n_pages = nbs * ppb
    n_rows = N_HEADS * SUBLANES

    seq_spec = pl.BlockSpec((None, n_tok, d), lambda s, g, pt: (s, 0, 0))
    body = functools.partial(_sattn_body, nbs=nbs, n_full=n_full, n_tok=n_tok, ppb=ppb)
    grid_spec = pltpu.PrefetchScalarGridSpec(
        num_scalar_prefetch=1,
        grid=(bs, n_steps),
        in_specs=[seq_spec, seq_spec, seq_spec,
                  pl.BlockSpec(bprev.shape, lambda s, g, pt: (0, 0)),
                  pl.BlockSpec(bcur.shape, lambda s, g, pt: (0, 0)),
                  pl.BlockSpec(memory_space=pl.ANY), pl.BlockSpec(memory_space=pl.ANY)],
        out_specs=seq_spec,
        scratch_shapes=[pltpu.VMEM((n_rows, LANES), f32),
                        pltpu.VMEM((n_rows, LANES), f32),
                        pltpu.VMEM((n_rows, LANES), f32),
                        pltpu.VMEM((n_full, n_rows, d), f32),
                        pltpu.VMEM((PAGE_SLOTS, n_pages, n_heads, hd, page), f32),
                        pltpu.VMEM((PAGE_SLOTS, n_pages, n_heads, hd, page), f32),
                        pltpu.SemaphoreType.DMA((PAGE_SLOTS, 2 * n_pages))],
    )
    return pl.pallas_call(
        body,
        grid_spec=grid_spec,
        out_shape=jax.ShapeDtypeStruct((bs, n_tok, d), f32),
        compiler_params=_params(2),
        name="moba_sample",
    )(page_table, q, kn, vn, bprev, bcur, ck, cv)


def _merge_body(at_ref, yn_ref, ga_ref, gb_ref, x_ref, g1_ref, wpa_ref, wps_ref, wo_ref, lg_ref, lb_ref,
                o_ref, *, alpha):
    pa = _dot(at_ref[...].astype(bf16), wpa_ref[...])
    ps = _dot(yn_ref[...].astype(bf16), wps_ref[...])
    merged = jax.nn.sigmoid(ga_ref[...].astype(f32)) * pa + jax.nn.sigmoid(gb_ref[...].astype(f32)) * ps
    mo = _dot(merged.astype(bf16), wo_ref[...])
    o_ref[...] = _layer_norm(alpha * x_ref[...] + g1_ref[...] * mo, lg_ref[...], lb_ref[...])


def _mod_spec(per_row_mod, tm, d, tiles_per_seq):
    if per_row_mod:
        return pl.BlockSpec((tm, d), lambda i: (i, 0))
    return pl.BlockSpec((None, 1, d), lambda i: (i // tiles_per_seq, 0, 0))


def _merge(attn, yn, ga, gb, x2d, gate1, wpa, wps, wo, lg, lb, *, per_row_mod, seq_len, alpha):
    r, d = x2d.shape
    tm = 256
    assert r % tm == 0
    tiles_per_seq = 1 if per_row_mod else seq_len // tm
    ms = _mod_spec(per_row_mod, tm, d, tiles_per_seq)

    def row_spec(w):
        return pl.BlockSpec((tm, w), lambda i: (i, 0))

    return pl.pallas_call(
        functools.partial(_merge_body, alpha=alpha),
        grid=(r // tm,),
        in_specs=[row_spec(attn.shape[1]), row_spec(yn.shape[1]), row_spec(d), row_spec(d), row_spec(d), ms,
                  _const_spec(wpa.shape), _const_spec(wps.shape), _const_spec(wo.shape),
                  _const_spec(lg.shape), _const_spec(lb.shape)],
        out_specs=row_spec(d),
        out_shape=jax.ShapeDtypeStruct((r, d), f32),
        compiler_params=_params(1),
        name="merge_ln1",
    )(attn, yn, ga, gb, x2d, gate1, wpa, wps, wo, lg, lb)


FFN_COL_CHUNKS = 1


def _ffn_body(x_ref, sh_ref, sc_ref, g2_ref, wu_ref, wc_ref, bc_ref, wd_ref, lg_ref, lb_ref, *rest,
              alpha, conv_w, tiles_per_seq, sample_len):
    if sample_len is None:
        y_ref, tail_ref, buf_ref, carry_ref = rest
    else:
        p_refs = rest[:conv_w - 1]
        y_ref, hup_ref, buf_ref, carry_ref = rest[conv_w - 1:]
    x = x_ref[...]
    tm = x.shape[0]
    ff = wd_ref.shape[0]
    cw = ff // FFN_COL_CHUNKS
    t = pl.program_id(0) % tiles_per_seq

    @pl.when(t == 0)
    def _():
        carry_ref[...] = jnp.zeros(carry_ref.shape, f32)

    u = (x * (1.0 + sc_ref[...]) + sh_ref[...]).astype(bf16)
    if sample_len is not None:
        tmod = lax.broadcasted_iota(jnp.int32, (tm, cw), 0) % sample_len
    f = jnp.zeros((tm, x.shape[1]), f32)
    for c in range(FFN_COL_CHUNKS):
        halves = []
        for half in range(2):
            c0 = half * ff + c * cw
            buf_ref[0:SUBLANES, :] = carry_ref[:, c0:c0 + cw]
            hup = _dot(u, wu_ref[:, c0:c0 + cw])
            buf_ref[SUBLANES:SUBLANES + tm, :] = hup
            if sample_len is not None:
                hup_ref[:, c0:c0 + cw] = hup
            hc = bc_ref[:, c0:c0 + cw] + wc_ref[conv_w - 1:conv_w, c0:c0 + cw] * hup
            for k in range(conv_w - 1):
                back = conv_w - 1 - k
                prev = _rows_back(hup, buf_ref[0:tm, :], back)
                if sample_len is not None:
                    prev = jnp.where(tmod >= back, prev, p_refs[back - 1][:, c0:c0 + cw])
                hc = hc + wc_ref[k:k + 1, c0:c0 + cw] * prev
            carry_ref[:, c0:c0 + cw] = buf_ref[tm:tm + SUBLANES, :]
            halves.append(hc)
        gact = (_silu(halves[0]) * halves[1]).astype(bf16)
        f = f + _dot(gact, wd_ref[c * cw:(c + 1) * cw, :])
    y_ref[...] = _layer_norm(alpha * x + g2_ref[...] * f, lg_ref[...], lb_ref[...])
    if sample_len is None:
        @pl.when(t == tiles_per_seq - 1)
        def _():
            tail_ref[...] = carry_ref[...]


def _ffn(x2d, shift, scale, gate, wu, wc, bc, wd, lg, lb, prevs, *, per_row_mod, seq_len, bn, alpha,
         sample_len):
    r, d = x2d.shape
    ff2 = wu.shape[1]
    ff = wd.shape[0]
    conv_w = wc.shape[0]
    tm = 256 if sample_len is None else 128
    assert r % tm == 0 and ff % (FFN_COL_CHUNKS * LANES) == 0 and conv_w - 1 <= SUBLANES
    tiles_per_seq = 1 if per_row_mod else seq_len // tm
    ms = _mod_spec(per_row_mod, tm, d, tiles_per_seq)

    def row_spec(w):
        return pl.BlockSpec((tm, w), lambda i: (i, 0))

    in_specs = [row_spec(d), ms, ms, ms, _const_spec(wu.shape), _const_spec(wc.shape), _const_spec(bc.shape),
                _const_spec(wd.shape), _const_spec(lg.shape), _const_spec(lb.shape)]
    args = [x2d, shift, scale, gate, wu, wc, bc, wd, lg, lb]
    if sample_len is None:
        out_shape = [jax.ShapeDtypeStruct((r, d), f32), jax.ShapeDtypeStruct((bn, SUBLANES, ff2), f32)]
        out_specs = [row_spec(d), pl.BlockSpec((None, SUBLANES, ff2), lambda i: (i // tiles_per_seq, 0, 0))]
    else:
        assert tm % sample_len == 0 and len(prevs) == conv_w - 1
        in_specs += [row_spec(ff2)] * len(prevs)
        args += list(prevs)
        out_shape = [jax.ShapeDtypeStruct((r, d), f32), jax.ShapeDtypeStruct((r, ff2), f32)]
        out_specs = [row_spec(d), row_spec(ff2)]
    body = functools.partial(_ffn_body, alpha=alpha, conv_w=conv_w, tiles_per_seq=tiles_per_seq,
                             sample_len=sample_len)
    return pl.pallas_call(
        body,
        grid=(r // tm,),
        in_specs=in_specs,
        out_specs=out_specs,
        out_shape=out_shape,
        scratch_shapes=[pltpu.VMEM((tm + SUBLANES, ff // FFN_COL_CHUNKS), f32),
                        pltpu.VMEM((SUBLANES, ff2), f32)],
        compiler_params=_params(1),
        name="conv_ffn",
    )(*args)


def _rel_bucket(dist):
    n = jnp.maximum(dist, 0)
    max_exact = REL_BUCKETS // 2
    nf = jnp.maximum(n, 1).astype(f32)
    large = max_exact + (jnp.log(nf / max_exact) / math.log(REL_MAX_DIST / max_exact)
                         * (REL_BUCKETS - max_exact)).astype(jnp.int32)
    large = jnp.minimum(large, REL_BUCKETS - 1)
    return jnp.where(n < max_exact, n, large)


def _rel_bias(rel_table, dist):
    rel = rel_table - rel_table[REL_BUCKETS - 1]
    onehot = (_rel_bucket(dist)[..., None] == jnp.arange(REL_BUCKETS)).astype(f32)
    b = jnp.dot(onehot, rel, precision=lax.Precision.HIGHEST)
    return jnp.moveaxis(b, -1, 0)


def _prompt_bias_tables(rel_table):
    n = MOBA_BLOCK
    by_dist = _rel_bias(rel_table, jnp.arange(2 * n, dtype=jnp.int32)) * LOG2E
    masked = jnp.full((by_dist.shape[0], n - 1), MASKED, f32)
    bd = _toeplitz(jnp.concatenate([masked, by_dist[:, :n]], axis=1), n)
    bp = _toeplitz(by_dist[:, 1:], n)
    return bd.astype(f32), bp.astype(f32)


def _toeplitz(f, n):
    g = jnp.concatenate([f, jnp.zeros((f.shape[0], 1), f.dtype)], axis=1)
    y = jnp.tile(g, (1, n))[:, :n * (2 * n - 1)].reshape(f.shape[0], n, 2 * n - 1)
    return y[:, :, n - 1:]


def _sample_bias_tables(rel_table, n_tok):
    n_rows = N_HEADS * SUBLANES
    t = jnp.arange(SUBLANES, dtype=jnp.int32)
    a = jnp.arange(MOBA_BLOCK, dtype=jnp.int32)
    bprev = _rel_bias(rel_table, MOBA_BLOCK + t[:, None] - a[None, :])
    bprev = jnp.where((t < n_tok)[None, :, None], bprev, 0.0).reshape(n_rows, MOBA_BLOCK)
    tk = jnp.arange(SAMPLE_ROWS, dtype=jnp.int32)
    dist = t[:, None] - tk[None, :]
    ok = (dist >= 0) & (t[:, None] < n_tok) & (tk[None, :] < n_tok)
    bcur = jnp.where(ok[None], _rel_bias(rel_table, dist), MASKED).reshape(n_rows, SAMPLE_ROWS)
    return bprev.astype(f32), bcur.astype(f32)


def _pad_cols(w, n):
    return jnp.pad(w, ((0, 0), (0, n - w.shape[1])))


def kernel(x_prompt, x_sample, cache_k, cache_v, page_table, state_ssm, state_conv_ssm, state_conv_ffn,
           c_prompt, c_sample, rel_table, w_ada, b_ada, w_in, w_conv_ssm, b_conv_ssm, dt_bias, a_log,
           d_skip, w_norm_ssm, w_proj_attn, w_proj_ssm, w_out, ln1_g, ln1_b, w_up, w_conv_ffn, b_conv_ffn,
           w_down, ln2_g, ln2_b):
    depth = w_ada.shape[0]
    alpha = (2 * depth) ** 0.25
    bp_, seq, d = x_prompt.shape
    bs, n_tok, _ = x_sample.shape
    ssm_heads = dt_bias.shape[1]
    d_inner = ssm_heads * SSM_HEAD_DIM
    n_state = state_ssm.shape[-1]
    conv_dim = w_conv_ssm.shape[-1]
    ff2 = w_up.shape[-1]
    ssm_conv = w_conv_ssm.shape[1]
    ffn_conv = w_conv_ffn.shape[1]
    assert ssm_heads <= LANES and n_tok <= SUBLANES
    page = cache_k.shape[2]
    scale = HEAD_DIM ** -0.5

    bd, bpv = _prompt_bias_tables(rel_table)
    bprev_s, bcur_s = _sample_bias_tables(rel_table, n_tok)
    expand = (jnp.arange(LANES)[:, None] == (jnp.arange(d_inner)[None, :] // SSM_HEAD_DIM)).astype(bf16)

    yp = x_prompt.reshape(bp_ * seq, d)
    ys = x_sample.reshape(bs * n_tok, d)
    outs_p = [[] for _ in range(5)]
    outs_s = [[] for _ in range(5)]
    for l in range(depth):
        cuts = [ATTN_DIM, 2 * ATTN_DIM, 3 * ATTN_DIM, 3 * ATTN_DIM + d_inner,
                3 * ATTN_DIM + d_inner + conv_dim, 3 * ATTN_DIM + d_inner + conv_dim + ssm_heads,
                3 * ATTN_DIM + d_inner + conv_dim + ssm_heads + d]
        wq, wk, wv, wz, wxbc, wdt, wga, wgb = jnp.split(w_in[l], cuts, axis=1)
        wdt = _pad_cols(wdt, LANES)
        wq = wq * scale
        nat_rest = [(d_inner, bf16), (conv_dim, f32), (LANES, f32), (d, bf16), (d, bf16)]
        w_rest = [wz, wxbc, wdt, wga, wgb]
        nat_p = [(ATTN_DIM, None)] + nat_rest
        wn_p = jnp.concatenate([wk] + w_rest, axis=1).astype(bf16)
        a_ = ATTN_DIM
        tr_p = [(0, a_, bf16, 0, False), (a_, a_, f32, 0, True), (2 * a_, a_, f32, 0, True),
                (2 * a_, a_, bf16, BF16_ROWS, False), (3 * a_, LANES, f32, 0, False)]
        wt_p = jnp.concatenate([wq * LOG2E, wk, wv, wdt], axis=1).T.astype(bf16)
        nat_s = [(ATTN_DIM, f32)] * 3 + nat_rest
        wn_s = jnp.concatenate([wq, wk, wv] + w_rest, axis=1).astype(bf16)
        tr_s = [(0, LANES, f32, 0, False)]
        wt_s = wdt.T.astype(bf16)
        ssm_w = (w_conv_ssm[l], b_conv_ssm[l].reshape(1, conv_dim),
                 _pad_cols(dt_bias[l].reshape(1, -1), LANES), _pad_cols(dt_bias[l].reshape(1, -1), LANES).T,
                 -jnp.exp(_pad_cols(a_log[l].reshape(1, -1), LANES)),
                 -jnp.exp(_pad_cols(a_log[l].reshape(1, -1), LANES)).T,
                 expand, jnp.repeat(d_skip[l], SSM_HEAD_DIM).reshape(1, d_inner),
                 w_norm_ssm[l].reshape(1, d_inner))
        wpa, wps, wo = (w_proj_attn[l].astype(bf16), w_proj_ssm[l].astype(bf16), w_out[l].astype(bf16))
        lg1, lb1 = ln1_g[l].reshape(1, d), ln1_b[l].reshape(1, d)
        lg2, lb2 = ln2_g[l].reshape(1, d), ln2_b[l].reshape(1, d)
        wu, wd = w_up[l].astype(bf16), w_down[l].astype(bf16)
        wcf, bcf = w_conv_ffn[l], b_conv_ffn[l].reshape(1, ff2)

        mod = _ada(jnp.concatenate([c_prompt, c_sample], axis=0), w_ada[l], b_ada[l])
        mod_p = mod[:bp_].reshape(bp_, 6, 1, d)
        mod_s = jnp.repeat(mod[bp_:].reshape(bs, 6, 1, d), n_tok, axis=2).reshape(bs, 6, n_tok, d)
        mod_s = jnp.moveaxis(mod_s, 1, 0).reshape(6, bs * n_tok, d)
        sh1p, sc1p, g1p, sh2p, sc2p, g2p = (mod_p[:, i] for i in range(6))
        sh1s, sc1s, g1s, sh2s, sc2s, g2s = (mod_s[i] for i in range(6))

        z_p, xbc_p, dt_p, ga_p, gb_p, qT_p, kT_p, vTf_p, vT_p, dtT_p, k2_p = _inproj(
            yp, sh1p, sc1p, wn_p, wt_p, nat_p, tr_p, per_row_mod=False, seq_len=seq, attn_layout=True)
        attn_p = _attn_prompt(k2_p, qT_p, vT_p, bd, bpv, bn=bp_, seq_len=seq)
        yn_p, st_p = _ssd_prompt(xbc_p, dt_p, dtT_p, z_p, ssm_w, bn=bp_, seq_len=seq, n_state=n_state)
        x1_p = _merge(attn_p, yn_p, ga_p, gb_p, yp, g1p, wpa, wps, wo, lg1, lb1,
                      per_row_mod=False, seq_len=seq, alpha=alpha)
        yp, tail_p = _ffn(x1_p, sh2p, sc2p, g2p, wu, wcf, bcf, wd, lg2, lb2, (),
                          per_row_mod=False, seq_len=seq, bn=bp_, alpha=alpha, sample_len=None)
        outs_p[0].append(jnp.transpose(kT_p.reshape(bp_, N_HEADS, HEAD_DIM, seq), (0, 3, 1, 2)))
        outs_p[1].append(jnp.transpose(vTf_p.reshape(bp_, N_HEADS, HEAD_DIM, seq), (0, 3, 1, 2)))
        outs_p[2].append(st_p.reshape(bp_, ssm_heads, SSM_HEAD_DIM, n_state))
        outs_p[3].append(xbc_p.reshape(bp_, seq, conv_dim)[:, seq - (ssm_conv - 1):])
        outs_p[4].append(tail_p[:, SUBLANES - (ffn_conv - 1):])

        r_s = bs * n_tok
        r_pad = -(-r_s // 256) * 256

        def pad_rows(a):
            return jnp.pad(a, ((0, r_pad - r_s), (0, 0)))

        q_s, k_s, v_s, z_s, xbc_s, dt_s, ga_s, gb_s, dtT_s = _inproj(
            pad_rows(ys), pad_rows(sh1s), pad_rows(sc1s), wn_s, wt_s, nat_s, tr_s,
            per_row_mod=True, seq_len=None, attn_layout=False)
        q_s, k_s, v_s = (a[:r_s].reshape(bs, n_tok, ATTN_DIM) for a in (q_s, k_s, v_s))
        attn_s = _attn_sample(q_s, k_s, v_s, jnp.transpose(cache_k[l], (0, 2, 3, 1)),
                              jnp.transpose(cache_v[l], (0, 2, 3, 1)), page_table, bprev_s, bcur_s)
        row_pad = SAMPLE_ROWS - n_tok
        xbc_s3 = xbc_s[:r_s].reshape(bs, n_tok, conv_dim)
        ext = jnp.concatenate([state_conv_ssm[l], xbc_s3,
                               jnp.zeros((bs, row_pad + SUBLANES - (ssm_conv - 1), conv_dim), f32)], axis=1)
        dt_s3 = jnp.pad(dt_s[:r_s].reshape(bs, n_tok, LANES), ((0, 0), (0, row_pad), (0, 0)))
        dtT_s3 = jnp.pad(jnp.moveaxis(dtT_s[:, :r_s].reshape(LANES, bs, n_tok), 0, 1),
                         ((0, 0), (0, 0), (0, row_pad)))
        z_s3 = jnp.pad(z_s[:r_s].reshape(bs, n_tok, d_inner), ((0, 0), (0, row_pad), (0, 0)))
        yn_s3, st_s = _ssd_sample(ext, dt_s3, dtT_s3, z_s3, state_ssm[l].reshape(bs, d_inner, n_state),
                                  ssm_w, n_valid=n_tok)
        yn_s = yn_s3[:, :n_tok].reshape(r_s, d_inner)
        x1_s = _merge(pad_rows(attn_s.reshape(r_s, ATTN_DIM)), pad_rows(yn_s), ga_s, gb_s, pad_rows(ys),
                      pad_rows(g1s), wpa, wps, wo, lg1, lb1, per_row_mod=True, seq_len=None, alpha=alpha)
        cf = state_conv_ffn[l]
        prevs = []
        for back in range(1, ffn_conv):
            rows = [cf[:, ffn_conv - 1 - back + t] if t < back else jnp.zeros((bs, ff2), f32)
                    for t in range(n_tok)]
            prevs.append(pad_rows(jnp.stack(rows, axis=1).reshape(r_s, ff2)))
        y_s, hup_s = _ffn(x1_s, pad_rows(sh2s), pad_rows(sc2s), pad_rows(g2s), wu, wcf, bcf, wd, lg2, lb2,
                          prevs, per_row_mod=True, seq_len=None, bn=bs, alpha=alpha, sample_len=n_tok)
        ys = y_s[:r_s]
        outs_s[0].append(k_s.reshape(bs, n_tok, N_HEADS, HEAD_DIM))
        outs_s[1].append(v_s.reshape(bs, n_tok, N_HEADS, HEAD_DIM))
        outs_s[2].append(st_s.reshape(bs, ssm_heads, SSM_HEAD_DIM, n_state))
        cs_ext = jnp.concatenate([state_conv_ssm[l], xbc_s3], axis=1)
        outs_s[3].append(cs_ext[:, -(ssm_conv - 1):])
        cf_ext = jnp.concatenate([cf, hup_s[:r_s].reshape(bs, n_tok, ff2)], axis=1)
        outs_s[4].append(cf_ext[:, -(ffn_conv - 1):])

    return (yp.reshape(bp_, seq, d), ys.reshape(bs, n_tok, d),
            *(jnp.stack(o) for o in outs_p), *(jnp.stack(o) for o in outs_s))
```

```python
import functools
import math

import jax
import jax.numpy as jnp
from jax import lax
from jax.experimental import pallas as pl
from jax.experimental.pallas import tpu as pltpu

N_HEADS = 8
HEAD_DIM = 64
ATTN_DIM = N_HEADS * HEAD_DIM
MOBA_BLOCK = 256
MOBA_TOPK = 3
REL_BUCKETS = 32
REL_MAX_DIST = 128
SSM_HEAD_DIM = 64
SSM_CHUNK = 256
EPS = 1e-5

LANES = 128
SUBLANES = 8
BF16_ROWS = 16
VMEM_LIMIT = 56 * 1024 * 1024

MASKED = -1e30
SAMPLE_ROWS = 16
SAMPLE_BLOCKS_PER_STEP = 4
PAGE_SLOTS = 4
SSD_SAMPLE_SEQS = 2
FAR_GROUP = 4
QUERY_BLOCKS_PER_STEP = 4
LOG2E = math.log2(math.e)

f32 = jnp.float32
bf16 = jnp.bfloat16

_NT = (((1,), (1,)), ((), ()))
_TN = (((0,), (0,)), ((), ()))


def _dot(a, b):
    return jnp.dot(a, b, preferred_element_type=f32)


def _dot_nt(a, b):
    return lax.dot_general(a, b, _NT, preferred_element_type=f32)


def _dot_tn(a, b):
    return lax.dot_general(a, b, _TN, preferred_element_type=f32)


def _split3(x):
    hi = x.astype(bf16)
    r = x - hi.astype(f32)
    mid = r.astype(bf16)
    lo = (r - mid.astype(f32)).astype(bf16)
    return hi, mid, lo


def _dot_exact_rhs(a, m):
    return _dot(jnp.concatenate(_split3(a), axis=1), jnp.concatenate([m, m, m], axis=0))


def _dot_exact_lhs(m, a):
    return _dot(jnp.concatenate([m, m, m], axis=1), jnp.concatenate(_split3(a), axis=0))


def _rows_back(cur, before, back):
    rows, cols = cur.shape
    sub = lax.broadcasted_iota(jnp.int32, cur.shape, 0) % SUBLANES
    mixed = jnp.where(sub >= SUBLANES - back, before, cur)
    return pltpu.roll(mixed.reshape(rows // SUBLANES, SUBLANES, cols), back, axis=1).reshape(rows, cols)


def _silu(x):
    return x * jax.nn.sigmoid(x)


def _softplus(x):
    return jnp.maximum(x, 0.0) + jnp.log1p(jnp.exp(-jnp.abs(x)))


def _layer_norm(x, g, b):
    mu = jnp.mean(x, axis=-1, keepdims=True)
    xc = x - mu
    var = jnp.mean(xc * xc, axis=-1, keepdims=True)
    return xc * lax.rsqrt(var + EPS) * g + b


def _const_spec(shape):
    nd = len(shape)
    return pl.BlockSpec(shape, lambda *_: (0,) * nd, pipeline_mode=pl.Buffered(1))


def _params(n_grid, flags=None):
    return pltpu.CompilerParams(
        dimension_semantics=("arbitrary",) * n_grid, vmem_limit_bytes=VMEM_LIMIT, flags=flags)


def _ada_body(c_ref, w_ref, b_ref, o_ref):
    a = _silu(c_ref[...])
    hi, mid, lo = _split3(a)
    whi, wmid, wlo = _split3(w_ref[...])
    acc = _dot(hi, whi) + (_dot(hi, wmid) + _dot(mid, whi))
    acc = acc + (_dot(hi, wlo) + _dot(mid, wmid) + _dot(lo, whi))
    o_ref[...] = acc + b_ref[...]


def _ada(c, w, b):
    n, d = c.shape
    dn = w.shape[1]
    tn = 1024 if dn % 1024 == 0 else dn
    return pl.pallas_call(
        _ada_body,
        grid=(dn // tn,),
        in_specs=[pl.BlockSpec((n, d), lambda j: (0, 0)),
                  pl.BlockSpec((d, tn), lambda j: (0, j)),
                  pl.BlockSpec((1, tn), lambda j: (0, j))],
        out_specs=pl.BlockSpec((n, tn), lambda j: (0, j)),
        out_shape=jax.ShapeDtypeStruct((n, dn), f32),
        compiler_params=_params(1),
        name="ada_mod",
    )(c, w, b.reshape(1, dn))


def _inproj_body(x_ref, sh_ref, sc_ref, wn_ref, wt_ref, *outs, nat, tr, k_col, tiles_per_seq):
    u = (x_ref[...] * (1.0 + sc_ref[...]) + sh_ref[...]).astype(bf16)
    tm = u.shape[0]
    o = 0
    col = 0
    for width, dt in nat:
        res = _dot(u, wn_ref[:, col:col + width])
        if k_col is not None and col == k_col:
            kf = res
        if dt is not None:
            outs[o][...] = res.astype(dt)
            o += 1
        col += width
    if tr:
        t = _dot_nt(wt_ref[...], u)
        for row, height, _, ones_rows, _ in tr:
            if ones_rows:
                step = HEAD_DIM + ones_rows
                for h in range(height // HEAD_DIM):
                    outs[o][h * step:h * step + HEAD_DIM, :] = (
                        t[row + h * HEAD_DIM:row + (h + 1) * HEAD_DIM].astype(outs[o].dtype))
                    outs[o][h * step + HEAD_DIM:(h + 1) * step, :] = jnp.ones((ones_rows, tm), outs[o].dtype)
            else:
                outs[o][...] = t[row:row + height].astype(outs[o].dtype)
            o += 1
    if k_col is not None:
        pos = (pl.program_id(0) % tiles_per_seq) * tm
        blk = pos // MOBA_BLOCK
        onehot = (lax.broadcasted_iota(jnp.int32, (tm, LANES), 1) == blk).astype(bf16)
        k2_ref = outs[o]
        for p in range(ATTN_DIM // LANES):
            k2_ref[p] = jnp.concatenate([kf[:, p * LANES:(p + 1) * LANES].astype(bf16), onehot], axis=1)


def _inproj(x2d, shift, scale, wn, wt, nat, tr, *, per_row_mod, seq_len, attn_layout):
    r, d = x2d.shape
    tm = 256
    assert r % tm == 0
    n_tiles = r // tm
    if per_row_mod:
        mod_spec = pl.BlockSpec((tm, d), lambda i: (i, 0))
        tiles_per_seq = 1
    else:
        assert seq_len % tm == 0 and MOBA_BLOCK % tm == 0
        tiles_per_seq = seq_len // tm
        mod_spec = pl.BlockSpec((None, 1, d), lambda i: (i // tiles_per_seq, 0, 0))
    out_shape, out_specs = [], []
    for width, dt in nat:
        if dt is not None:
            out_shape.append(jax.ShapeDtypeStruct((r, width), dt))
            out_specs.append(pl.BlockSpec((tm, width), lambda i: (i, 0)))
    for _, height, dt, ones_rows, per_seq in tr:
        rows = height + (height // HEAD_DIM) * ones_rows
        if per_seq:
            out_shape.append(jax.ShapeDtypeStruct((r // seq_len, rows, seq_len), dt))
            out_specs.append(pl.BlockSpec((None, rows, tm), lambda i: (i // tiles_per_seq, 0, i % tiles_per_seq)))
        else:
            out_shape.append(jax.ShapeDtypeStruct((rows, r), dt))
            out_specs.append(pl.BlockSpec((rows, tm), lambda i: (0, i)))
    k_col = None
    if attn_layout:
        k_col = 0
        n_pairs = ATTN_DIM // LANES
        out_shape.append(jax.ShapeDtypeStruct((n_pairs, r, 2 * LANES), bf16))
        out_specs.append(pl.BlockSpec((n_pairs, tm, 2 * LANES), lambda i: (0, i, 0)))
    body = functools.partial(_inproj_body, nat=tuple(nat), tr=tuple(tr), k_col=k_col,
                             tiles_per_seq=tiles_per_seq)
    return pl.pallas_call(
        body,
        grid=(n_tiles,),
        in_specs=[pl.BlockSpec((tm, d), lambda i: (i, 0)), mod_spec, mod_spec,
                  _const_spec(wn.shape), _const_spec(wt.shape)],
        out_specs=out_specs,
        out_shape=out_shape,
        compiler_params=_params(1),
        name="in_proj",
    )(x2d, shift, scale, wn, wt)


def _select_topk_rows(s, n_valid_rows):
    nblk = s.shape[0]
    row = lax.broadcasted_iota(jnp.int32, s.shape, 0)
    low = jnp.float32(-3e38)
    s = jnp.where(row < n_valid_rows, s, low)
    sel = jnp.zeros(s.shape, jnp.bool_)
    for _ in range(MOBA_TOPK):
        m = jnp.max(s, axis=0, keepdims=True)
        idx = jnp.min(jnp.where(s == m, row, nblk), axis=0, keepdims=True)
        hit = (row == idx) & (m > low)
        sel = sel | hit
        s = jnp.where(row == idx, low, s)
    return sel


def _attn_body(k2_ref, qT_ref, vT_ref, bd_ref, bp_ref, o_ref, km_ref, *bufs, n_blocks, group, per):
    step = pl.program_id(2)
    blk = MOBA_BLOCK
    hd = HEAD_DIM
    nsel = km_ref.shape[0]

    @pl.when(step == 0)
    def _():
        km_ref[...] = jnp.zeros(km_ref.shape, f32)

        def mean_body(j, c):
            kk = k2_ref[pl.ds(pl.multiple_of(j * blk, blk), blk), :].astype(f32)
            km_ref[pl.ds(j, 1), :] = jnp.sum(kk, axis=0, keepdims=True) * (1.0 / blk)
            return c
        lax.fori_loop(0, n_blocks, mean_body, 0)

    km = km_ref[...]
    km_hi = km.astype(bf16)
    km_lo = (km - km_hi.astype(f32)).astype(bf16)
    vrows = vT_ref.shape[0] // 2
    last_group = n_blocks // group - 1

    def k_rows(j, n):
        return k2_ref[pl.ds(pl.multiple_of(j * blk, blk), n * blk), :]

    def v_cols(h, j, n):
        return vT_ref[h * vrows:(h + 1) * vrows, pl.ds(pl.multiple_of(j * blk, blk), n * blk)]

    def logits_into(s_ref, q2f_ref, g):
        s_ref[...] = _dot(k_rows(jnp.minimum(g, last_group) * group, group), q2f_ref[...])

    def start(i, q2n_ref, q2f_ref, sa_ref):
        q_pair = qT_ref[:, pl.ds(pl.multiple_of(i * blk, blk), blk)]
        zq = jnp.zeros((hd, blk), bf16)
        q_rows = jnp.concatenate([jnp.concatenate([q_pair[0:hd], zq], axis=1),
                                  jnp.concatenate([zq, q_pair[hd:2 * hd]], axis=1)], axis=0)
        zrest = jnp.zeros((q2f_ref.shape[0] - 2 * hd, 2 * blk), bf16)
        q2f_ref[0:2 * hd, :] = q_rows
        q2f_ref[2 * hd:, :] = zrest
        q2n_ref[0:2 * hd, :] = q_rows
        q2n_ref[2 * hd:, :] = zrest
        q2 = q2f_ref[...]
        s_blk = _dot(km_hi, q2) + _dot(km_lo, q2)
        sel = _select_topk_rows(s_blk, i)
        rowsel = lax.broadcasted_iota(jnp.int32, (nsel, 2 * blk), 0)
        far = sel & (rowsel < i - 1)
        near = (sel & (rowsel == i - 1)) | (rowsel == i)
        q2f_ref[2 * hd:2 * hd + nsel, :] = jnp.where(far, 0.0, MASKED).astype(bf16)
        q2n_ref[2 * hd:2 * hd + nsel, :] = jnp.where(near, 0.0, MASKED).astype(bf16)

        jp = jnp.maximum(i - 1, 0)
        first_pad = jnp.where(i > 0, 0.0, MASKED).astype(f32)
        s_cur = _dot(k_rows(i, 1), q2n_ref[...])
        s_prev = _dot(k_rows(jp, 1), q2n_ref[...])
        carry = []
        for h in range(2):
            s = jnp.concatenate([s_cur[:, h * blk:(h + 1) * blk] + bd_ref[h],
                                 s_prev[:, h * blk:(h + 1) * blk] + (bp_ref[h] + first_pad)], axis=0)
            m = jnp.max(s, axis=0, keepdims=True)
            p = jnp.exp2((s - m).astype(bf16))
            v = jnp.concatenate([v_cols(h, i, 1), v_cols(h, jp, 1)], axis=1)
            carry += [m, _dot(v, p)]
        logits_into(sa_ref, q2f_ref, 0)
        return carry

    def finish(i, q2f_ref, sa_ref, sb_ref, carry):
        n_groups = (jnp.maximum(i - 1, 0) + group - 1) // group

        def consume(s_ref, g, carry):
            j0 = jnp.minimum(g, last_group) * group
            pad = jnp.where(g < n_groups, 0.0, MASKED).astype(f32)
            out = []
            for h in range(2):
                m, acc = carry[2 * h:2 * h + 2]
                m2 = jnp.maximum(m, jnp.max(s_ref[:, h * blk:(h + 1) * blk], axis=0, keepdims=True) + pad)
                a = jnp.exp2(m - m2)
                p = jnp.exp2((s_ref[:, h * blk:(h + 1) * blk] - (m2 - pad)).astype(bf16))
                out += [m2, a * acc + _dot(v_cols(h, j0, group), p)]
            return out

        def far_body(t, carry):
            logits_into(sb_ref, q2f_ref, 2 * t + 1)
            carry = consume(sa_ref, 2 * t, list(carry))
            logits_into(sa_ref, q2f_ref, 2 * t + 2)
            return tuple(consume(sb_ref, 2 * t + 1, carry))

        carry = lax.fori_loop(0, (n_groups + 1) // 2, far_body, tuple(carry))
        outs = [carry[2 * h + 1][0:hd] / carry[2 * h + 1][hd:hd + 1] for h in range(2)]
        return jnp.concatenate(outs, axis=0).T

    sets = [bufs[4 * u:4 * u + 4] for u in range(per)]
    blocks = [step * per + u for u in range(per)]
    carries = [start(blocks[u], *sets[u][:3]) for u in range(per)]
    outs = [finish(blocks[u], *sets[u][1:], carries[u]) for u in range(per)]
    o_ref[...] = jnp.concatenate(outs, axis=0).astype(o_ref.dtype)


def _attn_prompt(k2, qT, vT, bd, bp, *, bn, seq_len):
    n_pairs, r, _ = k2.shape
    blk = MOBA_BLOCK
    assert seq_len % blk == 0
    nb = seq_len // blk
    nsel = HEAD_DIM
    assert nb <= nsel
    group = math.gcd(nb, FAR_GROUP)
    per = math.gcd(nb, QUERY_BLOCKS_PER_STEP)
    steps = nb // per
    body = functools.partial(_attn_body, n_blocks=nb, group=group, per=per)
    per_block = [pltpu.VMEM((2 * LANES, 2 * blk), bf16), pltpu.VMEM((2 * LANES, 2 * blk), bf16),
                 pltpu.VMEM((group * blk, 2 * blk), f32), pltpu.VMEM((group * blk, 2 * blk), f32)]
    return pl.pallas_call(
        body,
        grid=(bn, n_pairs, steps),
        in_specs=[pl.BlockSpec((None, seq_len, 2 * LANES), lambda b, p, i: (p, b, 0)),
                  pl.BlockSpec((LANES, seq_len), lambda b, p, i: (p, b)),
                  pl.BlockSpec((vT.shape[0] // n_pairs, seq_len), lambda b, p, i: (p, b)),
                  pl.BlockSpec((2, blk, blk), lambda b, p, i: (p, 0, 0)),
                  pl.BlockSpec((2, blk, blk), lambda b, p, i: (p, 0, 0))],
        out_specs=pl.BlockSpec((per * blk, LANES), lambda b, p, i: (b * steps + i, p)),
        out_shape=jax.ShapeDtypeStruct((r, ATTN_DIM), bf16),
        scratch_shapes=[pltpu.VMEM((nsel, 2 * LANES), f32)] + per_block * per,
        compiler_params=_params(3),
        name="moba_prompt",
    )(k2, qT, vT, bd, bp)


def _ssd_chunk(conv, dt_raw, dt_raw_t, z, st_ref, dtb, dtb_t, a_row, a_col, expand, dskip, wnorm,
               *, n_valid):
    l = conv.shape[0]
    d_inner = z.shape[1]
    n_state = st_ref.shape[1]
    n_heads = d_inner // SSM_HEAD_DIM
    n_groups = (conv.shape[1] - d_inner) // (2 * n_state)
    hpg = n_heads // n_groups
    gw = hpg * SSM_HEAD_DIM
    last = (l if n_valid is None else n_valid) - 1

    act = _silu(conv)
    xs = act[:, :d_inner]
    bm = act[:, d_inner:d_inner + n_groups * n_state]
    cm = act[:, d_inner + n_groups * n_state:]

    dt = _softplus(dt_raw + dtb)
    dt_t = _softplus(dt_raw_t + dtb_t)
    r_i = lax.broadcasted_iota(jnp.int32, (l, l), 0)
    c_i = lax.broadcasted_iota(jnp.int32, (l, l), 1)
    causal = r_i >= c_i
    tril = causal.astype(bf16)
    triu = (r_i <= c_i).astype(bf16)
    acs = _dot_exact_lhs(tril, dt * a_row)
    acs_t = _dot_exact_rhs(dt_t * a_col, triu)
    eacs = jnp.exp(acs)
    dec = jnp.exp(acs[last:last + 1, :] - acs)
    ea_t = jnp.exp(acs_t)

    dt_full = _dot_exact_rhs(dt, expand)
    eacs_full = _dot_exact_rhs(eacs, expand)
    dec_full = _dot_exact_rhs(dec, expand)
    xd = xs * dt_full
    xdd = xd * dec_full
    if n_valid is not None:
        rows = lax.broadcasted_iota(jnp.int32, xdd.shape, 0)
        xdd = jnp.where(rows < n_valid, xdd, 0.0)

    lane = lax.broadcasted_iota(jnp.int32, (l, 2 * SSM_HEAD_DIM), 1)
    y_parts = []
    for g in range(n_groups):
        bg = bm[:, g * n_state:(g + 1) * n_state].astype(bf16)
        cg = cm[:, g * n_state:(g + 1) * n_state].astype(bf16)
        cb = _dot_nt(cg, bg)
        st_g = st_ref[g * gw:(g + 1) * gw, :]
        y_inter = _dot_nt(cg, st_g.astype(bf16)) * eacs_full[:, g * gw:(g + 1) * gw]
        pair_parts = []
        for q in range(hpg // 2):
            h0 = g * hpg + 2 * q
            xdp = xd[:, h0 * SSM_HEAD_DIM:(h0 + 2) * SSM_HEAD_DIM].astype(bf16)
            res = []
            for h in (h0, h0 + 1):
                seg = acs[:, h:h + 1] - acs_t[h:h + 1, :]
                lm = jnp.exp(jnp.where(causal, seg, MASKED))
                res.append(_dot((cb * lm).astype(bf16), xdp))
            pair_parts.append(jnp.where(lane < SSM_HEAD_DIM, res[0], res[1]))
        y_parts.append(jnp.concatenate(pair_parts, axis=1) + y_inter)
        upd = _dot_tn(xdd[:, g * gw:(g + 1) * gw].astype(bf16), bg)
        for hl in range(hpg):
            h = g * hpg + hl
            rs = slice(g * gw + hl * SSM_HEAD_DIM, g * gw + (hl + 1) * SSM_HEAD_DIM)
            st_ref[rs, :] = (st_g[hl * SSM_HEAD_DIM:(hl + 1) * SSM_HEAD_DIM] * ea_t[h:h + 1, last:last + 1]
                             + upd[hl * SSM_HEAD_DIM:(hl + 1) * SSM_HEAD_DIM])
    y = jnp.concatenate(y_parts, axis=1) + dskip * xs
    y = y * _silu(z.astype(f32))
    normed = []
    for g in range(n_groups):
        yg = y[:, g * gw:(g + 1) * gw]
        normed.append(yg * lax.rsqrt(jnp.mean(yg * yg, axis=-1, keepdims=True) + EPS))
    return jnp.concatenate(normed, axis=1) * wnorm


def _ssd_prompt_body(xbc_ref, dt_ref, dtt_ref, z_ref, wc_ref, bc_ref, dtb_ref, dtbt_ref, a_ref, at_ref,
                     ex_ref, dk_ref, wn_ref, y_ref, st_ref, buf_ref, *, conv_w):
    c = pl.program_id(1)
    l = xbc_ref.shape[0]

    @pl.when(c == 0)
    def _():
        buf_ref[0:SUBLANES, :] = jnp.zeros((SUBLANES, buf_ref.shape[1]), f32)
        st_ref[...] = jnp.zeros(st_ref.shape, f32)

    cur = xbc_ref[...]
    buf_ref[SUBLANES:SUBLANES + l, :] = cur
    before = buf_ref[0:l, :]
    conv = bc_ref[...] + wc_ref[conv_w - 1:conv_w, :] * cur
    for back in range(1, conv_w):
        conv = conv + wc_ref[conv_w - 1 - back:conv_w - back, :] * _rows_back(cur, before, back)
    buf_ref[0:SUBLANES, :] = buf_ref[l:l + SUBLANES, :]
    y = _ssd_chunk(conv, dt_ref[...], dtt_ref[...], z_ref[...], st_ref, dtb_ref[...], dtbt_ref[...],
                   a_ref[...], at_ref[...], ex_ref[...], dk_ref[...], wn_ref[...], n_valid=None)
    y_ref[...] = y.astype(y_ref.dtype)


def _ssd_prompt(xbc, dt, dtt, z, ssm_w, *, bn, seq_len, n_state):
    r, conv_dim = xbc.shape
    d_inner = z.shape[1]
    l = math.gcd(seq_len, SSM_CHUNK)
    assert l % LANES == 0
    nc = seq_len // l
    wc, bc, dtb, dtbt, a_row, a_col, expand, dskip, wnorm = ssm_w
    conv_w = wc.shape[0]
    assert conv_w - 1 <= SUBLANES
    body = functools.partial(_ssd_prompt_body, conv_w=conv_w)
    consts = [wc, bc, dtb, dtbt, a_row, a_col, expand, dskip, wnorm]
    return pl.pallas_call(
        body,
        grid=(bn, nc),
        in_specs=[pl.BlockSpec((l, conv_dim), lambda b, c: (b * nc + c, 0)),
                  pl.BlockSpec((l, LANES), lambda b, c: (b * nc + c, 0)),
                  pl.BlockSpec((LANES, l), lambda b, c: (0, b * nc + c)),
                  pl.BlockSpec((l, d_inner), lambda b, c: (b * nc + c, 0))]
                 + [_const_spec(w.shape) for w in consts],
        out_specs=[pl.BlockSpec((l, d_inner), lambda b, c: (b * nc + c, 0)),
                   pl.BlockSpec((None, d_inner, n_state), lambda b, c: (b, 0, 0))],
        out_shape=[jax.ShapeDtypeStruct((r, d_inner), bf16),
                   jax.ShapeDtypeStruct((bn, d_inner, n_state), f32)],
        scratch_shapes=[pltpu.VMEM((l + SUBLANES, conv_dim), f32)],
        compiler_params=_params(2),
        name="ssd_prompt",
    )(xbc, dt, dtt, z, *consts)


def _ssd_sample_body(ext_ref, dt_ref, dtt_ref, z_ref, st_in_ref, wc_ref, bc_ref, dtb_ref, dtbt_ref, a_ref,
                     at_ref, ex_ref, dk_ref, wn_ref, y_ref, st_ref, *, conv_w, n_valid):
    lp = dt_ref.shape[1]
    st_ref[...] = st_in_ref[...]
    for u in range(dt_ref.shape[0]):
        conv = bc_ref[...]
        for k in range(conv_w):
            conv = conv + wc_ref[k:k + 1, :] * ext_ref[u, k:k + lp, :]
        y = _ssd_chunk(conv, dt_ref[u], dtt_ref[u], z_ref[u], st_ref.at[u], dtb_ref[...], dtbt_ref[...],
                       a_ref[...], at_ref[...], ex_ref[...], dk_ref[...], wn_ref[...], n_valid=n_valid)
        y_ref[u] = y.astype(y_ref.dtype)


def _ssd_sample(ext, dt, dtt, z, state, ssm_w, *, n_valid):
    bs, ext_rows, conv_dim = ext.shape
    lp = dt.shape[1]
    d_inner = z.shape[2]
    n_state = state.shape[2]
    wc, bc, dtb, dtbt, a_row, a_col, expand, dskip, wnorm = ssm_w
    body = functools.partial(_ssd_sample_body, conv_w=wc.shape[0], n_valid=n_valid)
    consts = [wc, bc, dtb, dtbt, a_row, a_col, expand, dskip, wnorm]
    per = math.gcd(bs, SSD_SAMPLE_SEQS)

    def seq_spec(rows, cols):
        return pl.BlockSpec((per, rows, cols), lambda s: (s, 0, 0))

    return pl.pallas_call(
        body,
        grid=(bs // per,),
        in_specs=[seq_spec(ext_rows, conv_dim), seq_spec(lp, LANES), seq_spec(LANES, lp), seq_spec(lp, d_inner),
                  seq_spec(d_inner, n_state)]
                 + [_const_spec(w.shape) for w in consts],
        out_specs=[seq_spec(lp, d_inner), seq_spec(d_inner, n_state)],
        out_shape=[jax.ShapeDtypeStruct((bs, lp, d_inner), bf16),
                   jax.ShapeDtypeStruct((bs, d_inner, n_state), f32)],
        compiler_params=_params(1),
        name="ssd_sample",
    )(ext, dt, dtt, z, state, *consts)


def _sattn_body(pt_ref, q_ref, kn_ref, vn_ref, bprev_ref, bcur_ref, ck_ref, cv_ref, o_ref,
                km_ref, ms_ref, ls_ref, os_ref, kbuf_ref, vbuf_ref, sem_ref, *, nbs, n_full, n_tok, ppb):
    n_pages = nbs * ppb
    g = pl.program_id(1)
    n_steps = pl.num_programs(1)
    n_rows = N_HEADS * SUBLANES
    d = ATTN_DIM

    step = pl.program_id(0) * n_steps + g
    total = pl.num_programs(0) * n_steps

    def page_copies(n):
        slot = n % PAGE_SLOTS
        seq = n // n_steps
        first = (n % n_steps) * n_pages
        out = []
        for t in range(n_pages):
            pid = pt_ref[seq, first + t]
            out.append(pltpu.make_async_copy(ck_ref.at[pid], kbuf_ref.at[slot, t], sem_ref.at[slot, t]))
            out.append(pltpu.make_async_copy(cv_ref.at[pid], vbuf_ref.at[slot, t], sem_ref.at[slot, n_pages + t]))
        return out

    @pl.when(step == 0)
    def _():
        for n in range(PAGE_SLOTS - 1):
            @pl.when(n < total)
            def _():
                for c in page_copies(n):
                    c.start()

    @pl.when(step + PAGE_SLOTS - 1 < total)
    def _():
        for c in page_copies(step + PAGE_SLOTS - 1):
            c.start()

    for c in page_copies(step):
        c.wait()
    slot = step % PAGE_SLOTS
    kp = [kbuf_ref.at[slot, t] for t in range(n_pages)]
    vp = [vbuf_ref.at[slot, t] for t in range(n_pages)]

    q4 = q_ref[...]
    q8 = jnp.concatenate([q4, jnp.zeros((SUBLANES - n_tok, d), f32)], axis=0)
    r_i = lax.broadcasted_iota(jnp.int32, (n_rows, d), 0)
    c_i = lax.broadcasted_iota(jnp.int32, (n_rows, d), 1)
    head_mask = (r_i // SUBLANES) == (c_i // HEAD_DIM)
    q_rows = jnp.where(head_mask, jnp.concatenate([q8] * N_HEADS, axis=0), 0.0)
    qb = q_rows.astype(bf16)

    @pl.when(g == 0)
    def _():
        km_ref[...] = jnp.zeros(km_ref.shape, f32)
        ms_ref[...] = jnp.zeros(ms_ref.shape, f32)
        ls_ref[...] = jnp.zeros(ls_ref.shape, f32)

    lane_blk = lax.broadcasted_iota(jnp.int32, (1, LANES), 1)
    sc_new = jnp.zeros((n_rows, LANES), f32)
    ms_new = jnp.zeros((n_rows, LANES), f32)
    ls_new = jnp.zeros((n_rows, LANES), f32)
    o_new = []
    k_t = jnp.concatenate([kp[t][...].reshape(d, -1).astype(bf16) for t in range(n_pages)], axis=1)
    s_all = _dot(qb, k_t)
    for b in range(nbs):
        vblk_t = jnp.concatenate([vp[b * ppb + t][...].reshape(d, -1) for t in range(ppb)], axis=1)
        jj = g * nbs + b
        s = s_all[:, b * MOBA_BLOCK:(b + 1) * MOBA_BLOCK]
        here = (lane_blk == jj).astype(f32)
        sc_new = sc_new + (jnp.sum(s, axis=-1, keepdims=True) * (1.0 / MOBA_BLOCK)) * here
        s = s + jnp.where(jj == n_full - 1, 1.0, 0.0).astype(f32) * bprev_ref[...]
        m = jnp.max(s, axis=-1, keepdims=True)
        p = jnp.exp(s - m)
        l = jnp.sum(p, axis=-1, keepdims=True)
        ms_new = ms_new + m * here
        ls_new = ls_new + l * here
        o_new.append(_dot_nt(p.astype(bf16), vblk_t.astype(bf16)))
    km_ref[...] += sc_new
    ms_ref[...] += ms_new
    ls_ref[...] += ls_new
    os_ref[pl.ds(pl.multiple_of(g * nbs, nbs), nbs)] = jnp.stack(o_new, axis=0)

    @pl.when(g == pl.num_programs(1) - 1)
    def _():
        sc = km_ref[...]
        col = lax.broadcasted_iota(jnp.int32, sc.shape, 1)
        low = jnp.float32(-3e38)
        sc = jnp.where(col < n_full, sc, low)
        self_ = jnp.zeros(sc.shape, f32)
        for _ in range(min(MOBA_TOPK, n_full)):
            mx = jnp.max(sc, axis=-1, keepdims=True)
            idx = jnp.min(jnp.where(sc == mx, col, LANES), axis=-1, keepdims=True)
            self_ = jnp.where(col == idx, 1.0, self_)
            sc = jnp.where(col == idx, low, sc)
        sel = self_ > 0.5
        kn = jnp.concatenate([kn_ref[...], jnp.zeros((SAMPLE_ROWS - n_tok, d), f32)], axis=0).astype(bf16)
        vn = jnp.concatenate([vn_ref[...], jnp.zeros((SAMPLE_ROWS - n_tok, d), f32)], axis=0).astype(bf16)
        s_cur = _dot_nt(qb, kn) + bcur_ref[...]
        ms = ms_ref[...]
        m_tot = jnp.maximum(jnp.max(s_cur, axis=-1, keepdims=True),
                            jnp.max(jnp.where(sel, ms, low), axis=-1, keepdims=True))
        p_cur = jnp.exp(s_cur - m_tot)
        w = jnp.where(sel, jnp.exp(ms - m_tot), 0.0)
        l_tot = jnp.sum(p_cur, axis=-1, keepdims=True) + jnp.sum(w * ls_ref[...], axis=-1, keepdims=True)
        acc = _dot(p_cur.astype(bf16), vn)
        for j in range(n_full):
            acc = acc + w[:, j:j + 1] * os_ref[j]
        out = jnp.where(head_mask, acc / l_tot, 0.0)
        out8 = out[0:SUBLANES]
        for h in range(1, N_HEADS):
            out8 = out8 + out[h * SUBLANES:(h + 1) * SUBLANES]
        o_ref[...] = out8[0:n_tok]


def _attn_sample(q, kn, vn, ck, cv, page_table, bprev, bcur):
    bs, n_tok, d = q.shape
    n_pool, n_heads, hd, page = ck.shape
    assert n_heads * hd == d
    ppb = MOBA_BLOCK // page
    n_pages_seq = page_table.shape[1]
    past = n_pages_seq * page
    assert MOBA_BLOCK % page == 0 and past % MOBA_BLOCK == 0 and n_tok <= SUBLANES
    n_full = past // MOBA_BLOCK
    assert n_full <= LANES
    nbs = math.gcd(n_full, SAMPLE_BLOCKS_PER_STEP)
    n_steps = n_full // nbs
    n_pages = nbs * ppb
    n_rows = N_HEADS * SUBLANES

    seq_spec = pl.BlockSpec((None, n_tok, d), lambda s, g, pt: (s, 0, 0))
    body = functools.partial(_sattn_body, nbs=nbs, n_full=n_full, n_tok=n_tok, ppb=ppb)
    grid_spec = pltpu.PrefetchScalarGridSpec(
        num_scalar_prefetch=1,
        grid=(bs, n_steps),
        in_specs=[seq_spec, seq_spec, seq_spec,
                  pl.BlockSpec(bprev.shape, lambda s, g, pt: (0, 0)),
                  pl.BlockSpec(bcur.shape, lambda s, g, pt: (0, 0)),
                  pl.BlockSpec(memory_space=pl.ANY), pl.BlockSpec(memory_space=pl.ANY)],
        out_specs=seq_spec,
        scratch_shapes=[pltpu.VMEM((n_rows, LANES), f32),
                        pltpu.VMEM((n_rows, LANES), f32),
                        pltpu.VMEM((n_rows, LANES), f32),
                        pltpu.VMEM((n_full, n_rows, d), f32),
                        pltpu.VMEM((PAGE_SLOTS, n_pages, n_heads, hd, page), f32),
                        pltpu.VMEM((PAGE_SLOTS, n_pages, n_heads, hd, page), f32),
                        pltpu.SemaphoreType.DMA((PAGE_SLOTS, 2 * n_pages))],
    )
    return pl.pallas_call(
        body,
        grid_spec=grid_spec,
        out_shape=jax.ShapeDtypeStruct((bs, n_tok, d), f32),
        compiler_params=_params(2),
        name="moba_sample",
    )(page_table, q, kn, vn, bprev, bcur, ck, cv)


def _merge_body(at_ref, yn_ref, ga_ref, gb_ref, x_ref, g1_ref, wpa_ref, wps_ref, wo_ref, lg_ref, lb_ref,
                o_ref, *, alpha):
    pa = _dot(at_ref[...].astype(bf16), wpa_ref[...])
    ps = _dot(yn_ref[...].astype(bf16), wps_ref[...])
    merged = jax.nn.sigmoid(ga_ref[...].astype(f32)) * pa + jax.nn.sigmoid(gb_ref[...].astype(f32)) * ps
    mo = _dot(merged.astype(bf16), wo_ref[...])
    o_ref[...] = _layer_norm(alpha * x_ref[...] + g1_ref[...] * mo, lg_ref[...], lb_ref[...])


def _mod_spec(per_row_mod, tm, d, tiles_per_seq):
    if per_row_mod:
        return pl.BlockSpec((tm, d), lambda i: (i, 0))
    return pl.BlockSpec((None, 1, d), lambda i: (i // tiles_per_seq, 0, 0))


def _merge(attn, yn, ga, gb, x2d, gate1, wpa, wps, wo, lg, lb, *, per_row_mod, seq_len, alpha):
    r, d = x2d.shape
    tm = 256
    assert r % tm == 0
    tiles_per_seq = 1 if per_row_mod else seq_len // tm
    ms = _mod_spec(per_row_mod, tm, d, tiles_per_seq)

    def row_spec(w):
        return pl.BlockSpec((tm, w), lambda i: (i, 0))

    return pl.pallas_call(
        functools.partial(_merge_body, alpha=alpha),
        grid=(r // tm,),
        in_specs=[row_spec(attn.shape[1]), row_spec(yn.shape[1]), row_spec(d), row_spec(d), row_spec(d), ms,
                  _const_spec(wpa.shape), _const_spec(wps.shape), _const_spec(wo.shape),
                  _const_spec(lg.shape), _const_spec(lb.shape)],
        out_specs=row_spec(d),
        out_shape=jax.ShapeDtypeStruct((r, d), f32),
        compiler_params=_params(1),
        name="merge_ln1",
    )(attn, yn, ga, gb, x2d, gate1, wpa, wps, wo, lg, lb)


FFN_COL_CHUNKS = 1


def _ffn_body(x_ref, sh_ref, sc_ref, g2_ref, wu_ref, wc_ref, bc_ref, wd_ref, lg_ref, lb_ref, *rest,
              alpha, conv_w, tiles_per_seq, sample_len):
    if sample_len is None:
        y_ref, tail_ref, buf_ref, carry_ref = rest
    else:
        p_refs = rest[:conv_w - 1]
        y_ref, hup_ref, buf_ref, carry_ref = rest[conv_w - 1:]
    x = x_ref[...]
    tm = x.shape[0]
    ff = wd_ref.shape[0]
    cw = ff // FFN_COL_CHUNKS
    t = pl.program_id(0) % tiles_per_seq

    @pl.when(t == 0)
    def _():
        carry_ref[...] = jnp.zeros(carry_ref.shape, f32)

    u = (x * (1.0 + sc_ref[...]) + sh_ref[...]).astype(bf16)
    if sample_len is not None:
        tmod = lax.broadcasted_iota(jnp.int32, (tm, cw), 0) % sample_len
    f = jnp.zeros((tm, x.shape[1]), f32)
    for c in range(FFN_COL_CHUNKS):
        halves = []
        for half in range(2):
            c0 = half * ff + c * cw
            buf_ref[0:SUBLANES, :] = carry_ref[:, c0:c0 + cw]
            hup = _dot(u, wu_ref[:, c0:c0 + cw])
            buf_ref[SUBLANES:SUBLANES + tm, :] = hup
            if sample_len is not None:
                hup_ref[:, c0:c0 + cw] = hup
            hc = bc_ref[:, c0:c0 + cw] + wc_ref[conv_w - 1:conv_w, c0:c0 + cw] * hup
            for k in range(conv_w - 1):
                back = conv_w - 1 - k
                prev = _rows_back(hup, buf_ref[0:tm, :], back)
                if sample_len is not None:
                    prev = jnp.where(tmod >= back, prev, p_refs[back - 1][:, c0:c0 + cw])
                hc = hc + wc_ref[k:k + 1, c0:c0 + cw] * prev
            carry_ref[:, c0:c0 + cw] = buf_ref[tm:tm + SUBLANES, :]
            halves.append(hc)
        gact = (_silu(halves[0]) * halves[1]).astype(bf16)
        f = f + _dot(gact, wd_ref[c * cw:(c + 1) * cw, :])
    y_ref[...] = _layer_norm(alpha * x + g2_ref[...] * f, lg_ref[...], lb_ref[...])
    if sample_len is None:
        @pl.when(t == tiles_per_seq - 1)
        def _():
            tail_ref[...] = carry_ref[...]


def _ffn(x2d, shift, scale, gate, wu, wc, bc, wd, lg, lb, prevs, *, per_row_mod, seq_len, bn, alpha,
         sample_len):
    r, d = x2d.shape
    ff2 = wu.shape[1]
    ff = wd.shape[0]
    conv_w = wc.shape[0]
    tm = 256 if sample_len is None else 128
    assert r % tm == 0 and ff % (FFN_COL_CHUNKS * LANES) == 0 and conv_w - 1 <= SUBLANES
    tiles_per_seq = 1 if per_row_mod else seq_len // tm
    ms = _mod_spec(per_row_mod, tm, d, tiles_per_seq)

    def row_spec(w):
        return pl.BlockSpec((tm, w), lambda i: (i, 0))

    in_specs = [row_spec(d), ms, ms, ms, _const_spec(wu.shape), _const_spec(wc.shape), _const_spec(bc.shape),
                _const_spec(wd.shape), _const_spec(lg.shape), _const_spec(lb.shape)]
    args = [x2d, shift, scale, gate, wu, wc, bc, wd, lg, lb]
    if sample_len is None:
        out_shape = [jax.ShapeDtypeStruct((r, d), f32), jax.ShapeDtypeStruct((bn, SUBLANES, ff2), f32)]
        out_specs = [row_spec(d), pl.BlockSpec((None, SUBLANES, ff2), lambda i: (i // tiles_per_seq, 0, 0))]
    else:
        assert tm % sample_len == 0 and len(prevs) == conv_w - 1
        in_specs += [row_spec(ff2)] * len(prevs)
        args += list(prevs)
        out_shape = [jax.ShapeDtypeStruct((r, d), f32), jax.ShapeDtypeStruct((r, ff2), f32)]
        out_specs = [row_spec(d), row_spec(ff2)]
    body = functools.partial(_ffn_body, alpha=alpha, conv_w=conv_w, tiles_per_seq=tiles_per_seq,
                             sample_len=sample_len)
    return pl.pallas_call(
        body,
        grid=(r // tm,),
        in_specs=in_specs,
        out_specs=out_specs,
        out_shape=out_shape,
        scratch_shapes=[pltpu.VMEM((tm + SUBLANES, ff // FFN_COL_CHUNKS), f32),
                        pltpu.VMEM((SUBLANES, ff2), f32)],
        compiler_params=_params(1),
        name="conv_ffn",
    )(*args)


def _rel_bucket(dist):
    n = jnp.maximum(dist, 0)
    max_exact = REL_BUCKETS // 2
    nf = jnp.maximum(n, 1).astype(f32)
    large = max_exact + (jnp.log(nf / max_exact) / math.log(REL_MAX_DIST / max_exact)
                         * (REL_BUCKETS - max_exact)).astype(jnp.int32)
    large = jnp.minimum(large, REL_BUCKETS - 1)
    return jnp.where(n < max_exact, n, large)


def _rel_bias(rel_table, dist):
    rel = rel_table - rel_table[REL_BUCKETS - 1]
    onehot = (_rel_bucket(dist)[..., None] == jnp.arange(REL_BUCKETS)).astype(f32)
    b = jnp.dot(onehot, rel, precision=lax.Precision.HIGHEST)
    return jnp.moveaxis(b, -1, 0)


def _prompt_bias_tables(rel_table):
    n = MOBA_BLOCK
    by_dist = _rel_bias(rel_table, jnp.arange(2 * n, dtype=jnp.int32)) * LOG2E
    masked = jnp.full((by_dist.shape[0], n - 1), MASKED, f32)
    bd = _toeplitz(jnp.concatenate([masked, by_dist[:, :n]], axis=1), n)
    bp = _toeplitz(by_dist[:, 1:], n)
    return bd.astype(f32), bp.astype(f32)


def _toeplitz(f, n):
    g = jnp.concatenate([f, jnp.zeros((f.shape[0], 1), f.dtype)], axis=1)
    y = jnp.tile(g, (1, n))[:, :n * (2 * n - 1)].reshape(f.shape[0], n, 2 * n - 1)
    return y[:, :, n - 1:]


def _sample_bias_tables(rel_table, n_tok):
    n_rows = N_HEADS * SUBLANES
    t = jnp.arange(SUBLANES, dtype=jnp.int32)
    a = jnp.arange(MOBA_BLOCK, dtype=jnp.int32)
    bprev = _rel_bias(rel_table, MOBA_BLOCK + t[:, None] - a[None, :])
    bprev = jnp.where((t < n_tok)[None, :, None], bprev, 0.0).reshape(n_rows, MOBA_BLOCK)
    tk = jnp.arange(SAMPLE_ROWS, dtype=jnp.int32)
    dist = t[:, None] - tk[None, :]
    ok = (dist >= 0) & (t[:, None] < n_tok) & (tk[None, :] < n_tok)
    bcur = jnp.where(ok[None], _rel_bias(rel_table, dist), MASKED).reshape(n_rows, SAMPLE_ROWS)
    return bprev.astype(f32), bcur.astype(f32)


def _pad_cols(w, n):
    return jnp.pad(w, ((0, 0), (0, n - w.shape[1])))


def kernel(x_prompt, x_sample, cache_k, cache_v, page_table, state_ssm, state_conv_ssm, state_conv_ffn,
           c_prompt, c_sample, rel_table, w_ada, b_ada, w_in, w_conv_ssm, b_conv_ssm, dt_bias, a_log,
           d_skip, w_norm_ssm, w_proj_attn, w_proj_ssm, w_out, ln1_g, ln1_b, w_up, w_conv_ffn, b_conv_ffn,
           w_down, ln2_g, ln2_b):
    depth = w_ada.shape[0]
    alpha = (2 * depth) ** 0.25
    bp_, seq, d = x_prompt.shape
    bs, n_tok, _ = x_sample.shape
    ssm_heads = dt_bias.shape[1]
    d_inner = ssm_heads * SSM_HEAD_DIM
    n_state = state_ssm.shape[-1]
    conv_dim = w_conv_ssm.shape[-1]
    ff2 = w_up.shape[-1]
    ssm_conv = w_conv_ssm.shape[1]
    ffn_conv = w_conv_ffn.shape[1]
    assert ssm_heads <= LANES and n_tok <= SUBLANES
    page = cache_k.shape[2]
    scale = HEAD_DIM ** -0.5

    bd, bpv = _prompt_bias_tables(rel_table)
    bprev_s, bcur_s = _sample_bias_tables(rel_table, n_tok)
    expand = (jnp.arange(LANES)[:, None] == (jnp.arange(d_inner)[None, :] // SSM_HEAD_DIM)).astype(bf16)

    yp = x_prompt.reshape(bp_ * seq, d)
    ys = x_sample.reshape(bs * n_tok, d)
    outs_p = [[] for _ in range(5)]
    outs_s = [[] for _ in range(5)]
    for l in range(depth):
        cuts = [ATTN_DIM, 2 * ATTN_DIM, 3 * ATTN_DIM, 3 * ATTN_DIM + d_inner,
                3 * ATTN_DIM + d_inner + conv_dim, 3 * ATTN_DIM + d_inner + conv_dim + ssm_heads,
                3 * ATTN_DIM + d_inner + conv_dim + ssm_heads + d]
        wq, wk, wv, wz, wxbc, wdt, wga, wgb = jnp.split(w_in[l], cuts, axis=1)
        wdt = _pad_cols(wdt, LANES)
        wq = wq * scale
        nat_rest = [(d_inner, bf16), (conv_dim, f32), (LANES, f32), (d, bf16), (d, bf16)]
        w_rest = [wz, wxbc, wdt, wga, wgb]
        nat_p = [(ATTN_DIM, None)] + nat_rest
        wn_p = jnp.concatenate([wk] + w_rest, axis=1).astype(bf16)
        a_ = ATTN_DIM
        tr_p = [(0, a_, bf16, 0, False), (a_, a_, f32, 0, True), (2 * a_, a_, f32, 0, True),
                (2 * a_, a_, bf16, BF16_ROWS, False), (3 * a_, LANES, f32, 0, False)]
        wt_p = jnp.concatenate([wq * LOG2E, wk, wv, wdt], axis=1).T.astype(bf16)
        nat_s = [(ATTN_DIM, f32)] * 3 + nat_rest
        wn_s = jnp.concatenate([wq, wk, wv] + w_rest, axis=1).astype(bf16)
        tr_s = [(0, LANES, f32, 0, False)]
        wt_s = wdt.T.astype(bf16)
        ssm_w = (w_conv_ssm[l], b_conv_ssm[l].reshape(1, conv_dim),
                 _pad_cols(dt_bias[l].reshape(1, -1), LANES), _pad_cols(dt_bias[l].reshape(1, -1), LANES).T,
                 -jnp.exp(_pad_cols(a_log[l].reshape(1, -1), LANES)),
                 -jnp.exp(_pad_cols(a_log[l].reshape(1, -1), LANES)).T,
                 expand, jnp.repeat(d_skip[l], SSM_HEAD_DIM).reshape(1, d_inner),
                 w_norm_ssm[l].reshape(1, d_inner))
        wpa, wps, wo = (w_proj_attn[l].astype(bf16), w_proj_ssm[l].astype(bf16), w_out[l].astype(bf16))
        lg1, lb1 = ln1_g[l].reshape(1, d), ln1_b[l].reshape(1, d)
        lg2, lb2 = ln2_g[l].reshape(1, d), ln2_b[l].reshape(1, d)
        wu, wd = w_up[l].astype(bf16), w_down[l].astype(bf16)
        wcf, bcf = w_conv_ffn[l], b_conv_ffn[l].reshape(1, ff2)

        mod = _ada(jnp.concatenate([c_prompt, c_sample], axis=0), w_ada[l], b_ada[l])
        mod_p = mod[:bp_].reshape(bp_, 6, 1, d)
        mod_s = jnp.repeat(mod[bp_:].reshape(bs, 6, 1, d), n_tok, axis=2).reshape(bs, 6, n_tok, d)
        mod_s = jnp.moveaxis(mod_s, 1, 0).reshape(6, bs * n_tok, d)
        sh1p, sc1p, g1p, sh2p, sc2p, g2p = (mod_p[:, i] for i in range(6))
        sh1s, sc1s, g1s, sh2s, sc2s, g2s = (mod_s[i] for i in range(6))

        z_p, xbc_p, dt_p, ga_p, gb_p, qT_p, kT_p, vTf_p, vT_p, dtT_p, k2_p = _inproj(
            yp, sh1p, sc1p, wn_p, wt_p, nat_p, tr_p, per_row_mod=False, seq_len=seq, attn_layout=True)
        attn_p = _attn_prompt(k2_p, qT_p, vT_p, bd, bpv, bn=bp_, seq_len=seq)
        yn_p, st_p = _ssd_prompt(xbc_p, dt_p, dtT_p, z_p, ssm_w, bn=bp_, seq_len=seq, n_state=n_state)
        x1_p = _merge(attn_p, yn_p, ga_p, gb_p, yp, g1p, wpa, wps, wo, lg1, lb1,
                      per_row_mod=False, seq_len=seq, alpha=alpha)
        yp, tail_p = _ffn(x1_p, sh2p, sc2p, g2p, wu, wcf, bcf, wd, lg2, lb2, (),
                          per_row_mod=False, seq_len=seq, bn=bp_, alpha=alpha, sample_len=None)
        outs_p[0].append(jnp.transpose(kT_p.reshape(bp_, N_HEADS, HEAD_DIM, seq), (0, 3, 1, 2)))
        outs_p[1].append(jnp.transpose(vTf_p.reshape(bp_, N_HEADS, HEAD_DIM, seq), (0, 3, 1, 2)))
        outs_p[2].append(st_p.reshape(bp_, ssm_heads, SSM_HEAD_DIM, n_state))
        outs_p[3].append(xbc_p.reshape(bp_, seq, conv_dim)[:, seq - (ssm_conv - 1):])
        outs_p[4].append(tail_p[:, SUBLANES - (ffn_conv - 1):])

        r_s = bs * n_tok
        r_pad = -(-r_s // 256) * 256

        def pad_rows(a):
            return jnp.pad(a, ((0, r_pad - r_s), (0, 0)))

        q_s, k_s, v_s, z_s, xbc_s, dt_s, ga_s, gb_s, dtT_s = _inproj(
            pad_rows(ys), pad_rows(sh1s), pad_rows(sc1s), wn_s, wt_s, nat_s, tr_s,
            per_row_mod=True, seq_len=None, attn_layout=False)
        q_s, k_s, v_s = (a[:r_s].reshape(bs, n_tok, ATTN_DIM) for a in (q_s, k_s, v_s))
        attn_s = _attn_sample(q_s, k_s, v_s, jnp.transpose(cache_k[l], (0, 2, 3, 1)),
                              jnp.transpose(cache_v[l], (0, 2, 3, 1)), page_table, bprev_s, bcur_s)
        row_pad = SAMPLE_ROWS - n_tok
        xbc_s3 = xbc_s[:r_s].reshape(bs, n_tok, conv_dim)
        ext = jnp.concatenate([state_conv_ssm[l], xbc_s3,
                               jnp.zeros((bs, row_pad + SUBLANES - (ssm_conv - 1), conv_dim), f32)], axis=1)
        dt_s3 = jnp.pad(dt_s[:r_s].reshape(bs, n_tok, LANES), ((0, 0), (0, row_pad), (0, 0)))
        dtT_s3 = jnp.pad(jnp.moveaxis(dtT_s[:, :r_s].reshape(LANES, bs, n_tok), 0, 1),
                         ((0, 0), (0, 0), (0, row_pad)))
        z_s3 = jnp.pad(z_s[:r_s].reshape(bs, n_tok, d_inner), ((0, 0), (0, row_pad), (0, 0)))
        yn_s3, st_s = _ssd_sample(ext, dt_s3, dtT_s3, z_s3, state_ssm[l].reshape(bs, d_inner, n_state),
                                  ssm_w, n_valid=n_tok)
        yn_s = yn_s3[:, :n_tok].reshape(r_s, d_inner)
        x1_s = _merge(pad_rows(attn_s.reshape(r_s, ATTN_DIM)), pad_rows(yn_s), ga_s, gb_s, pad_rows(ys),
                      pad_rows(g1s), wpa, wps, wo, lg1, lb1, per_row_mod=True, seq_len=None, alpha=alpha)
        cf = state_conv_ffn[l]
        prevs = []
        for back in range(1, ffn_conv):
            rows = [cf[:, ffn_conv - 1 - back + t] if t < back else jnp.zeros((bs, ff2), f32)
                    for t in range(n_tok)]
            prevs.append(pad_rows(jnp.stack(rows, axis=1).reshape(r_s, ff2)))
        y_s, hup_s = _ffn(x1_s, pad_rows(sh2s), pad_rows(sc2s), pad_rows(g2s), wu, wcf, bcf, wd, lg2, lb2,
                          prevs, per_row_mod=True, seq_len=None, bn=bs, alpha=alpha, sample_len=n_tok)
        ys = y_s[:r_s]
        outs_s[0].append(k_s.reshape(bs, n_tok, N_HEADS, HEAD_DIM))
        outs_s[1].append(v_s.reshape(bs, n_tok, N_HEADS, HEAD_DIM))
        outs_s[2].append(st_s.reshape(bs, ssm_heads, SSM_HEAD_DIM, n_state))
        cs_ext = jnp.concatenate([state_conv_ssm[l], xbc_s3], axis=1)
        outs_s[3].append(cs_ext[:, -(ssm_conv - 1):])
        cf_ext = jnp.concatenate([cf, hup_s[:r_s].reshape(bs, n_tok, ff2)], axis=1)
        outs_s[4].append(cf_ext[:, -(ffn_conv - 1):])

    return (yp.reshape(bp_, seq, d), ys.reshape(bs, n_tok, d),
            *(jnp.stack(o) for o in outs_p), *(jnp.stack(o) for o in outs_s))
```
